```python
import functools
import jax, jax.numpy as jnp
from jax import lax
import numpy as np

D_MODEL = 1024
BATCH = 16
SEQ = 2048
DEPTH = 1
DEC_BATCH = 128
DEC_SEQ = 1
PAST_LEN = 16384
PAGE_SIZE = 128

HEAD_DIM = 64
A_Q_HEADS = 8
A_KV_HEADS = 2
A_GROUP = A_Q_HEADS // A_KV_HEADS
A_WINDOW = 128
B_PATTERNS = ((128, 1), (512, 4), (2048, 16))
B_N_GROUPS = 3
B_HEADS_PER_GROUP = 4
B_HEADS = B_N_GROUPS * B_HEADS_PER_GROUP
WIN_UNITS = 128
BLOCK = WIN_UNITS
D_FF = 2816
EPS = 1e-6
ATTN_SCALE = HEAD_DIM ** -0.5
N_ALIBI_HEADS = A_Q_HEADS + B_HEADS
A_W = A_Q_HEADS * HEAD_DIM
A_KVW = A_KV_HEADS * HEAD_DIM
B_W = B_HEADS * HEAD_DIM
B_OUT_W = B_HEADS_PER_GROUP * HEAD_DIM
IN_W = A_W + 2 * A_KVW + 3 * B_W + 2 * D_MODEL

kernel_name = "gated_parallel_swa_sink_dilated_macaron_step"


def rmsnorm(x, g):
    xf = x.astype(jnp.float32)
    y = xf * lax.rsqrt(jnp.mean(xf * xf, axis=-1, keepdims=True) + EPS)
    return (y * g.astype(jnp.float32)).astype(x.dtype)


def swiglu(x, w_gate, w_up, w_down):
    return (jax.nn.silu(x @ w_gate) * (x @ w_up)) @ w_down


def alibi_slopes():
    i = jnp.arange(1, N_ALIBI_HEADS + 1, dtype=jnp.float32)
    return jnp.exp2(-8.0 * i / N_ALIBI_HEADS)


def a_slopes(slopes):
    return slopes[:A_Q_HEADS].reshape(A_KV_HEADS, A_GROUP)


def b_slopes(slopes, g):
    lo = A_Q_HEADS + g * B_HEADS_PER_GROUP
    return slopes[lo:lo + B_HEADS_PER_GROUP].reshape(B_HEADS_PER_GROUP, 1)


def pad_seq(t, length):
    return jnp.pad(t, [(0, 0), (0, length - t.shape[1])] + [(0, 0)] * (t.ndim - 2))


def fold(t, d):
    n, sp = t.shape[:2]
    t = jnp.moveaxis(t.reshape(n, sp // d, d, *t.shape[2:]), 2, 1)
    return t.reshape(n * d, sp // d, *t.shape[3:])


def unfold(t, d, n):
    u = t.shape[1]
    t = jnp.moveaxis(t.reshape(n, d, u, *t.shape[2:]), 1, 2)
    return t.reshape(n, u * d, *t.shape[3:])


def attend(s, mask, sink):
    s = jnp.where(mask, s, -jnp.inf)
    m = jnp.max(s, axis=-1)
    if sink is not None:
        sink = sink.astype(jnp.float32)
        m = jnp.maximum(m, sink)
    p = jnp.exp(s - m[..., None])
    denom = jnp.sum(p, axis=-1)
    if sink is not None:
        denom = denom + jnp.exp(sink - m)
    return p / denom[..., None], m + jnp.log(denom)


def banded_window_attention(q, k, v, slopes, dil, sink):
    n, length, kvh, grp, hd = q.shape
    nb = length // BLOCK
    qb = q.reshape(n, nb, BLOCK, kvh, grp, hd)
    kb = k.reshape(n, nb, BLOCK, kvh, hd)
    vb = v.reshape(n, nb, BLOCK, kvh, hd)
    shift = lambda t: jnp.concatenate([jnp.zeros_like(t[:, :1]), t[:, :-1]], axis=1)
    kk = jnp.concatenate([shift(kb), kb], axis=2)
    vv = jnp.concatenate([shift(vb), vb], axis=2)
    s = jnp.einsum('nbqkgd,nbskd->nbkgqs', qb, kk, preferred_element_type=jnp.float32) * ATTN_SCALE
    qi = jnp.arange(BLOCK)[:, None] + BLOCK
    ki = jnp.arange(2 * BLOCK)[None, :]
    dist = qi - ki
    band = (dist >= 0) & (dist <= WIN_UNITS)
    key_ok = (jnp.arange(nb)[:, None] * BLOCK - BLOCK + ki) >= 0
    mask = (band[None] & key_ok[:, None, :])[None, :, None, None]
    s = s - slopes.astype(jnp.float32)[:, :, None, None] * (dil * dist).astype(jnp.float32)
    p, lse = attend(s, mask, None if sink is None else sink[:, :, None])
    o = jnp.einsum('nbkgqs,nbskd->nbqkgd', p.astype(v.dtype), vv)
    lse = jnp.transpose(lse, (0, 1, 4, 2, 3)).reshape(n, length, kvh, grp)
    return o.reshape(n, length, kvh, grp, hd), lse


def gathered_window_attention(q, k_all, v_all, q_off, dil, slopes, sink):
    t = q.shape[1]
    mstep = jnp.arange(WIN_UNITS + 1)
    idx = (q_off + jnp.arange(t))[:, None] - dil * mstep[None, :]
    valid = idx >= 0
    idx = jnp.maximum(idx, 0)
    kg = jnp.take(k_all, idx, axis=1)
    vg = jnp.take(v_all, idx, axis=1)
    s = jnp.einsum('ntkgd,ntmkd->ntkgm', q, kg, preferred_element_type=jnp.float32) * ATTN_SCALE
    s = s - slopes.astype(jnp.float32)[:, :, None] * (dil * mstep).astype(jnp.float32)
    p, lse = attend(s, valid[None, :, None, None, :], sink)
    o = jnp.einsum('ntkgm,ntmkd->ntkgd', p.astype(v_all.dtype), vg)
    return o, lse


def project_heads(h, w_in, q_norm_a, k_norm_a, q_norm_b, k_norm_b):
    n, t, _ = h.shape
    z = h @ w_in
    cuts = [A_W, A_W + A_KVW, A_W + 2 * A_KVW, A_W + 2 * A_KVW + B_W,
            A_W + 2 * A_KVW + 2 * B_W, A_W + 2 * A_KVW + 3 * B_W,
            A_W + 2 * A_KVW + 3 * B_W + D_MODEL]
    qa, ka, va, qb, kb, vb, ga, gb = jnp.split(z, cuts, axis=-1)
    qa = rmsnorm(qa.reshape(n, t, A_KV_HEADS, A_GROUP, HEAD_DIM), q_norm_a)
    ka = rmsnorm(ka.reshape(n, t, A_KV_HEADS, HEAD_DIM), k_norm_a)
    va = va.reshape(n, t, A_KV_HEADS, HEAD_DIM)
    qb = rmsnorm(qb.reshape(n, t, B_N_GROUPS, B_HEADS_PER_GROUP, HEAD_DIM), q_norm_b)
    kb = rmsnorm(kb.reshape(n, t, B_N_GROUPS, B_HEADS_PER_GROUP, HEAD_DIM), k_norm_b)
    vb = vb.reshape(n, t, B_N_GROUPS, B_HEADS_PER_GROUP, HEAD_DIM)
    return qa, ka, va, qb, kb, vb, ga, gb


def merge(o_a, o_b, lse_b, ga, gb, w_up_a, w_up_b, w_o):
    n, t = o_a.shape[:2]
    wts = jax.nn.softmax(lse_b, axis=2)
    ob = jnp.sum(wts[..., None] * o_b.astype(jnp.float32), axis=2).astype(o_b.dtype)
    ua = o_a.reshape(n, t, A_W) @ w_up_a
    ub = ob.reshape(n, t, B_OUT_W) @ w_up_b
    return (jax.nn.sigmoid(ga) * ua + jax.nn.sigmoid(gb) * ub) @ w_o


def mixer_prompt(h, proj, sinks, outp, slopes):
    qa, ka, va, qb, kb, vb, ga, gb = project_heads(h, *proj)
    n, s_len = h.shape[:2]
    sa = -(-s_len // BLOCK) * BLOCK
    o_a, _ = banded_window_attention(pad_seq(qa, sa), pad_seq(ka, sa), pad_seq(va, sa),
                                     a_slopes(slopes), 1, sinks)
    o_a = o_a[:, :s_len]
    states = [jnp.stack([ka, va], axis=2)[:, s_len - min(A_WINDOW, s_len):]]
    o_b, lse_b = [], []
    for g, (win, dil) in enumerate(B_PATTERNS):
        sp = -(-s_len // (dil * BLOCK)) * dil * BLOCK
        q = fold(pad_seq(qb[:, :, g, :, None], sp), dil)
        k = fold(pad_seq(kb[:, :, g], sp), dil)
        v = fold(pad_seq(vb[:, :, g], sp), dil)
        o, lse = banded_window_attention(q, k, v, b_slopes(slopes, g), dil, None)
        o_b.append(unfold(o, dil, n)[:, :s_len, :, 0])
        lse_b.append(unfold(lse, dil, n)[:, :s_len, :, 0])
        states.append(jnp.stack([kb[:, :, g], vb[:, :, g]], axis=2)[:, s_len - min(win, s_len):])
    y = merge(o_a, jnp.stack(o_b, axis=2), jnp.stack(lse_b, axis=2), ga, gb, *outp)
    return y, states


def mixer_step(h, bufs, proj, sinks, outp, slopes):
    qa, ka, va, qb, kb, vb, ga, gb = project_heads(h, *proj)
    t = h.shape[1]

    def extend(buf, k, v, win):
        rows = jnp.concatenate([buf, jnp.stack([k, v], axis=2)], axis=1)
        return rows[:, :, 0], rows[:, :, 1], rows[:, -min(win, buf.shape[1] + t):]

    k_all, v_all, new_a = extend(bufs[0], ka, va, A_WINDOW)
    o_a, _ = gathered_window_attention(qa, k_all, v_all, bufs[0].shape[1], 1, a_slopes(slopes), sinks)
    states = [new_a]
    o_b, lse_b = [], []
    for g, (win, dil) in enumerate(B_PATTERNS):
        buf = bufs[1 + g]
        k_all, v_all, new_b = extend(buf, kb[:, :, g], vb[:, :, g], win)
        o, lse = gathered_window_attention(qb[:, :, g, :, None], k_all, v_all, buf.shape[1], dil,
                                           b_slopes(slopes, g), None)
        o_b.append(o[:, :, :, 0])
        lse_b.append(lse[..., 0])
        states.append(new_b)
    y = merge(o_a, jnp.stack(o_b, axis=2), jnp.stack(lse_b, axis=2), ga, gb, *outp)
    return y, states


def macaron_layer(x, mixer, norm_ffn1, w1_gate, w1_up, w1_down, norm_mix,
                  norm_ffn2, w2_gate, w2_up, w2_down):
    x = x + 0.5 * swiglu(rmsnorm(x, norm_ffn1), w1_gate, w1_up, w1_down)
    mixed, states = mixer(rmsnorm(x, norm_mix))
    x = x + mixed
    x = x + 0.5 * swiglu(rmsnorm(x, norm_ffn2), w2_gate, w2_up, w2_down)
    return x, states


def setup_inputs(seed: int = 0) -> dict:
    key = jax.random.key(seed)
    ks = iter(jax.random.split(key, 32))
    f32 = jnp.float32

    def nrm(shape, scale=1.0):
        return scale * jax.random.normal(next(ks), shape, f32)

    def gain(shape):
        return 1.0 + 0.02 * jax.random.normal(next(ks), shape, f32)

    def kv_buf(window, heads):
        return nrm((DEPTH, DEC_BATCH, min(window, PAST_LEN), 2, heads, HEAD_DIM))

    return {
        "x_prompt": nrm((BATCH, SEQ, D_MODEL)),
        "x_sample": nrm((DEC_BATCH, DEC_SEQ, D_MODEL)),
        "cache_a_kv": kv_buf(A_WINDOW, A_KV_HEADS),
        "cache_b1_kv": kv_buf(B_PATTERNS[0][0], B_HEADS_PER_GROUP),
        "cache_b2_kv": kv_buf(B_PATTERNS[1][0], B_HEADS_PER_GROUP),
        "cache_b3_kv": kv_buf(B_PATTERNS[2][0], B_HEADS_PER_GROUP),
        "norm_ffn1": gain((DEPTH, D_MODEL)),
        "w1_gate": nrm((DEPTH, D_MODEL, D_FF), D_MODEL ** -0.5),
        "w1_up": nrm((DEPTH, D_MODEL, D_FF), D_MODEL ** -0.5),
        "w1_down": nrm((DEPTH, D_FF, D_MODEL), D_FF ** -0.5),
        "norm_mix": gain((DEPTH, D_MODEL)),
        "w_in": nrm((DEPTH, D_MODEL, IN_W), D_MODEL ** -0.5),
        "q_norm_a": gain((DEPTH, HEAD_DIM)),
        "k_norm_a": gain((DEPTH, HEAD_DIM)),
        "q_norm_b": gain((DEPTH, HEAD_DIM)),
        "k_norm_b": gain((DEPTH, HEAD_DIM)),
        "sinks_a": nrm((DEPTH, A_KV_HEADS, A_GROUP), 0.5),
        "w_up_a": nrm((DEPTH, A_W, D_MODEL), A_W ** -0.5),
        "w_up_b": nrm((DEPTH, B_OUT_W, D_MODEL), B_OUT_W ** -0.5),
        "w_o": nrm((DEPTH, D_MODEL, D_MODEL), D_MODEL ** -0.5),
        "norm_ffn2": gain((DEPTH, D_MODEL)),
        "w2_gate": nrm((DEPTH, D_MODEL, D_FF), D_MODEL ** -0.5),
        "w2_up": nrm((DEPTH, D_MODEL, D_FF), D_MODEL ** -0.5),
        "w2_down": nrm((DEPTH, D_FF, D_MODEL), D_FF ** -0.5),
    }


def reference(x_prompt, x_sample, cache_a_kv, cache_b1_kv, cache_b2_kv, cache_b3_kv,
              norm_ffn1, w1_gate, w1_up, w1_down, norm_mix, w_in,
              q_norm_a, k_norm_a, q_norm_b, k_norm_b, sinks_a,
              w_up_a, w_up_b, w_o, norm_ffn2, w2_gate, w2_up, w2_down):
    slopes = alibi_slopes()
    yp, ys = x_prompt, x_sample
    st_p = ([], [], [], [])
    st_s = ([], [], [], [])
    for l in range(DEPTH):
        proj = (w_in[l], q_norm_a[l], k_norm_a[l], q_norm_b[l], k_norm_b[l])
        outp = (w_up_a[l], w_up_b[l], w_o[l])
        ffn = (norm_ffn1[l], w1_gate[l], w1_up[l], w1_down[l], norm_mix[l],
               norm_ffn2[l], w2_gate[l], w2_up[l], w2_down[l])
        bufs = (cache_a_kv[l], cache_b1_kv[l], cache_b2_kv[l], cache_b3_kv[l])
        yp, sp = macaron_layer(yp, functools.partial(mixer_prompt, proj=proj, sinks=sinks_a[l],
                                                     outp=outp, slopes=slopes), *ffn)
        ys, ss = macaron_layer(ys, functools.partial(mixer_step, bufs=bufs, proj=proj, sinks=sinks_a[l],
                                                     outp=outp, slopes=slopes), *ffn)
        for i in range(4):
            st_p[i].append(sp[i])
            st_s[i].append(ss[i])
    a_p, b1_p, b2_p, b3_p = [jnp.stack(s) for s in st_p]
    a_s, b1_s, b2_s, b3_s = [jnp.stack(s) for s in st_s]
    return (yp, ys, a_p, b1_p, b2_p, b3_p, a_s, b1_s, b2_s, b3_s)
```

```python
import functools
import math

import jax
import jax.numpy as jnp
from jax import lax
from jax.experimental import pallas as pl
from jax.experimental.pallas import tpu as pltpu

D_MODEL = 1024
D_FF = 2816
HEAD_DIM = 64
A_Q_HEADS = 8
A_KV_HEADS = 2
A_GROUP = A_Q_HEADS // A_KV_HEADS
B_PATTERNS = ((128, 1), (512, 4), (2048, 16))
B_HEADS_PER_GROUP = 4
B_N_GROUPS = 3
N_ALIBI_HEADS = A_Q_HEADS + B_N_GROUPS * B_HEADS_PER_GROUP
BLOCK = 128
EPS = 1e-6
ATTN_SCALE = HEAD_DIM ** -0.5
A_W = A_Q_HEADS * HEAD_DIM
A_KVW = A_KV_HEADS * HEAD_DIM
B_GW = B_HEADS_PER_GROUP * HEAD_DIM
B_W = B_N_GROUPS * B_GW
QKV_W = A_W + 2 * A_KVW + 3 * B_W
IN_W = QKV_W + 2 * D_MODEL

LANES = 128
CHUNK = 256
NEG = -1e30
VMEM_LIMIT = 56 * 1024 * 1024

F32 = jnp.float32
BF16 = jnp.bfloat16


def _const_spec(shape):
    nd = len(shape)
    return pl.BlockSpec(shape, lambda *_: (0,) * nd, pipeline_mode=pl.Buffered(1))


def _smem_spec():
    return pl.BlockSpec(memory_space=pltpu.SMEM)


def _params(n_axes):
    return pltpu.CompilerParams(
        dimension_semantics=("arbitrary",) * n_axes, vmem_limit_bytes=VMEM_LIMIT)


def _rmsnorm(x, g):
    return x * lax.rsqrt(jnp.mean(x * x, axis=-1, keepdims=True) + EPS) * g


def _dot(a, b):
    return jnp.dot(a, b, preferred_element_type=F32)


def _dot_nt(a, b):
    return lax.dot_general(a, b, (((1,), (1,)), ((), ())), preferred_element_type=F32)


def _swiglu_residual(x, gain, wg_ref, wu_ref, wd_ref, act_ref):
    h = _rmsnorm(x, gain).astype(BF16)
    for c in range(D_FF // CHUNK):
        sl = slice(c * CHUNK, (c + 1) * CHUNK)
        g = _dot(h, wg_ref[:, sl])
        u = _dot(h, wu_ref[:, sl])
        act_ref[:, sl] = (g * jax.nn.sigmoid(g) * u).astype(BF16)
    return x + 0.5 * _dot(act_ref[...], wd_ref[...])


def _ffn_kernel(x_ref, gain_ref, wg_ref, wu_ref, wd_ref, o_ref, act_ref):
    o_ref[...] = _swiglu_residual(x_ref[...], gain_ref[...], wg_ref, wu_ref, wd_ref, act_ref)


def _ffn(x, gain, wg, wu, wd, tm):
    t = x.shape[0]
    return pl.pallas_call(
        _ffn_kernel,
        grid=(t // tm,),
        in_specs=[pl.BlockSpec((tm, D_MODEL), lambda i: (i, 0)),
                  _const_spec((1, D_MODEL)), _const_spec((D_MODEL, D_FF)),
                  _const_spec((D_MODEL, D_FF)), _const_spec((D_FF, D_MODEL))],
        out_specs=pl.BlockSpec((tm, D_MODEL), lambda i: (i, 0)),
        out_shape=jax.ShapeDtypeStruct((t, D_MODEL), F32),
        scratch_shapes=[pltpu.VMEM((tm, D_FF), BF16)],
        compiler_params=_params(1),
        name="ffn1",
    )(x, gain, wg, wu, wd)


_N_QKV_CHUNKS = QKV_W // CHUNK
_N_CHUNKS = IN_W // CHUNK


def _tail_plan(tail, tm, n_tiles):
    if tail >= tm:
        return tm, n_tiles - tail // tm
    return tail, n_tiles - 1


def _proj_kernel(x_ref, gain_ref, w_ref, qkgain_ref, ones_ref,
                 qa_ref, kva_ref, b0_ref, b12_ref, gates_ref, ta_ref, tb0_ref, tb1_ref, tb2_ref,
                 *, tm, n_tiles, tails):
    j = pl.program_id(1)
    h = _rmsnorm(x_ref[...], gain_ref[...]).astype(BF16)
    lane = lax.broadcasted_iota(jnp.int32, (tm, CHUNK), 1)
    lane1 = lax.broadcasted_iota(jnp.int32, (tm, LANES), 1)

    def z_chunk(c):
        return _dot(h, w_ref[:, c * CHUNK:(c + 1) * CHUNK])

    def head_norm(z, c):
        zz = z * z
        hi = zz.astype(BF16)
        lo = (zz - hi.astype(F32)).astype(BF16)
        ss = _dot(hi, ones_ref[...]) + _dot(lo, ones_ref[...])
        return z * lax.rsqrt(ss * (1.0 / HEAD_DIM) + EPS) * qkgain_ref[:, c * CHUNK:(c + 1) * CHUNK]

    def write_tail(t_ref, row0, y, tail):
        block_w, first = _tail_plan(tail, tm, n_tiles)
        data = y if block_w == tm else y[tm - block_w:, :]

        def store():
            t_ref[row0:row0 + CHUNK, :] = data.T

        if first == 0:
            store()
        else:
            pl.when(j >= first)(store)

    for c in range(2):
        qa_ref[:, c * CHUNK:(c + 1) * CHUNK] = head_norm(z_chunk(c), c).astype(BF16)

    z = z_chunk(2)
    y = jnp.where(lane < A_KVW, head_norm(z, 2), z)
    write_tail(ta_ref, 0, y, tails[0])
    for part in range(2):
        pair = y[:, part * LANES:(part + 1) * LANES]
        swapped = pltpu.roll(pair, HEAD_DIM, axis=1)
        base = part * 2 * LANES
        kva_ref[:, base:base + LANES] = jnp.where(lane1 < HEAD_DIM, pair, swapped).astype(BF16)
        kva_ref[:, base + LANES:base + 2 * LANES] = jnp.where(lane1 < HEAD_DIM, swapped, pair).astype(BF16)

    tb_refs = (tb0_ref, tb1_ref, tb2_ref)
    for kind in range(3):
        for g in range(B_N_GROUPS):
            c = 3 + 3 * kind + g
            z = z_chunk(c)
            y = z if kind == 2 else head_norm(z, c)
            if g == 0:
                b0_ref[:, kind * CHUNK:(kind + 1) * CHUNK] = y.astype(BF16)
            else:
                col = ((g - 1) * 3 + kind) * CHUNK
                b12_ref[:, col:col + CHUNK] = y
            if kind > 0:
                write_tail(tb_refs[g], (kind - 1) * CHUNK, y, tails[1 + g])

    for c in range(_N_QKV_CHUNKS, _N_CHUNKS):
        col = (c - _N_QKV_CHUNKS) * CHUNK
        gates_ref[:, col:col + CHUNK] = jax.nn.sigmoid(z_chunk(c)).astype(BF16)


def _proj(x, gain, w_in, qkgain, ones_bd, n_seq, seq, tm):
    t = n_seq * seq
    n_tiles = seq // tm
    tails = (min(128, seq),) + tuple(min(w, seq) for w, _ in B_PATTERNS)
    tail_rows = (2 * A_KVW, 2 * B_GW, 2 * B_GW, 2 * B_GW)

    def tail_spec(rows, tail):
        block_w, first = _tail_plan(tail, tm, n_tiles)
        return pl.BlockSpec((None, rows, block_w), lambda n, j: (n, 0, jnp.maximum(j - first, 0)))

    row_spec = lambda w: pl.BlockSpec((tm, w), lambda n, j: (n * n_tiles + j, 0))
    out_shape = [
        jax.ShapeDtypeStruct((t, A_W), BF16),
        jax.ShapeDtypeStruct((t, 4 * LANES), BF16),
        jax.ShapeDtypeStruct((t, 3 * B_GW), BF16),
        jax.ShapeDtypeStruct((t, 6 * B_GW), F32),
        jax.ShapeDtypeStruct((t, 2 * D_MODEL), BF16),
    ] + [jax.ShapeDtypeStruct((n_seq, r, tl), F32) for r, tl in zip(tail_rows, tails)]
    out_specs = [row_spec(A_W), row_spec(4 * LANES), row_spec(3 * B_GW), row_spec(6 * B_GW),
                 row_spec(2 * D_MODEL)] + [tail_spec(r, tl) for r, tl in zip(tail_rows, tails)]
    return pl.pallas_call(
        functools.partial(_proj_kernel, tm=tm, n_tiles=n_tiles, tails=tails),
        grid=(n_seq, n_tiles),
        in_specs=[row_spec(D_MODEL), _const_spec((1, D_MODEL)), _const_spec((D_MODEL, IN_W)),
                  _const_spec((1, QKV_W)), _const_spec((CHUNK, CHUNK))],
        out_specs=out_specs,
        out_shape=out_shape,
        compiler_params=_params(2),
        name="proj",
    )(x, gain, w_in, qkgain, ones_bd)


def _attn_kernel(*refs, n_pairs, k_pair, dil, seq, head0, has_sink, has_lse):
    refs = list(refs)
    slopes_ref = refs.pop(0)
    sinks_ref = refs.pop(0) if has_sink else None
    n_kv = max(k_pair) + 1
    take = lambda count: [refs.pop(0) for _ in range(count)]
    q_refs, k_refs, v_refs, o_refs = take(n_pairs), take(n_kv), take(n_kv), take(n_pairs)
    lse_refs = take(n_pairs) if has_lse else None

    n_blocks = seq // (BLOCK * dil)
    qi = lax.broadcasted_iota(jnp.int32, (BLOCK, BLOCK), 0)
    ki = lax.broadcasted_iota(jnp.int32, (BLOCK, BLOCK), 1)
    d_cur = qi - ki
    cur_ok = d_cur >= 0
    prev_ok = d_cur <= 0
    f_cur = (dil * d_cur).astype(F32)
    f_prev = (dil * (d_cur + BLOCK)).astype(F32)
    lo = ki < HEAD_DIM

    def body(step, carry):
        if n_blocks == 1:
            r, j = step, 0
        else:
            r, j = step // n_blocks, step % n_blocks
        if dil == 1:
            cur = pl.ds(pl.multiple_of(j * BLOCK, BLOCK), BLOCK)
            prev = pl.ds(pl.multiple_of(jnp.maximum(j - 1, 0) * BLOCK, BLOCK), BLOCK)
        else:
            cur = pl.ds(r + j * (BLOCK * dil), BLOCK, stride=dil)
            prev = pl.ds(r + jnp.maximum(j - 1, 0) * (BLOCK * dil), BLOCK, stride=dil)
        use_prev = n_blocks > 1
        if use_prev:
            first_block_penalty = jnp.where(j == 0, NEG, 0.0).astype(F32)

        for p in range(n_pairs):
            k_ref, v_ref = k_refs[k_pair[p]], v_refs[k_pair[p]]
            q = q_refs[p][cur, :].astype(BF16)
            k_c = k_ref[cur, :].astype(BF16)
            v_c = v_ref[cur, :].astype(BF16)
            if use_prev:
                k_p = k_ref[prev, :].astype(BF16)
                v_p = v_ref[prev, :].astype(BF16)
            outs, lses = [], []
            for half in range(2):
                head = 2 * p + half
                slope = slopes_ref[head0 + head]
                qm = jnp.where(lo if half == 0 else jnp.logical_not(lo), q, jnp.zeros_like(q))
                s_c = jnp.where(cur_ok, _dot_nt(qm, k_c) * ATTN_SCALE - slope * f_cur, NEG)
                m = jnp.max(s_c, axis=-1, keepdims=True)
                if use_prev:
                    s_p = jnp.where(prev_ok, _dot_nt(qm, k_p) * ATTN_SCALE - slope * f_prev, NEG)
                    s_p = s_p + first_block_penalty
                    m = jnp.maximum(m, jnp.max(s_p, axis=-1, keepdims=True))
                if has_sink:
                    sink = sinks_ref[head]
                    m = jnp.maximum(m, sink)
                e_c = jnp.exp(s_c - m)
                denom = jnp.sum(e_c, axis=-1, keepdims=True)
                acc = _dot(e_c.astype(BF16), v_c)
                if use_prev:
                    e_p = jnp.exp(s_p - m)
                    denom = denom + jnp.sum(e_p, axis=-1, keepdims=True)
                    acc = acc + _dot(e_p.astype(BF16), v_p)
                if has_sink:
                    denom = denom + jnp.exp(sink - m)
                outs.append(acc / denom)
                lses.append(m + jnp.log(denom))
            o_refs[p][cur, :] = jnp.where(lo, outs[0], outs[1]).astype(o_refs[p].dtype)
            if has_lse:
                lse_refs[p][cur, :] = jnp.where(lo, lses[0], lses[1])
        return carry

    lax.fori_loop(0, dil * n_blocks, body, 0)


def _attn(slopes, sinks, q_arr, k_arr, v_arr, q_idx, k_idx, v_idx, *, n_seq, seq, k_pair, dil, head0,
          out_dtype, has_lse, name):
    has_sink = sinks is not None
    n_pairs = len(q_idx)

    def pair_spec(idx):
        return pl.BlockSpec((None, seq, LANES), lambda n: (n, 0, idx))

    view = lambda a: a.reshape(n_seq, seq, a.shape[-1])
    operands = [slopes] + ([sinks] if has_sink else [])
    operands += [view(q_arr)] * n_pairs + [view(k_arr)] * len(k_idx) + [view(v_arr)] * len(v_idx)
    in_specs = [_smem_spec()] * (2 if has_sink else 1)
    in_specs += [pair_spec(i) for i in tuple(q_idx) + tuple(k_idx) + tuple(v_idx)]
    out_shape = [jax.ShapeDtypeStruct((n_seq, seq, LANES), out_dtype)] * n_pairs
    if has_lse:
        out_shape += [jax.ShapeDtypeStruct((n_seq, seq, LANES), F32)] * n_pairs
    outs = pl.pallas_call(
        functools.partial(_attn_kernel, n_pairs=n_pairs, k_pair=k_pair, dil=dil, seq=seq, head0=head0,
                          has_sink=has_sink, has_lse=has_lse),
        grid=(n_seq,),
        in_specs=in_specs,
        out_specs=[pair_spec(0)] * len(out_shape),
        out_shape=out_shape,
        compiler_params=_params(1),
        name=name,
    )(*operands)
    outs = [o.reshape(n_seq * seq, LANES) for o in outs]
    return outs[:n_pairs], outs[n_pairs:]


def _rows_where(row_iota, values):
    out = jnp.zeros(row_iota.shape, F32)
    for r, v in enumerate(values):
        out = jnp.where(row_iota == r, v, out)
    return out


def _shift_cache(cache_ref, tail_ref, out_ref, n, length):
    shifted = pltpu.roll(cache_ref[...], length - 1, axis=1)
    out_ref[...] = shifted
    new_col = pltpu.roll(tail_ref[...], (LANES - 1) - n, axis=1)
    lane = lax.broadcasted_iota(jnp.int32, new_col.shape, 1)
    out_ref[:, length - LANES:] = jnp.where(lane == LANES - 1, new_col, shifted[:, length - LANES:])


def _sample_kernel(slopes_ref, sinks_ref, qa_ref, kva_ref, b0_ref, b12_ref,
                   ta_ref, tb0_ref, tb1_ref, tb2_ref, ca_ref, cb0_ref, cb1_ref, cb2_ref,
                   oa_ref, ob_ref, lse_ref, na_ref, nb0_ref, nb1_ref, nb2_ref):
    n = pl.program_id(0)
    row = pl.ds(n, 1)
    sub = lax.broadcasted_iota(jnp.int32, (8, LANES), 0)
    lane = lax.broadcasted_iota(jnp.int32, (8, LANES), 1)
    half = lane // HEAD_DIM

    kv_head = sub // A_GROUP
    q_row = qa_ref[row, :]
    q_blk = jnp.zeros((8, LANES), F32)
    for r in range(A_Q_HEADS):
        k, g = divmod(r, A_GROUP)
        chunk = q_row[:, (r // 2) * LANES:(r // 2 + 1) * LANES]
        if g % 2 != k:
            chunk = pltpu.roll(chunk, HEAD_DIM, axis=1)
        q_blk = jnp.where((sub == r) & (half == k), jnp.broadcast_to(chunk, (8, LANES)), q_blk)
    kva = kva_ref[row, :]
    k_new = jnp.where(lane[:1] < HEAD_DIM, kva[:, 0:LANES], kva[:, LANES:2 * LANES])
    v_new = jnp.where(lane[:1] < HEAD_DIM, kva[:, 2 * LANES:3 * LANES], kva[:, 3 * LANES:])
    slope = _rows_where(sub[:, :1], [slopes_ref[r] for r in range(A_Q_HEADS)])
    sink = _rows_where(sub[:, :1], [sinks_ref[r] for r in range(A_Q_HEADS)])
    dist = (BLOCK - lane).astype(F32)
    s = _dot(q_blk.astype(BF16), ca_ref[0:A_KVW, :].astype(BF16)) * ATTN_SCALE - slope * dist
    s_new = jnp.sum(q_blk * k_new, axis=-1, keepdims=True) * ATTN_SCALE
    m = jnp.maximum(jnp.maximum(jnp.max(s, axis=-1, keepdims=True), s_new), sink)
    e = jnp.exp(s - m)
    e_new = jnp.exp(s_new - m)
    denom = jnp.sum(e, axis=-1, keepdims=True) + e_new + jnp.exp(sink - m)
    acc = _dot_nt(e.astype(BF16), ca_ref[A_KVW:2 * A_KVW, :].astype(BF16))
    out = jnp.where(half == kv_head, (acc + e_new * v_new) / denom, 0.0)
    for c in range(A_W // LANES):
        k = c // 2
        pieces = []
        for hh in range(2):
            r = k * A_GROUP + 2 * (c % 2) + hh
            piece = out[r:r + 1, :]
            pieces.append(piece if hh == k else pltpu.roll(piece, HEAD_DIM, axis=1))
        oa_ref[:, c * LANES:(c + 1) * LANES] = jnp.where(lane[:1] < HEAD_DIM, pieces[0], pieces[1])
    _shift_cache(ca_ref, ta_ref, na_ref, n, BLOCK)

    sub_b = lax.broadcasted_iota(jnp.int32, (8, B_GW), 0)
    head_mask = (lax.broadcasted_iota(jnp.int32, (8, B_GW), 1) // HEAD_DIM) == sub_b
    groups = ((cb0_ref, tb0_ref, nb0_ref), (cb1_ref, tb1_ref, nb1_ref), (cb2_ref, tb2_ref, nb2_ref))
    for g, (c_ref, t_ref, n_ref) in enumerate(groups):
        length, dil = B_PATTERNS[g]
        if g == 0:
            qkv = b0_ref[row, :]
        else:
            qkv = b12_ref[row, (g - 1) * 3 * B_GW:g * 3 * B_GW]
        q_row, k_new, v_new = qkv[:, 0:B_GW], qkv[:, B_GW:2 * B_GW], qkv[:, 2 * B_GW:]
        q_blk = jnp.where(head_mask, jnp.broadcast_to(q_row, (8, B_GW)), 0.0)
        head0 = A_Q_HEADS + g * B_HEADS_PER_GROUP
        sub_l = lax.broadcasted_iota(jnp.int32, (8, length), 0)
        col = lax.broadcasted_iota(jnp.int32, (8, length), 1)
        slope = _rows_where(sub_l, [slopes_ref[head0 + h] for h in range(B_HEADS_PER_GROUP)])
        s = _dot(q_blk.astype(BF16), c_ref[0:B_GW, :].astype(BF16)) * ATTN_SCALE
        s = jnp.where((col & (dil - 1)) == 0, s - slope * (length - col).astype(F32), NEG)
        s_new = jnp.sum(q_blk * k_new, axis=-1, keepdims=True) * ATTN_SCALE
        m = jnp.maximum(jnp.max(s, axis=-1, keepdims=True), s_new)
        e = jnp.exp(s - m)
        e_new = jnp.exp(s_new - m)
        denom = jnp.sum(e, axis=-1, keepdims=True) + e_new
        acc = _dot_nt(e.astype(BF16), c_ref[B_GW:2 * B_GW, :].astype(BF16))
        out = jnp.where(head_mask, (acc + e_new * v_new) / denom, 0.0)
        ob_ref[g] = jnp.sum(out, axis=0, keepdims=True)
        lse = jnp.where(head_mask, m + jnp.log(denom), 0.0)
        lse_ref[g] = jnp.sum(lse, axis=0, keepdims=True)
        _shift_cache(c_ref, t_ref, n_ref, n, length)


def _sample(slopes, sinks, qa, kva, b0, b12, tails, caches):
    n_seq = qa.shape[0]
    full = lambda a: pl.BlockSpec(a.shape, lambda n: (0,) * a.ndim)
    per_seq = lambda a: pl.BlockSpec((None,) + a.shape[1:], lambda n: (n, 0, 0))
    small = [qa, kva, b0, b12] + list(tails)
    out_shape = [jax.ShapeDtypeStruct((n_seq, 1, A_W), F32),
                 jax.ShapeDtypeStruct((B_N_GROUPS, n_seq, 1, B_GW), F32),
                 jax.ShapeDtypeStruct((B_N_GROUPS, n_seq, 1, B_GW), F32)]
    out_shape += [jax.ShapeDtypeStruct(c.shape, F32) for c in caches]
    out_specs = [pl.BlockSpec((None, 1, A_W), lambda n: (n, 0, 0)),
                 pl.BlockSpec((B_N_GROUPS, None, 1, B_GW), lambda n: (0, n, 0, 0)),
                 pl.BlockSpec((B_N_GROUPS, None, 1, B_GW), lambda n: (0, n, 0, 0))]
    out_specs += [per_seq(c) for c in caches]
    oa, ob, lse, *new_caches = pl.pallas_call(
        _sample_kernel,
        grid=(n_seq,),
        in_specs=[_smem_spec(), _smem_spec()] + [full(a) for a in small] + [per_seq(c) for c in caches],
        out_specs=out_specs,
        out_shape=out_shape,
        compiler_params=_params(1),
        name="sample_attn",
    )(slopes, sinks, *small, *caches)
    return (oa.reshape(n_seq, A_W), ob.reshape(B_N_GROUPS, n_seq, B_GW), lse.reshape(B_N_GROUPS, n_seq, B_GW),
            *new_caches)


_N_A_PAIRS = A_W // LANES
_N_B_PAIRS = B_GW // LANES


def _merge_kernel(*refs):
    refs = list(refs)
    take = lambda count: [refs.pop(0) for _ in range(count)]
    (x_ref,), oa_refs = take(1), take(_N_A_PAIRS)
    ob_refs, lse_refs = take(B_N_GROUPS * _N_B_PAIRS), take(B_N_GROUPS * _N_B_PAIRS)
    gates_ref, wua_ref, wub_ref, wo_ref, gain_ref, wg_ref, wu_ref, wd_ref, y_ref, act_ref = refs

    ob_pairs = []
    for p in range(_N_B_PAIRS):
        lses = [lse_refs[g * _N_B_PAIRS + p][...] for g in range(B_N_GROUPS)]
        m = functools.reduce(jnp.maximum, lses)
        es = [jnp.exp(l - m) for l in lses]
        num = sum(e * ob_refs[g * _N_B_PAIRS + p][...] for g, e in enumerate(es))
        ob_pairs.append((num / sum(es)).astype(BF16))
    ua = _dot(jnp.concatenate([r[...] for r in oa_refs], axis=1), wua_ref[...])
    ub = _dot(jnp.concatenate(ob_pairs, axis=1), wub_ref[...])
    gate_a = gates_ref[:, :D_MODEL].astype(F32)
    gate_b = gates_ref[:, D_MODEL:].astype(F32)
    mixed = (gate_a * ua + gate_b * ub).astype(BF16)
    x = x_ref[...] + _dot(mixed, wo_ref[...])
    y_ref[...] = _swiglu_residual(x, gain_ref[...], wg_ref, wu_ref, wd_ref, act_ref)


def _merge(x, oa_pairs, ob_pairs, lse_pairs, gates, wua, wub, wo, gain, wg, wu, wd, tm):
    t = x.shape[0]
    row_spec = lambda w: pl.BlockSpec((tm, w), lambda i: (i, 0))
    n_pair_inputs = len(oa_pairs) + len(ob_pairs) + len(lse_pairs)
    return pl.pallas_call(
        _merge_kernel,
        grid=(t // tm,),
        in_specs=[row_spec(D_MODEL)] + [row_spec(LANES)] * n_pair_inputs + [row_spec(2 * D_MODEL),
                  _const_spec((A_W, D_MODEL)), _const_spec((B_GW, D_MODEL)), _const_spec((D_MODEL, D_MODEL)),
                  _const_spec((1, D_MODEL)), _const_spec((D_MODEL, D_FF)), _const_spec((D_MODEL, D_FF)),
                  _const_spec((D_FF, D_MODEL))],
        out_specs=row_spec(D_MODEL),
        out_shape=jax.ShapeDtypeStruct((t, D_MODEL), F32),
        scratch_shapes=[pltpu.VMEM((tm, D_FF), BF16)],
        compiler_params=_params(1),
        name="merge_ffn2",
    )(x, *oa_pairs, *ob_pairs, *lse_pairs, gates, wua, wub, wo, gain, wg, wu, wd)


def _cache_view(cache):
    _, n, length, two, h, d = cache.shape
    return jnp.transpose(cache, (0, 1, 3, 4, 5, 2)).reshape(n, two * h * d, length)


def _state_view(rows_by_len, heads):
    n, _, length = rows_by_len.shape
    return jnp.transpose(rows_by_len.reshape(1, n, 2, heads, HEAD_DIM, length), (0, 1, 5, 2, 3, 4))


def kernel(x_prompt, x_sample, cache_a_kv, cache_b1_kv, cache_b2_kv, cache_b3_kv, norm_ffn1, w1_gate, w1_up,
           w1_down, norm_mix, w_in, q_norm_a, k_norm_a, q_norm_b, k_norm_b, sinks_a, w_up_a, w_up_b, w_o,
           norm_ffn2, w2_gate, w2_up, w2_down):
    assert x_prompt.shape[-1] == D_MODEL and w_in.shape == (1, D_MODEL, IN_W)
    batch, seq, _ = x_prompt.shape
    dec = x_sample.shape[0]
    assert x_sample.shape[1] == 1 and seq % (BLOCK * B_PATTERNS[-1][1]) == 0
    assert cache_a_kv.shape[2] == BLOCK
    assert all(c.shape[2] == w for c, (w, _) in zip((cache_b1_kv, cache_b2_kv, cache_b3_kv), B_PATTERNS))

    bf = lambda w: w[0].astype(BF16)
    wg1, wu1, wd1, wg2, wu2, wd2 = map(bf, (w1_gate, w1_up, w1_down, w2_gate, w2_up, w2_down))
    w_in_b, wua, wub, wo = map(bf, (w_in, w_up_a, w_up_b, w_o))

    i = jnp.arange(1, N_ALIBI_HEADS + 1, dtype=F32)
    slopes = jnp.exp2(-8.0 * i / N_ALIBI_HEADS)
    sinks = sinks_a[0].reshape(A_Q_HEADS).astype(F32)
    ones64 = jnp.ones((HEAD_DIM,), F32)
    qkgain = jnp.concatenate([
        jnp.tile(q_norm_a[0], A_Q_HEADS), jnp.tile(k_norm_a[0], A_KV_HEADS), jnp.tile(ones64, A_KV_HEADS),
        jnp.tile(q_norm_b[0], B_N_GROUPS * B_HEADS_PER_GROUP), jnp.tile(k_norm_b[0], B_N_GROUPS * B_HEADS_PER_GROUP),
        jnp.tile(ones64, B_N_GROUPS * B_HEADS_PER_GROUP)]).reshape(1, QKV_W).astype(F32)
    head_of = jnp.arange(CHUNK) // HEAD_DIM
    ones_bd = (head_of[:, None] == head_of[None, :]).astype(BF16)

    def front(x, n_seq, s, tm):
        x1 = _ffn(x, norm_ffn1, wg1, wu1, wd1, tm)
        return x1, _proj(x1, norm_mix, w_in_b, qkgain, ones_bd, n_seq, s, tm)

    def back(x1, oa, obs, lses, gates, tm):
        return _merge(x1, oa, obs, lses, gates, wua, wub, wo, norm_ffn2, wg2, wu2, wd2, tm)

    x1p, (qa, kva, b0, b12, gates, ta, tb0, tb1, tb2) = front(x_prompt.reshape(batch * seq, D_MODEL), batch, seq, 512)
    common = dict(n_seq=batch, seq=seq)
    oa, _ = _attn(slopes, sinks, qa, kva, kva, (0, 1, 2, 3), (0, 1), (2, 3), k_pair=(0, 0, 1, 1), dil=1,
                  head0=0, out_dtype=BF16, has_lse=False, name="attn_a", **common)
    obs, lses = [], []
    for g, (_, dil) in enumerate(B_PATTERNS):
        src = b0 if g == 0 else b12
        first = 0 if g == 0 else (g - 1) * 3 * _N_B_PAIRS
        ob, lse = _attn(slopes, None, src, src, src, (first, first + 1), (first + 2, first + 3),
                        (first + 4, first + 5), k_pair=(0, 1), dil=dil,
                        head0=A_Q_HEADS + g * B_HEADS_PER_GROUP, out_dtype=F32, has_lse=True,
                        name=f"attn_b{g}", **common)
        obs += ob
        lses += lse
    y_prompt = back(x1p, oa, obs, lses, gates, 256).reshape(batch, seq, D_MODEL)

    x1s, (qa_s, kva_s, b0_s, b12_s, gates_s, *tails_s) = front(x_sample.reshape(dec, D_MODEL), 1, dec, dec)
    caches = [_cache_view(c) for c in (cache_a_kv, cache_b1_kv, cache_b2_kv, cache_b3_kv)]
    oa_s, ob_s, lse_s, na, nb0, nb1, nb2 = _sample(
        slopes, sinks, qa_s.astype(F32), kva_s.astype(F32), b0_s.astype(F32), b12_s,
        [t[0] for t in tails_s], caches)
    pairs = lambda a: [a[..., p * LANES:(p + 1) * LANES] for p in range(a.shape[-1] // LANES)]
    ob_s = [pair for g in range(B_N_GROUPS) for pair in pairs(ob_s[g])]
    lse_s = [pair for g in range(B_N_GROUPS) for pair in pairs(lse_s[g])]
    y_sample = back(x1s, pairs(oa_s.astype(BF16)), ob_s, lse_s, gates_s, dec).reshape(dec, 1, D_MODEL)

    return (y_prompt, y_sample,
            _state_view(ta, A_KV_HEADS), _state_view(tb0, B_HEADS_PER_GROUP),
            _state_view(tb1, B_HEADS_PER_GROUP), _state_view(tb2, B_HEADS_PER_GROUP),
            _state_view(na, A_KV_HEADS), _state_view(nb0, B_HEADS_PER_GROUP),
            _state_view(nb1, B_HEADS_PER_GROUP), _state_view(nb2, B_HEADS_PER_GROUP))
```

```python
import functools
import math

import jax
import jax.numpy as jnp
from jax import lax
from jax.experimental import pallas as pl
from jax.experimental.pallas import tpu as pltpu

D_MODEL = 1024
D_FF = 2816
HEAD_DIM = 64
A_Q_HEADS = 8
A_KV_HEADS = 2
A_GROUP = A_Q_HEADS // A_KV_HEADS
B_PATTERNS = ((128, 1), (512, 4), (2048, 16))
B_HEADS_PER_GROUP = 4
B_N_GROUPS = 3
N_ALIBI_HEADS = A_Q_HEADS + B_N_GROUPS * B_HEADS_PER_GROUP
BLOCK = 128
EPS = 1e-6
ATTN_SCALE = HEAD_DIM ** -0.5
LOG2E = math.log2(math.e)
LN2 = math.log(2.0)
A_W = A_Q_HEADS * HEAD_DIM
A_KVW = A_KV_HEADS * HEAD_DIM
B_GW = B_HEADS_PER_GROUP * HEAD_DIM
B_W = B_N_GROUPS * B_GW
QKV_W = A_W + 2 * A_KVW + 3 * B_W
IN_W = QKV_W + 2 * D_MODEL

LANES = 128
CHUNK = 256
NEG = -1e30
VMEM_LIMIT = 56 * 1024 * 1024

F32 = jnp.float32
BF16 = jnp.bfloat16


def _const_spec(shape):
    nd = len(shape)
    return pl.BlockSpec(shape, lambda *_: (0,) * nd, pipeline_mode=pl.Buffered(1))


def _smem_spec():
    return pl.BlockSpec(memory_space=pltpu.SMEM)


def _params(n_axes):
    return pltpu.CompilerParams(
        dimension_semantics=("arbitrary",) * n_axes, vmem_limit_bytes=VMEM_LIMIT)


def _rmsnorm(x, g):
    return x * lax.rsqrt(jnp.mean(x * x, axis=-1, keepdims=True) + EPS) * g


def _dot(a, b):
    return jnp.dot(a, b, preferred_element_type=F32)


def _dot_nt(a, b):
    return lax.dot_general(a, b, (((1,), (1,)), ((), ())), preferred_element_type=F32)


def _swiglu_residual(x, gain, wg_ref, wu_ref, wd_ref, act_ref):
    h = _rmsnorm(x, gain).astype(BF16)
    for c in range(D_FF // CHUNK):
        sl = slice(c * CHUNK, (c + 1) * CHUNK)
        g = _dot(h, wg_ref[:, sl])
        u = _dot(h, wu_ref[:, sl])
        act_ref[:, sl] = (g * jax.nn.sigmoid(g) * u).astype(BF16)
    return x + 0.5 * _dot(act_ref[...], wd_ref[...])


def _ffn_kernel(x_ref, gain_ref, wg_ref, wu_ref, wd_ref, o_ref, act_ref):
    o_ref[...] = _swiglu_residual(x_ref[...], gain_ref[...], wg_ref, wu_ref, wd_ref, act_ref)


def _ffn(x, gain, wg, wu, wd, tm):
    t = x.shape[0]
    return pl.pallas_call(
        _ffn_kernel,
        grid=(t // tm,),
        in_specs=[pl.BlockSpec((tm, D_MODEL), lambda i: (i, 0)),
                  _const_spec((1, D_MODEL)), _const_spec((D_MODEL, D_FF)),
                  _const_spec((D_MODEL, D_FF)), _const_spec((D_FF, D_MODEL))],
        out_specs=pl.BlockSpec((tm, D_MODEL), lambda i: (i, 0)),
        out_shape=jax.ShapeDtypeStruct((t, D_MODEL), F32),
        scratch_shapes=[pltpu.VMEM((tm, D_FF), BF16)],
        compiler_params=_params(1),
        name="ffn1",
    )(x, gain, wg, wu, wd)


_N_QKV_CHUNKS = QKV_W // CHUNK
_N_CHUNKS = IN_W // CHUNK


def _tail_plan(tail, tm, n_tiles):
    if tail >= tm:
        return tm, n_tiles - tail // tm
    return tail, n_tiles - 1


def _proj_kernel(x_ref, gain_ref, w_ref, qkgain_ref, ones_ref,
                 qa_ref, kva_ref, b0_ref, b12_ref, gates_ref, ta_ref, tb0_ref, tb1_ref, tb2_ref,
                 *, tm, n_tiles, tails):
    h = _rmsnorm(x_ref[...], gain_ref[...]).astype(BF16)
    lane = lax.broadcasted_iota(jnp.int32, (tm, CHUNK), 1)
    lane1 = lax.broadcasted_iota(jnp.int32, (tm, LANES), 1)

    def z_chunk(c):
        return _dot(h, w_ref[:, c * CHUNK:(c + 1) * CHUNK])

    def head_norm(z, c):
        zz = z * z
        hi = zz.astype(BF16)
        lo = (zz - hi.astype(F32)).astype(BF16)
        ss = _dot(hi, ones_ref[...]) + _dot(lo, ones_ref[...])
        return z * lax.rsqrt(ss * (1.0 / HEAD_DIM) + EPS) * qkgain_ref[:, c * CHUNK:(c + 1) * CHUNK]

    def write_tail(t_ref, row0, y, tail):
        block_w, _ = _tail_plan(tail, tm, n_tiles)
        data = y if block_w == tm else y[tm - block_w:, :]
        t_ref[row0:row0 + CHUNK, :] = data.T

    tb_refs = (tb0_ref, tb1_ref, tb2_ref)

    def consume(c, z):
        if c < 2:
            qa_ref[:, c * CHUNK:(c + 1) * CHUNK] = head_norm(z, c).astype(BF16)
        elif c == 2:
            y = jnp.where(lane < A_KVW, head_norm(z, c), z)
            write_tail(ta_ref, 0, y, tails[0])
            for part in range(2):
                pair = y[:, part * LANES:(part + 1) * LANES]
                swapped = pltpu.roll(pair, HEAD_DIM, axis=1)
                base = part * 2 * LANES
                kva_ref[:, base:base + LANES] = jnp.where(lane1 < HEAD_DIM, pair, swapped).astype(BF16)
                kva_ref[:, base + LANES:base + 2 * LANES] = jnp.where(lane1 < HEAD_DIM, swapped, pair).astype(BF16)
        elif c < _N_QKV_CHUNKS:
            kind, g = divmod(c - 3, B_N_GROUPS)
            y = z if kind == 2 else head_norm(z, c)
            if g == 0:
                b0_ref[:, kind * CHUNK:(kind + 1) * CHUNK] = y.astype(BF16)
            else:
                col = ((g - 1) * 3 + kind) * CHUNK
                b12_ref[:, col:col + CHUNK] = y
            if kind > 0:
                write_tail(tb_refs[g], (kind - 1) * CHUNK, y, tails[1 + g])
        else:
            col = (c - _N_QKV_CHUNKS) * CHUNK
            gates_ref[:, col:col + CHUNK] = jax.nn.sigmoid(z).astype(BF16)

    z_next = z_chunk(0)
    for c in range(_N_CHUNKS):
        z = z_next
        if c + 1 < _N_CHUNKS:
            z_next = z_chunk(c + 1)
        consume(c, z)


def _proj(x, gain, w_in, qkgain, ones_bd, n_seq, seq, tm):
    t = n_seq * seq
    n_tiles = seq // tm
    tails = (min(128, seq),) + tuple(min(w, seq) for w, _ in B_PATTERNS)
    tail_rows = (2 * A_KVW, 2 * B_GW, 2 * B_GW, 2 * B_GW)

    def tail_spec(rows, tail):
        block_w, first = _tail_plan(tail, tm, n_tiles)
        return pl.BlockSpec((None, rows, block_w), lambda n, j: (n, 0, jnp.maximum(j - first, 0)))

    row_spec = lambda w: pl.BlockSpec((tm, w), lambda n, j: (n * n_tiles + j, 0))
    out_shape = [
        jax.ShapeDtypeStruct((t, A_W), BF16),
        jax.ShapeDtypeStruct((t, 4 * LANES), BF16),
        jax.ShapeDtypeStruct((t, 3 * B_GW), BF16),
        jax.ShapeDtypeStruct((t, 6 * B_GW), F32),
        jax.ShapeDtypeStruct((t, 2 * D_MODEL), BF16),
    ] + [jax.ShapeDtypeStruct((n_seq, r, tl), F32) for r, tl in zip(tail_rows, tails)]
    out_specs = [row_spec(A_W), row_spec(4 * LANES), row_spec(3 * B_GW), row_spec(6 * B_GW),
                 row_spec(2 * D_MODEL)] + [tail_spec(r, tl) for r, tl in zip(tail_rows, tails)]
    return pl.pallas_call(
        functools.partial(_proj_kernel, tm=tm, n_tiles=n_tiles, tails=tails),
        grid=(n_seq, n_tiles),
        in_specs=[row_spec(D_MODEL), _const_spec((1, D_MODEL)), _const_spec((D_MODEL, IN_W)),
                  _const_spec((1, QKV_W)), _const_spec((CHUNK, CHUNK))],
        out_specs=out_specs,
        out_shape=out_shape,
        compiler_params=_params(2),
        name="proj",
    )(x, gain, w_in, qkgain, ones_bd)


_HEADS_PER_ITER = 16


def _attn_kernel(*refs, n_pairs, k_pair, dil, seq, head0, has_sink, has_lse):
    refs = list(refs)
    slopes_ref = refs.pop(0)
    sinks_ref = refs.pop(0) if has_sink else None
    n_kv = max(k_pair) + 1
    take = lambda count: [refs.pop(0) for _ in range(count)]
    q_refs, k_refs, v_refs, o_refs = take(n_pairs), take(n_kv), take(n_kv), take(n_pairs)
    lse_refs = take(n_pairs) if has_lse else None
    (bias_ref,) = refs

    n_blocks = seq // (BLOCK * dil)
    use_prev = n_blocks > 1
    n_heads = 2 * n_pairs
    qi = lax.broadcasted_iota(jnp.int32, (BLOCK, BLOCK), 0)
    ki = lax.broadcasted_iota(jnp.int32, (BLOCK, BLOCK), 1)
    d_cur = qi - ki
    lo = ki < HEAD_DIM

    for h in range(n_heads):
        slope = slopes_ref[head0 + h] * (float(dil) * LOG2E)
        cur_bias = jnp.where(d_cur >= 0, -slope * d_cur.astype(F32), NEG)
        if use_prev:
            prev_bias = jnp.where(d_cur <= 0, -slope * (d_cur + BLOCK).astype(F32), NEG)
            bias_ref[h] = jnp.concatenate([prev_bias, cur_bias], axis=1)
            bias_ref[n_heads + h] = jnp.concatenate([jnp.full_like(prev_bias, NEG), cur_bias], axis=1)
        else:
            bias_ref[h] = cur_bias

    def block_rows(step):
        if n_blocks == 1:
            r, j = step, 0
        else:
            r, j = step // n_blocks, step % n_blocks
        if dil == 1:
            cur = pl.ds(pl.multiple_of(j * BLOCK, BLOCK), BLOCK)
            prev = pl.ds(pl.multiple_of(jnp.maximum(j - 1, 0) * BLOCK, BLOCK), BLOCK)
        else:
            cur = pl.ds(r + j * (BLOCK * dil), BLOCK, stride=dil)
            prev = pl.ds(r + jnp.maximum(j - 1, 0) * (BLOCK * dil), BLOCK, stride=dil)
        table = jnp.where(j == 0, n_heads, 0) if use_prev else 0
        return cur, prev, table

    blocks_per_iter = _HEADS_PER_ITER // (2 * n_pairs)
    assert (dil * n_blocks) % blocks_per_iter == 0

    def body(it, carry):
        blocks = [block_rows(it * blocks_per_iter + b) for b in range(blocks_per_iter)]
        scores, values = {}, {}
        for b, (cur, prev, _) in enumerate(blocks):
            for p in range(n_pairs):
                k_ref, v_ref = k_refs[k_pair[p]], v_refs[k_pair[p]]
                q = q_refs[p][cur, :].astype(BF16)
                keys = k_ref[cur, :].astype(BF16)
                vals = v_ref[cur, :].astype(BF16)
                if use_prev:
                    keys = jnp.concatenate([k_ref[prev, :].astype(BF16), keys], axis=0)
                    vals = jnp.concatenate([v_ref[prev, :].astype(BF16), vals], axis=0)
                values[b, p] = vals
                for half in range(2):
                    qm = jnp.where(lo if half == 0 else jnp.logical_not(lo), q, jnp.zeros_like(q))
                    scores[b, p, half] = _dot_nt(qm, keys)
        probs, row_max, denoms = {}, {}, {}
        for (b, p, half), s in scores.items():
            head = 2 * p + half
            s = s + bias_ref[blocks[b][2] + head]
            m = jnp.max(s, axis=-1, keepdims=True)
            if has_sink:
                sink = sinks_ref[head] * LOG2E
                m = jnp.maximum(m, sink)
            e = jnp.exp2(s - m)
            denom = jnp.sum(e, axis=-1, keepdims=True)
            if has_sink:
                denom = denom + jnp.exp2(sink - m)
            probs[b, p, half], row_max[b, p, half], denoms[b, p, half] = e.astype(BF16), m, denom
        for b, (cur, _, _) in enumerate(blocks):
            for p in range(n_pairs):
                outs = [_dot(probs[b, p, half], values[b, p]) * (1.0 / denoms[b, p, half]) for half in range(2)]
                o_refs[p][cur, :] = jnp.where(lo, outs[0], outs[1]).astype(o_refs[p].dtype)
                if has_lse:
                    lses = [LN2 * (row_max[b, p, half] + jnp.log2(denoms[b, p, half])) for half in range(2)]
                    lse_refs[p][cur, :] = jnp.where(lo, lses[0], lses[1])
        return carry

    lax.fori_loop(0, dil * n_blocks // blocks_per_iter, body, 0)


def _attn(slopes, sinks, q_arr, k_arr, v_arr, q_idx, k_idx, v_idx, *, n_seq, seq, k_pair, dil, head0,
          out_dtype, has_lse, name):
    has_sink = sinks is not None
    n_pairs = len(q_idx)

    def pair_spec(idx):
        return pl.BlockSpec((None, seq, LANES), lambda n: (n, 0, idx))

    view = lambda a: a.reshape(n_seq, seq, a.shape[-1])
    operands = [slopes] + ([sinks] if has_sink else [])
    operands += [view(q_arr)] * n_pairs + [view(k_arr)] * len(k_idx) + [view(v_arr)] * len(v_idx)
    in_specs = [_smem_spec()] * (2 if has_sink else 1)
    in_specs += [pair_spec(i) for i in tuple(q_idx) + tuple(k_idx) + tuple(v_idx)]
    out_shape = [jax.ShapeDtypeStruct((n_seq, seq, LANES), out_dtype)] * n_pairs
    if has_lse:
        out_shape += [jax.ShapeDtypeStruct((n_seq, seq, LANES), F32)] * n_pairs
    outs = pl.pallas_call(
        functools.partial(_attn_kernel, n_pairs=n_pairs, k_pair=k_pair, dil=dil, seq=seq, head0=head0,
                          has_sink=has_sink, has_lse=has_lse),
        grid=(n_seq,),
        in_specs=in_specs,
        out_specs=[pair_spec(0)] * len(out_shape),
        out_shape=out_shape,
        scratch_shapes=[pltpu.VMEM((2 * n_pairs, BLOCK, BLOCK) if seq == BLOCK * dil else
                                   (4 * n_pairs, BLOCK, 2 * BLOCK), F32)],
        compiler_params=_params(1),
        name=name,
    )(*operands)
    outs = [o.reshape(n_seq * seq, LANES) for o in outs]
    return outs[:n_pairs], outs[n_pairs:]


def _rows_where(row_iota, values):
    out = jnp.zeros(row_iota.shape, F32)
    for r, v in enumerate(values):
        out = jnp.where(row_iota == r, v, out)
    return out


def _shift_cache(cache_ref, tail_ref, out_ref, n, length):
    shifted = pltpu.roll(cache_ref[...], length - 1, axis=1)
    out_ref[...] = shifted
    new_col = pltpu.roll(tail_ref[...], (LANES - 1) - n, axis=1)
    lane = lax.broadcasted_iota(jnp.int32, new_col.shape, 1)
    out_ref[:, length - LANES:] = jnp.where(lane == LANES - 1, new_col, shifted[:, length - LANES:])


def _sample_kernel(slopes_ref, sinks_ref, qa_ref, kva_ref, b0_ref, b12_ref,
                   ta_ref, tb0_ref, tb1_ref, tb2_ref, ca_ref, cb0_ref, cb1_ref, cb2_ref,
                   oa_ref, ob_ref, lse_ref, na_ref, nb0_ref, nb1_ref, nb2_ref):
    n = pl.program_id(0)
    row = pl.ds(n, 1)
    sub = lax.broadcasted_iota(jnp.int32, (8, LANES), 0)
    lane = lax.broadcasted_iota(jnp.int32, (8, LANES), 1)
    half = lane // HEAD_DIM

    kv_head = sub // A_GROUP
    q_row = qa_ref[row, :]
    q_blk = jnp.zeros((8, LANES), F32)
    for r in range(A_Q_HEADS):
        k, g = divmod(r, A_GROUP)
        chunk = q_row[:, (r // 2) * LANES:(r // 2 + 1) * LANES]
        if g % 2 != k:
            chunk = pltpu.roll(chunk, HEAD_DIM, axis=1)
        q_blk = jnp.where((sub == r) & (half == k), jnp.broadcast_to(chunk, (8, LANES)), q_blk)
    kva = kva_ref[row, :]
    k_new = jnp.where(lane[:1] < HEAD_DIM, kva[:, 0:LANES], kva[:, LANES:2 * LANES])
    v_new = jnp.where(lane[:1] < HEAD_DIM, kva[:, 2 * LANES:3 * LANES], kva[:, 3 * LANES:])
    slope = _rows_where(sub[:, :1], [slopes_ref[r] * LOG2E for r in range(A_Q_HEADS)])
    sink = _rows_where(sub[:, :1], [sinks_ref[r] * LOG2E for r in range(A_Q_HEADS)])
    dist = (BLOCK - lane).astype(F32)
    s = _dot(q_blk.astype(BF16), ca_ref[0:A_KVW, :].astype(BF16)) - slope * dist
    s_new = jnp.sum(q_blk * k_new, axis=-1, keepdims=True)
    m = jnp.maximum(jnp.maximum(jnp.max(s, axis=-1, keepdims=True), s_new), sink)
    e = jnp.exp2(s - m)
    e_new = jnp.exp2(s_new - m)
    denom = jnp.sum(e, axis=-1, keepdims=True) + e_new + jnp.exp2(sink - m)
    acc = _dot_nt(e.astype(BF16), ca_ref[A_KVW:2 * A_KVW, :].astype(BF16))
    out = jnp.where(half == kv_head, (acc + e_new * v_new) / denom, 0.0)
    for c in range(A_W // LANES):
        k = c // 2
        pieces = []
        for hh in range(2):
            r = k * A_GROUP + 2 * (c % 2) + hh
            piece = out[r:r + 1, :]
            pieces.append(piece if hh == k else pltpu.roll(piece, HEAD_DIM, axis=1))
        oa_ref[:, c * LANES:(c + 1) * LANES] = jnp.where(lane[:1] < HEAD_DIM, pieces[0], pieces[1])
    _shift_cache(ca_ref, ta_ref, na_ref, n, BLOCK)

    sub_b = lax.broadcasted_iota(jnp.int32, (8, B_GW), 0)
    head_mask = (lax.broadcasted_iota(jnp.int32, (8, B_GW), 1) // HEAD_DIM) == sub_b
    groups = ((cb0_ref, tb0_ref, nb0_ref), (cb1_ref, tb1_ref, nb1_ref), (cb2_ref, tb2_ref, nb2_ref))
    for g, (c_ref, t_ref, n_ref) in enumerate(groups):
        length, dil = B_PATTERNS[g]
        if g == 0:
            qkv = b0_ref[row, :]
        else:
            qkv = b12_ref[row, (g - 1) * 3 * B_GW:g * 3 * B_GW]
        q_row, k_new, v_new = qkv[:, 0:B_GW], qkv[:, B_GW:2 * B_GW], qkv[:, 2 * B_GW:]
        q_blk = jnp.where(head_mask, jnp.broadcast_to(q_row, (8, B_GW)), 0.0)
        head0 = A_Q_HEADS + g * B_HEADS_PER_GROUP
        sub_l = lax.broadcasted_iota(jnp.int32, (8, length), 0)
        col = lax.broadcasted_iota(jnp.int32, (8, length), 1)
        slope = _rows_where(sub_l, [slopes_ref[head0 + h] * LOG2E for h in range(B_HEADS_PER_GROUP)])
        s = _dot(q_blk.astype(BF16), c_ref[0:B_GW, :].astype(BF16))
        s = jnp.where((col & (dil - 1)) == 0, s - slope * (length - col).astype(F32), NEG)
        s_new = jnp.sum(q_blk * k_new, axis=-1, keepdims=True)
        m = jnp.maximum(jnp.max(s, axis=-1, keepdims=True), s_new)
        e = jnp.exp2(s - m)
        e_new = jnp.exp2(s_new - m)
        denom = jnp.sum(e, axis=-1, keepdims=True) + e_new
        acc = _dot_nt(e.astype(BF16), c_ref[B_GW:2 * B_GW, :].astype(BF16))
        out = jnp.where(head_mask, (acc + e_new * v_new) / denom, 0.0)
        ob_ref[g] = jnp.sum(out, axis=0, keepdims=True)
        lse = jnp.where(head_mask, LN2 * (m + jnp.log2(denom)), 0.0)
        lse_ref[g] = jnp.sum(lse, axis=0, keepdims=True)
        _shift_cache(c_ref, t_ref, n_ref, n, length)


def _sample(slopes, sinks, qa, kva, b0, b12, tails, caches):
    n_seq = qa.shape[0]
    full = lambda a: pl.BlockSpec(a.shape, lambda n: (0,) * a.ndim)
    per_seq = lambda a: pl.BlockSpec((None,) + a.shape[1:], lambda n: (n, 0, 0))
    small = [qa, kva, b0, b12] + list(tails)
    out_shape = [jax.ShapeDtypeStruct((n_seq, 1, A_W), F32),
                 jax.ShapeDtypeStruct((B_N_GROUPS, n_seq, 1, B_GW), F32),
                 jax.ShapeDtypeStruct((B_N_GROUPS, n_seq, 1, B_GW), F32)]
    out_shape += [jax.ShapeDtypeStruct(c.shape, F32) for c in caches]
    out_specs = [pl.BlockSpec((None, 1, A_W), lambda n: (n, 0, 0)),
                 pl.BlockSpec((B_N_GROUPS, None, 1, B_GW), lambda n: (0, n, 0, 0)),
                 pl.BlockSpec((B_N_GROUPS, None, 1, B_GW), lambda n: (0, n, 0, 0))]
    out_specs += [per_seq(c) for c in caches]
    oa, ob, lse, *new_caches = pl.pallas_call(
        _sample_kernel,
        grid=(n_seq,),
        in_specs=[_smem_spec(), _smem_spec()] + [full(a) for a in small] + [per_seq(c) for c in caches],
        out_specs=out_specs,
        out_shape=out_shape,
        compiler_params=_params(1),
        name="sample_attn",
    )(slopes, sinks, *small, *caches)
    return (oa.reshape(n_seq, A_W), ob.reshape(B_N_GROUPS, n_seq, B_GW), lse.reshape(B_N_GROUPS, n_seq, B_GW),
            *new_caches)


_N_A_PAIRS = A_W // LANES
_N_B_PAIRS = B_GW // LANES


def _merge_kernel(*refs):
    refs = list(refs)
    take = lambda count: [refs.pop(0) for _ in range(count)]
    (x_ref,), oa_refs = take(1), take(_N_A_PAIRS)
    ob_refs, lse_refs = take(B_N_GROUPS * _N_B_PAIRS), take(B_N_GROUPS * _N_B_PAIRS)
    gates_ref, wua_ref, wub_ref, wo_ref, gain_ref, wg_ref, wu_ref, wd_ref, y_ref, act_ref = refs

    ob_pairs = []
    for p in range(_N_B_PAIRS):
        lses = [lse_refs[g * _N_B_PAIRS + p][...] for g in range(B_N_GROUPS)]
        m = functools.reduce(jnp.maximum, lses)
        es = [jnp.exp(l - m) for l in lses]
        num = sum(e * ob_refs[g * _N_B_PAIRS + p][...] for g, e in enumerate(es))
        ob_pairs.append((num / sum(es)).astype(BF16))
    ua = _dot(jnp.concatenate([r[...] for r in oa_refs], axis=1), wua_ref[...])
    ub = _dot(jnp.concatenate(ob_pairs, axis=1), wub_ref[...])
    gate_a = gates_ref[:, :D_MODEL].astype(F32)
    gate_b = gates_ref[:, D_MODEL:].astype(F32)
    mixed = (gate_a * ua + gate_b * ub).astype(BF16)
    x = x_ref[...] + _dot(mixed, wo_ref[...])
    y_ref[...] = _swiglu_residual(x, gain_ref[...], wg_ref, wu_ref, wd_ref, act_ref)


def _merge(x, oa_pairs, ob_pairs, lse_pairs, gates, wua, wub, wo, gain, wg, wu, wd, tm):
    t = x.shape[0]
    row_spec = lambda w: pl.BlockSpec((tm, w), lambda i: (i, 0))
    n_pair_inputs = len(oa_pairs) + len(ob_pairs) + len(lse_pairs)
    return pl.pallas_call(
        _merge_kernel,
        grid=(t // tm,),
        in_specs=[row_spec(D_MODEL)] + [row_spec(LANES)] * n_pair_inputs + [row_spec(2 * D_MODEL),
                  _const_spec((A_W, D_MODEL)), _const_spec((B_GW, D_MODEL)), _const_spec((D_MODEL, D_MODEL)),
                  _const_spec((1, D_MODEL)), _const_spec((D_MODEL, D_FF)), _const_spec((D_MODEL, D_FF)),
                  _const_spec((D_FF, D_MODEL))],
        out_specs=row_spec(D_MODEL),
        out_shape=jax.ShapeDtypeStruct((t, D_MODEL), F32),
        scratch_shapes=[pltpu.VMEM((tm, D_FF), BF16)],
        compiler_params=_params(1),
        name="merge_ffn2",
    )(x, *oa_pairs, *ob_pairs, *lse_pairs, gates, wua, wub, wo, gain, wg, wu, wd)


def _cache_view(cache):
    _, n, length, two, h, d = cache.shape
    return jnp.transpose(cache, (0, 1, 3, 4, 5, 2)).reshape(n, two * h * d, length)


def _state_view(rows_by_len, heads):
    n, _, length = rows_by_len.shape
    return jnp.transpose(rows_by_len.reshape(1, n, 2, heads, HEAD_DIM, length), (0, 1, 5, 2, 3, 4))


def kernel(x_prompt, x_sample, cache_a_kv, cache_b1_kv, cache_b2_kv, cache_b3_kv, norm_ffn1, w1_gate, w1_up,
           w1_down, norm_mix, w_in, q_norm_a, k_norm_a, q_norm_b, k_norm_b, sinks_a, w_up_a, w_up_b, w_o,
           norm_ffn2, w2_gate, w2_up, w2_down):
    assert x_prompt.shape[-1] == D_MODEL and w_in.shape == (1, D_MODEL, IN_W)
    batch, seq, _ = x_prompt.shape
    dec = x_sample.shape[0]
    assert x_sample.shape[1] == 1 and seq % (BLOCK * B_PATTERNS[-1][1]) == 0
    assert cache_a_kv.shape[2] == BLOCK
    assert all(c.shape[2] == w for c, (w, _) in zip((cache_b1_kv, cache_b2_kv, cache_b3_kv), B_PATTERNS))

    bf = lambda w: w[0].astype(BF16)
    wg1, wu1, wd1, wg2, wu2, wd2 = map(bf, (w1_gate, w1_up, w1_down, w2_gate, w2_up, w2_down))
    w_in_b, wua, wub, wo = map(bf, (w_in, w_up_a, w_up_b, w_o))

    i = jnp.arange(1, N_ALIBI_HEADS + 1, dtype=F32)
    slopes = jnp.exp2(-8.0 * i / N_ALIBI_HEADS)
    sinks = sinks_a[0].reshape(A_Q_HEADS).astype(F32)
    ones64 = jnp.ones((HEAD_DIM,), F32)
    q_scale = ATTN_SCALE * LOG2E
    qkgain = jnp.concatenate([
        jnp.tile(q_norm_a[0] * q_scale, A_Q_HEADS), jnp.tile(k_norm_a[0], A_KV_HEADS),
        jnp.tile(ones64, A_KV_HEADS),
        jnp.tile(q_norm_b[0] * q_scale, B_N_GROUPS * B_HEADS_PER_GROUP),
        jnp.tile(k_norm_b[0], B_N_GROUPS * B_HEADS_PER_GROUP),
        jnp.tile(ones64, B_N_GROUPS * B_HEADS_PER_GROUP)]).reshape(1, QKV_W).astype(F32)
    head_of = jnp.arange(CHUNK) // HEAD_DIM
    ones_bd = (head_of[:, None] == head_of[None, :]).astype(BF16)

    def front(x, n_seq, s, tm):
        x1 = _ffn(x, norm_ffn1, wg1, wu1, wd1, tm)
        return x1, _proj(x1, norm_mix, w_in_b, qkgain, ones_bd, n_seq, s, tm)

    def back(x1, oa, obs, lses, gates, tm):
        return _merge(x1, oa, obs, lses, gates, wua, wub, wo, norm_ffn2, wg2, wu2, wd2, tm)

    x1p, (qa, kva, b0, b12, gates, ta, tb0, tb1, tb2) = front(x_prompt.reshape(batch * seq, D_MODEL), batch, seq, 512)
    common = dict(n_seq=batch, seq=seq)
    oa, _ = _attn(slopes, sinks, qa, kva, kva, (0, 1, 2, 3), (0, 1), (2, 3), k_pair=(0, 0, 1, 1), dil=1,
                  head0=0, out_dtype=BF16, has_lse=False, name="attn_a", **common)
    obs, lses = [], []
    for g, (_, dil) in enumerate(B_PATTERNS):
        src = b0 if g == 0 else b12
        first = 0 if g == 0 else (g - 1) * 3 * _N_B_PAIRS
        ob, lse = _attn(slopes, None, src, src, src, (first, first + 1), (first + 2, first + 3),
                        (first + 4, first + 5), k_pair=(0, 1), dil=dil,
                        head0=A_Q_HEADS + g * B_HEADS_PER_GROUP, out_dtype=F32, has_lse=True,
                        name=f"attn_b{g}", **common)
        obs += ob
        lses += lse
    y_prompt = back(x1p, oa, obs, lses, gates, 256).reshape(batch, seq, D_MODEL)

    x1s, (qa_s, kva_s, b0_s, b12_s, gates_s, *tails_s) = front(x_sample.reshape(dec, D_MODEL), 1, dec, dec)
    caches = [_cache_view(c) for c in (cache_a_kv, cache_b1_kv, cache_b2_kv, cache_b3_kv)]
    oa_s, ob_s, lse_s, na, nb0, nb1, nb2 = _sample(
        slopes, sinks, qa_s.astype(F32), kva_s.astype(F32), b0_s.astype(F32), b12_s,
        [t[0] for t in tails_s], caches)
    pairs = lambda a: [a[..., p * LANES:(p + 1) * LANES] for p in range(a.shape[-1] // LANES)]
    ob_s = [pair for g in range(B_N_GROUPS) for pair in pairs(ob_s[g])]
    lse_s = [pair for g in range(B_N_GROUPS) for pair in pairs(lse_s[g])]
    y_sample = back(x1s, pairs(oa_s.astype(BF16)), ob_s, lse_s, gates_s, dec).reshape(dec, 1, D_MODEL)

    return (y_prompt, y_sample,
            _state_view(ta, A_KV_HEADS), _state_view(tb0, B_HEADS_PER_GROUP),
            _state_view(tb1, B_HEADS_PER_GROUP), _state_view(tb2, B_HEADS_PER_GROUP),
            _state_view(na, A_KV_HEADS), _state_view(nb0, B_HEADS_PER_GROUP),
            _state_view(nb1, B_HEADS_PER_GROUP), _state_view(nb2, B_HEADS_PER_GROUP))
```

```python
import functools
import math

import jax
import jax.numpy as jnp
from jax import lax
from jax.experimental import pallas as pl
from jax.experimental.pallas import tpu as pltpu

D_MODEL = 1024
D_FF = 2816
HEAD_DIM = 64
A_Q_HEADS = 8
A_KV_HEADS = 2
A_GROUP = A_Q_HEADS // A_KV_HEADS
B_PATTERNS = ((128, 1), (512, 4), (2048, 16))
B_HEADS_PER_GROUP = 4
B_N_GROUPS = 3
N_ALIBI_HEADS = A_Q_HEADS + B_N_GROUPS * B_HEADS_PER_GROUP
BLOCK = 128
EPS = 1e-6
ATTN_SCALE = HEAD_DIM ** -0.5
LOG2E = math.log2(math.e)
LN2 = math.log(2.0)
A_W = A_Q_HEADS * HEAD_DIM
A_KVW = A_KV_HEADS * HEAD_DIM
B_GW = B_HEADS_PER_GROUP * HEAD_DIM
B_W = B_N_GROUPS * B_GW
QKV_W = A_W + 2 * A_KVW + 3 * B_W
IN_W = QKV_W + 2 * D_MODEL

LANES = 128
CHUNK = 256
NEG = -1e30
VMEM_LIMIT = 56 * 1024 * 1024

F32 = jnp.float32
BF16 = jnp.bfloat16


def _const_spec(shape):
    nd = len(shape)
    return pl.BlockSpec(shape, lambda *_: (0,) * nd, pipeline_mode=pl.Buffered(1))


def _smem_spec():
    return pl.BlockSpec(memory_space=pltpu.SMEM)


def _params(n_axes):
    return pltpu.CompilerParams(
        dimension_semantics=("arbitrary",) * n_axes, vmem_limit_bytes=VMEM_LIMIT)


def _rmsnorm(x, g):
    return x * lax.rsqrt(jnp.mean(x * x, axis=-1, keepdims=True) + EPS) * g


def _dot(a, b):
    return jnp.dot(a, b, preferred_element_type=F32)


def _dot_nt(a, b):
    return lax.dot_general(a, b, (((1,), (1,)), ((), ())), preferred_element_type=F32)


def _swiglu_residual(x, gain, wg_ref, wu_ref, wd_ref, act_ref, side_work=()):
    n_chunks = D_FF // CHUNK
    assert len(side_work) <= n_chunks
    h = _rmsnorm(x, gain).astype(BF16)
    for c in range(n_chunks):
        if c < len(side_work):
            side_work[c]()
        sl = slice(c * CHUNK, (c + 1) * CHUNK)
        g = _dot(h, wg_ref[:, sl])
        u = _dot(h, wu_ref[:, sl])
        act_ref[:, sl] = (g * jax.nn.sigmoid(g) * u).astype(BF16)
    return x + 0.5 * _dot(act_ref[...], wd_ref[...])


def _ffn_kernel(x_ref, gain_ref, wg_ref, wu_ref, wd_ref, o_ref, act_ref):
    o_ref[...] = _swiglu_residual(x_ref[...], gain_ref[...], wg_ref, wu_ref, wd_ref, act_ref)


def _ffn(x, gain, wg, wu, wd, tm):
    t = x.shape[0]
    return pl.pallas_call(
        _ffn_kernel,
        grid=(t // tm,),
        in_specs=[pl.BlockSpec((tm, D_MODEL), lambda i: (i, 0)),
                  _const_spec((1, D_MODEL)), _const_spec((D_MODEL, D_FF)),
                  _const_spec((D_MODEL, D_FF)), _const_spec((D_FF, D_MODEL))],
        out_specs=pl.BlockSpec((tm, D_MODEL), lambda i: (i, 0)),
        out_shape=jax.ShapeDtypeStruct((t, D_MODEL), F32),
        scratch_shapes=[pltpu.VMEM((tm, D_FF), BF16)],
        compiler_params=_params(1),
        name="ffn1",
    )(x, gain, wg, wu, wd)


_N_QKV_CHUNKS = QKV_W // CHUNK
_N_CHUNKS = IN_W // CHUNK


def _tail_plan(tail, tm, n_tiles):
    if tail >= tm:
        return tm, n_tiles - tail // tm
    return tail, n_tiles - 1


def _proj_kernel(x_ref, gain_ref, w_ref, qkgain_ref, ones_ref,
                 qa_ref, kva_ref, b0_ref, b12_ref, gates_ref, ta_ref, tb0_ref, tb1_ref, tb2_ref,
                 *, tm, n_tiles, tails):
    h = _rmsnorm(x_ref[...], gain_ref[...]).astype(BF16)
    lane = lax.broadcasted_iota(jnp.int32, (tm, CHUNK), 1)
    lane1 = lax.broadcasted_iota(jnp.int32, (tm, LANES), 1)

    def z_chunk(c):
        return _dot(h, w_ref[:, c * CHUNK:(c + 1) * CHUNK])

    def head_norm(z, c):
        zz = z * z
        hi = zz.astype(BF16)
        lo = (zz - hi.astype(F32)).astype(BF16)
        ss = _dot(hi, ones_ref[...]) + _dot(lo, ones_ref[...])
        return z * lax.rsqrt(ss * (1.0 / HEAD_DIM) + EPS) * qkgain_ref[:, c * CHUNK:(c + 1) * CHUNK]

    def write_tail(t_ref, row0, y, tail):
        block_w, _ = _tail_plan(tail, tm, n_tiles)
        data = y if block_w == tm else y[tm - block_w:, :]
        t_ref[row0:row0 + CHUNK, :] = data.T

    tb_refs = (tb0_ref, tb1_ref, tb2_ref)

    def consume(c, z):
        if c < 2:
            qa_ref[:, c * CHUNK:(c + 1) * CHUNK] = head_norm(z, c).astype(BF16)
        elif c == 2:
            y = jnp.where(lane < A_KVW, head_norm(z, c), z)
            write_tail(ta_ref, 0, y, tails[0])
            for part in range(2):
                pair = y[:, part * LANES:(part + 1) * LANES]
                swapped = pltpu.roll(pair, HEAD_DIM, axis=1)
                base = part * 2 * LANES
                kva_ref[:, base:base + LANES] = jnp.where(lane1 < HEAD_DIM, pair, swapped).astype(BF16)
                kva_ref[:, base + LANES:base + 2 * LANES] = jnp.where(lane1 < HEAD_DIM, swapped, pair).astype(BF16)
        elif c < _N_QKV_CHUNKS:
            kind, g = divmod(c - 3, B_N_GROUPS)
            y = z if kind == 2 else head_norm(z, c)
            if g == 0:
                b0_ref[:, kind * CHUNK:(kind + 1) * CHUNK] = y.astype(BF16)
            else:
                col = ((g - 1) * 3 + kind) * CHUNK
                b12_ref[:, col:col + CHUNK] = y
            if kind > 0:
                write_tail(tb_refs[g], (kind - 1) * CHUNK, y, tails[1 + g])
        else:
            col = (c - _N_QKV_CHUNKS) * CHUNK
            gates_ref[:, col:col + CHUNK] = jax.nn.sigmoid(z).astype(BF16)

    z_next = z_chunk(0)
    for c in range(_N_CHUNKS):
        z = z_next
        if c + 1 < _N_CHUNKS:
            z_next = z_chunk(c + 1)
        consume(c, z)


def _proj(x, gain, w_in, qkgain, ones_bd, n_seq, seq, tm):
    t = n_seq * seq
    n_tiles = seq // tm
    tails = (min(128, seq),) + tuple(min(w, seq) for w, _ in B_PATTERNS)
    tail_rows = (2 * A_KVW, 2 * B_GW, 2 * B_GW, 2 * B_GW)

    def tail_spec(rows, tail):
        block_w, first = _tail_plan(tail, tm, n_tiles)
        return pl.BlockSpec((None, rows, block_w), lambda n, j: (n, 0, jnp.maximum(j - first, 0)))

    row_spec = lambda w: pl.BlockSpec((tm, w), lambda n, j: (n * n_tiles + j, 0))
    out_shape = [
        jax.ShapeDtypeStruct((t, A_W), BF16),
        jax.ShapeDtypeStruct((t, 4 * LANES), BF16),
        jax.ShapeDtypeStruct((t, 3 * B_GW), BF16),
        jax.ShapeDtypeStruct((t, 6 * B_GW), F32),
        jax.ShapeDtypeStruct((t, 2 * D_MODEL), BF16),
    ] + [jax.ShapeDtypeStruct((n_seq, r, tl), F32) for r, tl in zip(tail_rows, tails)]
    out_specs = [row_spec(A_W), row_spec(4 * LANES), row_spec(3 * B_GW), row_spec(6 * B_GW),
                 row_spec(2 * D_MODEL)] + [tail_spec(r, tl) for r, tl in zip(tail_rows, tails)]
    return pl.pallas_call(
        functools.partial(_proj_kernel, tm=tm, n_tiles=n_tiles, tails=tails),
        grid=(n_seq, n_tiles),
        in_specs=[row_spec(D_MODEL), _const_spec((1, D_MODEL)), _const_spec((D_MODEL, IN_W)),
                  _const_spec((1, QKV_W)), _const_spec((CHUNK, CHUNK))],
        out_specs=out_specs,
        out_shape=out_shape,
        compiler_params=_params(2),
        name="proj",
    )(x, gain, w_in, qkgain, ones_bd)


_HEADS_PER_ITER = 16


def _attn_kernel(*refs, n_pairs, k_pair, dil, seq, head0, has_sink, has_lse):
    refs = list(refs)
    slopes_ref = refs.pop(0)
    sinks_ref = refs.pop(0) if has_sink else None
    n_kv = max(k_pair) + 1
    take = lambda count: [refs.pop(0) for _ in range(count)]
    q_refs, k_refs, v_refs, o_refs = take(n_pairs), take(n_kv), take(n_kv), take(n_pairs)
    lse_refs = take(n_pairs) if has_lse else None
    (bias_ref,) = refs

    n_blocks = seq // (BLOCK * dil)
    use_prev = n_blocks > 1
    n_heads = 2 * n_pairs
    qi = lax.broadcasted_iota(jnp.int32, (BLOCK, BLOCK), 0)
    ki = lax.broadcasted_iota(jnp.int32, (BLOCK, BLOCK), 1)
    d_cur = qi - ki
    lo = ki < HEAD_DIM

    for h in range(n_heads):
        slope = slopes_ref[head0 + h] * (float(dil) * LOG2E)
        cur_bias = jnp.where(d_cur >= 0, -slope * d_cur.astype(F32), NEG)
        if use_prev:
            prev_bias = jnp.where(d_cur <= 0, -slope * (d_cur + BLOCK).astype(F32), NEG)
            bias_ref[h] = jnp.concatenate([prev_bias, cur_bias], axis=1)
            bias_ref[n_heads + h] = jnp.concatenate([jnp.full_like(prev_bias, NEG), cur_bias], axis=1)
        else:
            bias_ref[h] = cur_bias

    def block_rows(step):
        if n_blocks == 1:
            r, j = step, 0
        else:
            r, j = step // n_blocks, step % n_blocks
        if dil == 1:
            cur = pl.ds(pl.multiple_of(j * BLOCK, BLOCK), BLOCK)
            prev = pl.ds(pl.multiple_of(jnp.maximum(j - 1, 0) * BLOCK, BLOCK), BLOCK)
        else:
            cur = pl.ds(r + j * (BLOCK * dil), BLOCK, stride=dil)
            prev = pl.ds(r + jnp.maximum(j - 1, 0) * (BLOCK * dil), BLOCK, stride=dil)
        table = jnp.where(j == 0, n_heads, 0) if use_prev else 0
        return cur, prev, table

    blocks_per_iter = _HEADS_PER_ITER // (2 * n_pairs)
    assert (dil * n_blocks) % blocks_per_iter == 0

    def body(it, carry):
        blocks = [block_rows(it * blocks_per_iter + b) for b in range(blocks_per_iter)]
        scores, values = {}, {}
        for b, (cur, prev, _) in enumerate(blocks):
            for p in range(n_pairs):
                k_ref, v_ref = k_refs[k_pair[p]], v_refs[k_pair[p]]
                q = q_refs[p][cur, :].astype(BF16)
                keys = k_ref[cur, :].astype(BF16)
                vals = v_ref[cur, :].astype(BF16)
                if use_prev:
                    keys = jnp.concatenate([k_ref[prev, :].astype(BF16), keys], axis=0)
                    vals = jnp.concatenate([v_ref[prev, :].astype(BF16), vals], axis=0)
                values[b, p] = vals
                for half in range(2):
                    qm = jnp.where(lo if half == 0 else jnp.logical_not(lo), q, jnp.zeros_like(q))
                    scores[b, p, half] = _dot_nt(qm, keys)
        probs, row_max, denoms = {}, {}, {}
        for (b, p, half), s in scores.items():
            head = 2 * p + half
            s = s + bias_ref[blocks[b][2] + head]
            m = jnp.max(s, axis=-1, keepdims=True)
            if has_sink:
                sink = sinks_ref[head] * LOG2E
                m = jnp.maximum(m, sink)
            e = jnp.exp2(s - m)
            denom = jnp.sum(e, axis=-1, keepdims=True)
            if has_sink:
                denom = denom + jnp.exp2(sink - m)
            probs[b, p, half], row_max[b, p, half], denoms[b, p, half] = e.astype(BF16), m, denom
        for b, (cur, _, _) in enumerate(blocks):
            for p in range(n_pairs):
                outs = [_dot(probs[b, p, half], values[b, p]) * (1.0 / denoms[b, p, half]) for half in range(2)]
                o_refs[p][cur, :] = jnp.where(lo, outs[0], outs[1]).astype(o_refs[p].dtype)
                if has_lse:
                    lses = [LN2 * (row_max[b, p, half] + jnp.log2(denoms[b, p, half])) for half in range(2)]
                    lse_refs[p][cur, :] = jnp.where(lo, lses[0], lses[1])
        return carry

    lax.fori_loop(0, dil * n_blocks // blocks_per_iter, body, 0)


def _attn(slopes, sinks, q_arr, k_arr, v_arr, q_idx, k_idx, v_idx, *, n_seq, seq, k_pair, dil, head0,
          out_dtype, has_lse, name):
    has_sink = sinks is not None
    n_pairs = len(q_idx)

    def pair_spec(idx):
        return pl.BlockSpec((None, seq, LANES), lambda n: (n, 0, idx))

    view = lambda a: a.reshape(n_seq, seq, a.shape[-1])
    operands = [slopes] + ([sinks] if has_sink else [])
    operands += [view(q_arr)] * n_pairs + [view(k_arr)] * len(k_idx) + [view(v_arr)] * len(v_idx)
    in_specs = [_smem_spec()] * (2 if has_sink else 1)
    in_specs += [pair_spec(i) for i in tuple(q_idx) + tuple(k_idx) + tuple(v_idx)]
    out_shape = [jax.ShapeDtypeStruct((n_seq, seq, LANES), out_dtype)] * n_pairs
    if has_lse:
        out_shape += [jax.ShapeDtypeStruct((n_seq, seq, LANES), F32)] * n_pairs
    outs = pl.pallas_call(
        functools.partial(_attn_kernel, n_pairs=n_pairs, k_pair=k_pair, dil=dil, seq=seq, head0=head0,
                          has_sink=has_sink, has_lse=has_lse),
        grid=(n_seq,),
        in_specs=in_specs,
        out_specs=[pair_spec(0)] * len(out_shape),
        out_shape=out_shape,
        scratch_shapes=[pltpu.VMEM((2 * n_pairs, BLOCK, BLOCK) if seq == BLOCK * dil else
                                   (4 * n_pairs, BLOCK, 2 * BLOCK), F32)],
        compiler_params=_params(1),
        name=name,
    )(*operands)
    outs = [o.reshape(n_seq * seq, LANES) for o in outs]
    return outs[:n_pairs], outs[n_pairs:]


def _rows_where(row_iota, values):
    out = jnp.zeros(row_iota.shape, F32)
    for r, v in enumerate(values):
        out = jnp.where(row_iota == r, v, out)
    return out


def _shift_cache(cache_ref, tail_ref, out_ref, n, length, rows):
    shifted = pltpu.roll(cache_ref[rows, :], length - 1, axis=1)
    new_col = pltpu.roll(tail_ref[rows, :], (LANES - 1) - n, axis=1)
    lane = lax.broadcasted_iota(jnp.int32, new_col.shape, 1)
    last = jnp.where(lane == LANES - 1, new_col, shifted[:, length - LANES:])
    out_ref[rows, :] = last if length == LANES else jnp.concatenate([shifted[:, :length - LANES], last], axis=1)


def _sample_stages(slopes_ref, sinks_ref, qa_ref, kva_ref, b0_ref, b12_ref,
                   ta_ref, tb0_ref, tb1_ref, tb2_ref, ca_ref, cb0_ref, cb1_ref, cb2_ref,
                   oa_ref, ob_ref, lse_ref, na_ref, nb0_ref, nb1_ref, nb2_ref):
    n = pl.program_id(0)
    row = pl.ds(n, 1)
    sub = lax.broadcasted_iota(jnp.int32, (8, LANES), 0)
    lane = lax.broadcasted_iota(jnp.int32, (8, LANES), 1)
    half = lane // HEAD_DIM
    kv_head = sub // A_GROUP
    sub_b = lax.broadcasted_iota(jnp.int32, (8, B_GW), 0)
    head_mask = (lax.broadcasted_iota(jnp.int32, (8, B_GW), 1) // HEAD_DIM) == sub_b
    b_refs = ((cb0_ref, tb0_ref, nb0_ref), (cb1_ref, tb1_ref, nb1_ref), (cb2_ref, tb2_ref, nb2_ref))
    state = {}

    def scores():
        q_row = qa_ref[row, :]
        q_blk = jnp.zeros((8, LANES), F32)
        for r in range(A_Q_HEADS):
            k, g = divmod(r, A_GROUP)
            chunk = q_row[:, (r // 2) * LANES:(r // 2 + 1) * LANES]
            if g % 2 != k:
                chunk = pltpu.roll(chunk, HEAD_DIM, axis=1)
            q_blk = jnp.where((sub == r) & (half == k), jnp.broadcast_to(chunk, (8, LANES)), q_blk)
        kva = kva_ref[row, :]
        k_new = jnp.where(lane[:1] < HEAD_DIM, kva[:, 0:LANES], kva[:, LANES:2 * LANES])
        v_new = jnp.where(lane[:1] < HEAD_DIM, kva[:, 2 * LANES:3 * LANES], kva[:, 3 * LANES:])
        s = _dot(q_blk.astype(BF16), ca_ref[0:A_KVW, :].astype(BF16))
        state["a"] = (s, jnp.sum(q_blk * k_new, axis=-1, keepdims=True), v_new)
        for g, (c_ref, _, _) in enumerate(b_refs):
            qkv = b0_ref[row, :] if g == 0 else b12_ref[row, (g - 1) * 3 * B_GW:g * 3 * B_GW]
            q_row, k_new, v_new = qkv[:, 0:B_GW], qkv[:, B_GW:2 * B_GW], qkv[:, 2 * B_GW:]
            q_blk = jnp.where(head_mask, jnp.broadcast_to(q_row, (8, B_GW)), 0.0)
            s = _dot(q_blk.astype(BF16), c_ref[0:B_GW, :].astype(BF16))
            state["b", g] = (s, jnp.sum(q_blk * k_new, axis=-1, keepdims=True), v_new)

    def softmaxes():
        s, s_new, v_new = state["a"]
        slope = _rows_where(sub[:, :1], [slopes_ref[r] * LOG2E for r in range(A_Q_HEADS)])
        sink = _rows_where(sub[:, :1], [sinks_ref[r] * LOG2E for r in range(A_Q_HEADS)])
        s = s - slope * (BLOCK - lane).astype(F32)
        m = jnp.maximum(jnp.maximum(jnp.max(s, axis=-1, keepdims=True), s_new), sink)
        e = jnp.exp2(s - m)
        e_new = jnp.exp2(s_new - m)
        denom = jnp.sum(e, axis=-1, keepdims=True) + e_new + jnp.exp2(sink - m)
        state["a"] = (e.astype(BF16), e_new, v_new, m, denom)
        for g, (length, dil) in enumerate(B_PATTERNS):
            s, s_new, v_new = state["b", g]
            head0 = A_Q_HEADS + g * B_HEADS_PER_GROUP
            sub_l = lax.broadcasted_iota(jnp.int32, (8, length), 0)
            col = lax.broadcasted_iota(jnp.int32, (8, length), 1)
            slope = _rows_where(sub_l, [slopes_ref[head0 + h] * LOG2E for h in range(B_HEADS_PER_GROUP)])
            s = jnp.where((col & (dil - 1)) == 0, s - slope * (length - col).astype(F32), NEG)
            m = jnp.maximum(jnp.max(s, axis=-1, keepdims=True), s_new)
            e = jnp.exp2(s - m)
            e_new = jnp.exp2(s_new - m)
            denom = jnp.sum(e, axis=-1, keepdims=True) + e_new
            state["b", g] = (e.astype(BF16), e_new, v_new, m, denom)

    def values():
        e, e_new, v_new, m, denom = state["a"]
        acc = _dot_nt(e, ca_ref[A_KVW:2 * A_KVW, :].astype(BF16))
        out = jnp.where(half == kv_head, (acc + e_new * v_new) / denom, 0.0)
        for c in range(A_W // LANES):
            k = c // 2
            pieces = []
            for hh in range(2):
                r = k * A_GROUP + 2 * (c % 2) + hh
                piece = out[r:r + 1, :]
                pieces.append(piece if hh == k else pltpu.roll(piece, HEAD_DIM, axis=1))
            oa_ref[:, c * LANES:(c + 1) * LANES] = jnp.where(lane[:1] < HEAD_DIM, pieces[0], pieces[1])
        for g, (c_ref, _, _) in enumerate(b_refs):
            e, e_new, v_new, m, denom = state["b", g]
            acc = _dot_nt(e, c_ref[B_GW:2 * B_GW, :].astype(BF16))
            out = jnp.where(head_mask, (acc + e_new * v_new) / denom, 0.0)
            ob_ref[g] = jnp.sum(out, axis=0, keepdims=True)
            lse = jnp.where(head_mask, LN2 * (m + jnp.log2(denom)), 0.0)
            lse_ref[g] = jnp.sum(lse, axis=0, keepdims=True)

    def shift(c_ref, t_ref, n_ref, length, piece, n_pieces):
        rows_per = c_ref.shape[0] // n_pieces
        return lambda: _shift_cache(c_ref, t_ref, n_ref, n, length, slice(piece * rows_per, (piece + 1) * rows_per))

    def shift_small():
        shift(ca_ref, ta_ref, na_ref, BLOCK, 0, 1)()
        shift(cb0_ref, tb0_ref, nb0_ref, B_PATTERNS[0][0], 0, 1)()

    stages = [scores, shift_small, softmaxes, shift(cb1_ref, tb1_ref, nb1_ref, B_PATTERNS[1][0], 0, 1), values]
    stages += [shift(cb2_ref, tb2_ref, nb2_ref, B_PATTERNS[2][0], piece, _N_SHIFT_PIECES)
               for piece in range(_N_SHIFT_PIECES)]
    return stages


_N_SHIFT_PIECES = 4


_N_FFN_INPUTS = 5
_N_SAMPLE_INPUTS = 14
_N_SAMPLE_OUTPUTS = 7


def _ffn_sample_kernel(*refs):
    ffn_in, refs = refs[:_N_FFN_INPUTS], refs[_N_FFN_INPUTS:]
    sample_in, refs = refs[:_N_SAMPLE_INPUTS], refs[_N_SAMPLE_INPUTS:]
    o_ref, sample_out, (act_ref,) = refs[0], refs[1:1 + _N_SAMPLE_OUTPUTS], refs[1 + _N_SAMPLE_OUTPUTS:]
    x_ref, gain_ref, wg_ref, wu_ref, wd_ref = ffn_in
    o_ref[...] = _swiglu_residual(x_ref[...], gain_ref[...], wg_ref, wu_ref, wd_ref, act_ref,
                                  side_work=_sample_stages(*sample_in, *sample_out))


def _ffn_and_sample(x, gain, wg, wu, wd, slopes, sinks, qa, kva, b0, b12, tails, caches):
    n_seq = qa.shape[0]
    t = x.shape[0]
    assert t % n_seq == 0
    tm = t // n_seq
    full = lambda a: pl.BlockSpec(a.shape, lambda n: (0,) * a.ndim)
    per_seq = lambda a: pl.BlockSpec((None,) + a.shape[1:], lambda n: (n, 0, 0))
    small = [qa, kva, b0, b12] + list(tails)
    assert 2 + len(small) + len(caches) == _N_SAMPLE_INPUTS
    out_shape = [jax.ShapeDtypeStruct((t, D_MODEL), F32),
                 jax.ShapeDtypeStruct((n_seq, 1, A_W), F32),
                 jax.ShapeDtypeStruct((B_N_GROUPS, n_seq, 1, B_GW), F32),
                 jax.ShapeDtypeStruct((B_N_GROUPS, n_seq, 1, B_GW), F32)]
    out_shape += [jax.ShapeDtypeStruct(c.shape, F32) for c in caches]
    out_specs = [pl.BlockSpec((tm, D_MODEL), lambda n: (n, 0)),
                 pl.BlockSpec((None, 1, A_W), lambda n: (n, 0, 0)),
                 pl.BlockSpec((B_N_GROUPS, None, 1, B_GW), lambda n: (0, n, 0, 0)),
                 pl.BlockSpec((B_N_GROUPS, None, 1, B_GW), lambda n: (0, n, 0, 0))]
    out_specs += [per_seq(c) for c in caches]
    x1, oa, ob, lse, *new_caches = pl.pallas_call(
        _ffn_sample_kernel,
        grid=(n_seq,),
        in_specs=[pl.BlockSpec((tm, D_MODEL), lambda n: (n, 0)),
                  _const_spec((1, D_MODEL)), _const_spec((D_MODEL, D_FF)),
                  _const_spec((D_MODEL, D_FF)), _const_spec((D_FF, D_MODEL))]
                 + [_smem_spec(), _smem_spec()] + [full(a) for a in small] + [per_seq(c) for c in caches],
        out_specs=out_specs,
        out_shape=out_shape,
        scratch_shapes=[pltpu.VMEM((tm, D_FF), BF16)],
        compiler_params=_params(1),
        name="ffn1_sample",
    )(x, gain, wg, wu, wd, slopes, sinks, *small, *caches)
    return (x1, oa.reshape(n_seq, A_W), ob.reshape(B_N_GROUPS, n_seq, B_GW),
            lse.reshape(B_N_GROUPS, n_seq, B_GW), *new_caches)


_N_A_PAIRS = A_W // LANES
_N_B_PAIRS = B_GW // LANES


def _merge_kernel(*refs):
    refs = list(refs)
    take = lambda count: [refs.pop(0) for _ in range(count)]
    (x_ref,), oa_refs = take(1), take(_N_A_PAIRS)
    ob_refs, lse_refs = take(B_N_GROUPS * _N_B_PAIRS), take(B_N_GROUPS * _N_B_PAIRS)
    gates_ref, wua_ref, wub_ref, wo_ref, gain_ref, wg_ref, wu_ref, wd_ref, y_ref, act_ref = refs

    ob_pairs = []
    for p in range(_N_B_PAIRS):
        lses = [lse_refs[g * _N_B_PAIRS + p][...] for g in range(B_N_GROUPS)]
        m = functools.reduce(jnp.maximum, lses)
        es = [jnp.exp(l - m) for l in lses]
        num = sum(e * ob_refs[g * _N_B_PAIRS + p][...] for g, e in enumerate(es))
        ob_pairs.append((num / sum(es)).astype(BF16))
    ua = _dot(jnp.concatenate([r[...] for r in oa_refs], axis=1), wua_ref[...])
    ub = _dot(jnp.concatenate(ob_pairs, axis=1), wub_ref[...])
    gate_a = gates_ref[:, :D_MODEL].astype(F32)
    gate_b = gates_ref[:, D_MODEL:].astype(F32)
    mixed = (gate_a * ua + gate_b * ub).astype(BF16)
    x = x_ref[...] + _dot(mixed, wo_ref[...])
    y_ref[...] = _swiglu_residual(x, gain_ref[...], wg_ref, wu_ref, wd_ref, act_ref)


def _merge(x, oa_pairs, ob_pairs, lse_pairs, gates, wua, wub, wo, gain, wg, wu, wd, tm):
    t = x.shape[0]
    row_spec = lambda w: pl.BlockSpec((tm, w), lambda i: (i, 0))
    n_pair_inputs = len(oa_pairs) + len(ob_pairs) + len(lse_pairs)
    return pl.pallas_call(
        _merge_kernel,
        grid=(t // tm,),
        in_specs=[row_spec(D_MODEL)] + [row_spec(LANES)] * n_pair_inputs + [row_spec(2 * D_MODEL),
                  _const_spec((A_W, D_MODEL)), _const_spec((B_GW, D_MODEL)), _const_spec((D_MODEL, D_MODEL)),
                  _const_spec((1, D_MODEL)), _const_spec((D_MODEL, D_FF)), _const_spec((D_MODEL, D_FF)),
                  _const_spec((D_FF, D_MODEL))],
        out_specs=row_spec(D_MODEL),
        out_shape=jax.ShapeDtypeStruct((t, D_MODEL), F32),
        scratch_shapes=[pltpu.VMEM((tm, D_FF), BF16)],
        compiler_params=_params(1),
        name="merge_ffn2",
    )(x, *oa_pairs, *ob_pairs, *lse_pairs, gates, wua, wub, wo, gain, wg, wu, wd)


def _cache_view(cache):
    _, n, length, two, h, d = cache.shape
    return jnp.transpose(cache, (0, 1, 3, 4, 5, 2)).reshape(n, two * h * d, length)


def _state_view(rows_by_len, heads):
    n, _, length = rows_by_len.shape
    return jnp.transpose(rows_by_len.reshape(1, n, 2, heads, HEAD_DIM, length), (0, 1, 5, 2, 3, 4))


def kernel(x_prompt, x_sample, cache_a_kv, cache_b1_kv, cache_b2_kv, cache_b3_kv, norm_ffn1, w1_gate, w1_up,
           w1_down, norm_mix, w_in, q_norm_a, k_norm_a, q_norm_b, k_norm_b, sinks_a, w_up_a, w_up_b, w_o,
           norm_ffn2, w2_gate, w2_up, w2_down):
    assert x_prompt.shape[-1] == D_MODEL and w_in.shape == (1, D_MODEL, IN_W)
    batch, seq, _ = x_prompt.shape
    dec = x_sample.shape[0]
    assert x_sample.shape[1] == 1 and seq % (BLOCK * B_PATTERNS[-1][1]) == 0
    assert cache_a_kv.shape[2] == BLOCK
    assert all(c.shape[2] == w for c, (w, _) in zip((cache_b1_kv, cache_b2_kv, cache_b3_kv), B_PATTERNS))

    bf = lambda w: w[0].astype(BF16)
    wg1, wu1, wd1, wg2, wu2, wd2 = map(bf, (w1_gate, w1_up, w1_down, w2_gate, w2_up, w2_down))
    w_in_b, wua, wub, wo = map(bf, (w_in, w_up_a, w_up_b, w_o))

    i = jnp.arange(1, N_ALIBI_HEADS + 1, dtype=F32)
    slopes = jnp.exp2(-8.0 * i / N_ALIBI_HEADS)
    sinks = sinks_a[0].reshape(A_Q_HEADS).astype(F32)
    ones64 = jnp.ones((HEAD_DIM,), F32)
    q_scale = ATTN_SCALE * LOG2E
    qkgain = jnp.concatenate([
        jnp.tile(q_norm_a[0] * q_scale, A_Q_HEADS), jnp.tile(k_norm_a[0], A_KV_HEADS),
        jnp.tile(ones64, A_KV_HEADS),
        jnp.tile(q_norm_b[0] * q_scale, B_N_GROUPS * B_HEADS_PER_GROUP),
        jnp.tile(k_norm_b[0], B_N_GROUPS * B_HEADS_PER_GROUP),
        jnp.tile(ones64, B_N_GROUPS * B_HEADS_PER_GROUP)]).reshape(1, QKV_W).astype(F32)
    head_of = jnp.arange(CHUNK) // HEAD_DIM
    ones_bd = (head_of[:, None] == head_of[None, :]).astype(BF16)

    def proj(x1, n_seq, s, tm):
        return _proj(x1, norm_mix, w_in_b, qkgain, ones_bd, n_seq, s, tm)

    def back(x1, oa, obs, lses, gates, tm):
        return _merge(x1, oa, obs, lses, gates, wua, wub, wo, norm_ffn2, wg2, wu2, wd2, tm)

    x1s = _ffn(x_sample.reshape(dec, D_MODEL), norm_ffn1, wg1, wu1, wd1, dec)
    qa_s, kva_s, b0_s, b12_s, gates_s, *tails_s = proj(x1s, 1, dec, dec)
    caches = [_cache_view(c) for c in (cache_a_kv, cache_b1_kv, cache_b2_kv, cache_b3_kv)]
    x1p, oa_s, ob_s, lse_s, na, nb0, nb1, nb2 = _ffn_and_sample(
        x_prompt.reshape(batch * seq, D_MODEL), norm_ffn1, wg1, wu1, wd1,
        slopes, sinks, qa_s.astype(F32), kva_s.astype(F32), b0_s.astype(F32), b12_s,
        [t[0] for t in tails_s], caches)

    qa, kva, b0, b12, gates, ta, tb0, tb1, tb2 = proj(x1p, batch, seq, 512)
    common = dict(n_seq=batch, seq=seq)
    oa, _ = _attn(slopes, sinks, qa, kva, kva, (0, 1, 2, 3), (0, 1), (2, 3), k_pair=(0, 0, 1, 1), dil=1,
                  head0=0, out_dtype=BF16, has_lse=False, name="attn_a", **common)
    obs, lses = [], []
    for g, (_, dil) in enumerate(B_PATTERNS):
        src = b0 if g == 0 else b12
        first = 0 if g == 0 else (g - 1) * 3 * _N_B_PAIRS
        ob, lse = _attn(slopes, None, src, src, src, (first, first + 1), (first + 2, first + 3),
                        (first + 4, first + 5), k_pair=(0, 1), dil=dil,
                        head0=A_Q_HEADS + g * B_HEADS_PER_GROUP, out_dtype=F32, has_lse=True,
                        name=f"attn_b{g}", **common)
        obs += ob
        lses += lse
    y_prompt = back(x1p, oa, obs, lses, gates, 512).reshape(batch, seq, D_MODEL)

    pairs = lambda a: [a[..., p * LANES:(p + 1) * LANES] for p in range(a.shape[-1] // LANES)]
    ob_s = [pair for g in range(B_N_GROUPS) for pair in pairs(ob_s[g])]
    lse_s = [pair for g in range(B_N_GROUPS) for pair in pairs(lse_s[g])]
    y_sample = back(x1s, pairs(oa_s.astype(BF16)), ob_s, lse_s, gates_s, dec).reshape(dec, 1, D_MODEL)

    return (y_prompt, y_sample,
            _state_view(ta, A_KV_HEADS), _state_view(tb0, B_HEADS_PER_GROUP),
            _state_view(tb1, B_HEADS_PER_GROUP), _state_view(tb2, B_HEADS_PER_GROUP),
            _state_view(na, A_KV_HEADS), _state_view(nb0, B_HEADS_PER_GROUP),
            _state_view(nb1, B_HEADS_PER_GROUP), _state_view(nb2, B_HEADS_PER_GROUP))
```

```python
import functools
import math

import jax
import jax.numpy as jnp
from jax import lax
from jax.experimental import pallas as pl
from jax.experimental.pallas import tpu as pltpu

D_MODEL = 1024
D_FF = 2816
HEAD_DIM = 64
A_Q_HEADS = 8
A_KV_HEADS = 2
A_GROUP = A_Q_HEADS // A_KV_HEADS
B_PATTERNS = ((128, 1), (512, 4), (2048, 16))
B_HEADS_PER_GROUP = 4
B_N_GROUPS = 3
N_ALIBI_HEADS = A_Q_HEADS + B_N_GROUPS * B_HEADS_PER_GROUP
BLOCK = 128
EPS = 1e-6
ATTN_SCALE = HEAD_DIM ** -0.5
LOG2E = math.log2(math.e)
LN2 = math.log(2.0)
A_W = A_Q_HEADS * HEAD_DIM
A_KVW = A_KV_HEADS * HEAD_DIM
B_GW = B_HEADS_PER_GROUP * HEAD_DIM
B_W = B_N_GROUPS * B_GW
QKV_W = A_W + 2 * A_KVW + 3 * B_W
IN_W = QKV_W + 2 * D_MODEL

LANES = 128
CHUNK = 256
NEG = -1e30
VMEM_LIMIT = 56 * 1024 * 1024

F32 = jnp.float32
BF16 = jnp.bfloat16


def _const_spec(shape):
    nd = len(shape)
    return pl.BlockSpec(shape, lambda *_: (0,) * nd, pipeline_mode=pl.Buffered(1))


def _smem_spec():
    return pl.BlockSpec(memory_space=pltpu.SMEM)


def _params(n_axes):
    return pltpu.CompilerParams(
        dimension_semantics=("arbitrary",) * n_axes, vmem_limit_bytes=VMEM_LIMIT)


def _rmsnorm(x, g):
    return x * lax.rsqrt(jnp.mean(x * x, axis=-1, keepdims=True) + EPS) * g


def _dot(a, b):
    return jnp.dot(a, b, preferred_element_type=F32)


def _dot_nt(a, b):
    return lax.dot_general(a, b, (((1,), (1,)), ((), ())), preferred_element_type=F32)


def _swiglu_residual(x, gain, wg_ref, wu_ref, wd_ref, act_ref, side_work=()):
    n_chunks = D_FF // CHUNK
    assert len(side_work) <= n_chunks
    h = _rmsnorm(x, gain).astype(BF16)
    for c in range(n_chunks):
        if c < len(side_work):
            side_work[c]()
        sl = slice(c * CHUNK, (c + 1) * CHUNK)
        g = _dot(h, wg_ref[:, sl])
        u = _dot(h, wu_ref[:, sl])
        act_ref[:, sl] = (g * jax.nn.sigmoid(g) * u).astype(BF16)
    return x + 0.5 * _dot(act_ref[...], wd_ref[...])


def _ffn_kernel(x_ref, gain_ref, wg_ref, wu_ref, wd_ref, o_ref, act_ref):
    o_ref[...] = _swiglu_residual(x_ref[...], gain_ref[...], wg_ref, wu_ref, wd_ref, act_ref)


def _ffn(x, gain, wg, wu, wd, tm):
    t = x.shape[0]
    return pl.pallas_call(
        _ffn_kernel,
        grid=(t // tm,),
        in_specs=[pl.BlockSpec((tm, D_MODEL), lambda i: (i, 0)),
                  _const_spec((1, D_MODEL)), _const_spec((D_MODEL, D_FF)),
                  _const_spec((D_MODEL, D_FF)), _const_spec((D_FF, D_MODEL))],
        out_specs=pl.BlockSpec((tm, D_MODEL), lambda i: (i, 0)),
        out_shape=jax.ShapeDtypeStruct((t, D_MODEL), F32),
        scratch_shapes=[pltpu.VMEM((tm, D_FF), BF16)],
        compiler_params=_params(1),
        name="ffn1",
    )(x, gain, wg, wu, wd)


_N_QKV_CHUNKS = QKV_W // CHUNK
_N_CHUNKS = IN_W // CHUNK


def _tail_plan(tail, tm, n_tiles):
    if tail >= tm:
        return tm, n_tiles - tail // tm
    return tail, n_tiles - 1


def _proj_kernel(x_ref, gain_ref, w_ref, qkgain_ref, ones_ref,
                 qa_ref, kva_ref, b0_ref, b12_ref, gates_ref, ta_ref, tb0_ref, tb1_ref, tb2_ref,
                 *, tm, n_tiles, tails):
    h = _rmsnorm(x_ref[...], gain_ref[...]).astype(BF16)
    lane = lax.broadcasted_iota(jnp.int32, (tm, CHUNK), 1)
    lane1 = lax.broadcasted_iota(jnp.int32, (tm, LANES), 1)

    def z_chunk(c):
        return _dot(h, w_ref[:, c * CHUNK:(c + 1) * CHUNK])

    def head_norm(z, c):
        ss = _dot((z * z).astype(BF16), ones_ref[...])
        return z * lax.rsqrt(ss * (1.0 / HEAD_DIM) + EPS) * qkgain_ref[:, c * CHUNK:(c + 1) * CHUNK]

    def write_tail(t_ref, row0, y, tail):
        block_w, _ = _tail_plan(tail, tm, n_tiles)
        data = y if block_w == tm else y[tm - block_w:, :]
        t_ref[row0:row0 + CHUNK, :] = data.T

    tb_refs = (tb0_ref, tb1_ref, tb2_ref)

    def consume(c, z):
        if c < 2:
            qa_ref[:, c * CHUNK:(c + 1) * CHUNK] = head_norm(z, c).astype(BF16)
        elif c == 2:
            y = jnp.where(lane < A_KVW, head_norm(z, c), z)
            write_tail(ta_ref, 0, y, tails[0])
            for part in range(2):
                pair = y[:, part * LANES:(part + 1) * LANES]
                swapped = pltpu.roll(pair, HEAD_DIM, axis=1)
                base = part * 2 * LANES
                kva_ref[:, base:base + LANES] = jnp.where(lane1 < HEAD_DIM, pair, swapped).astype(BF16)
                kva_ref[:, base + LANES:base + 2 * LANES] = jnp.where(lane1 < HEAD_DIM, swapped, pair).astype(BF16)
        elif c < _N_QKV_CHUNKS:
            kind, g = divmod(c - 3, B_N_GROUPS)
            y = z if kind == 2 else head_norm(z, c)
            if g == 0:
                b0_ref[:, kind * CHUNK:(kind + 1) * CHUNK] = y.astype(BF16)
            else:
                col = ((g - 1) * 3 + kind) * CHUNK
                b12_ref[:, col:col + CHUNK] = y
            if kind > 0:
                write_tail(tb_refs[g], (kind - 1) * CHUNK, y, tails[1 + g])
        else:
            col = (c - _N_QKV_CHUNKS) * CHUNK
            gates_ref[:, col:col + CHUNK] = jax.nn.sigmoid(z).astype(BF16)

    z_next = z_chunk(0)
    for c in range(_N_CHUNKS):
        z = z_next
        if c + 1 < _N_CHUNKS:
            z_next = z_chunk(c + 1)
        consume(c, z)


def _proj(x, gain, w_in, qkgain, ones_bd, n_seq, seq, tm):
    t = n_seq * seq
    n_tiles = seq // tm
    tails = (min(128, seq),) + tuple(min(w, seq) for w, _ in B_PATTERNS)
    tail_rows = (2 * A_KVW, 2 * B_GW, 2 * B_GW, 2 * B_GW)

    def tail_spec(rows, tail):
        block_w, first = _tail_plan(tail, tm, n_tiles)
        return pl.BlockSpec((None, rows, block_w), lambda n, j: (n, 0, jnp.maximum(j - first, 0)))

    row_spec = lambda w: pl.BlockSpec((tm, w), lambda n, j: (n * n_tiles + j, 0))
    out_shape = [
        jax.ShapeDtypeStruct((t, A_W), BF16),
        jax.ShapeDtypeStruct((t, 4 * LANES), BF16),
        jax.ShapeDtypeStruct((t, 3 * B_GW), BF16),
        jax.ShapeDtypeStruct((t, 6 * B_GW), F32),
        jax.ShapeDtypeStruct((t, 2 * D_MODEL), BF16),
    ] + [jax.ShapeDtypeStruct((n_seq, r, tl), F32) for r, tl in zip(tail_rows, tails)]
    out_specs = [row_spec(A_W), row_spec(4 * LANES), row_spec(3 * B_GW), row_spec(6 * B_GW),
                 row_spec(2 * D_MODEL)] + [tail_spec(r, tl) for r, tl in zip(tail_rows, tails)]
    return pl.pallas_call(
        functools.partial(_proj_kernel, tm=tm, n_tiles=n_tiles, tails=tails),
        grid=(n_seq, n_tiles),
        in_specs=[row_spec(D_MODEL), _const_spec((1, D_MODEL)), _const_spec((D_MODEL, IN_W)),
                  _const_spec((1, QKV_W)), _const_spec((CHUNK, CHUNK))],
        out_specs=out_specs,
        out_shape=out_shape,
        compiler_params=_params(2),
        name="proj",
    )(x, gain, w_in, qkgain, ones_bd)


_HEADS_PER_ITER = 32


def _attn_kernel(*refs, n_pairs, k_pair, dil, seq, head0, has_sink, has_max):
    refs = list(refs)
    slopes_ref = refs.pop(0)
    sinks_ref = refs.pop(0) if has_sink else None
    n_kv = max(k_pair) + 1
    take = lambda count: [refs.pop(0) for _ in range(count)]
    q_refs, k_refs, v_refs, o_refs, den_refs = take(n_pairs), take(n_kv), take(n_kv), take(n_pairs), take(n_pairs)
    max_refs = take(n_pairs) if has_max else None
    (bias_ref,) = refs

    n_blocks = seq // (BLOCK * dil)
    use_prev = n_blocks > 1
    n_heads = 2 * n_pairs
    qi = lax.broadcasted_iota(jnp.int32, (BLOCK, BLOCK), 0)
    ki = lax.broadcasted_iota(jnp.int32, (BLOCK, BLOCK), 1)
    d_cur = qi - ki
    lo = ki < HEAD_DIM
    lo_keys = lax.broadcasted_iota(jnp.int32, ((2 if use_prev else 1) * BLOCK, LANES), 1) < HEAD_DIM

    for h in range(n_heads):
        slope = slopes_ref[head0 + h] * (float(dil) * LOG2E)
        cur_bias = jnp.where(d_cur >= 0, -slope * d_cur.astype(F32), NEG)
        if use_prev:
            prev_bias = jnp.where(d_cur <= 0, -slope * (d_cur + BLOCK).astype(F32), NEG)
            bias_ref[h] = jnp.concatenate([prev_bias, cur_bias], axis=1)
            bias_ref[n_heads + h] = jnp.concatenate([jnp.full_like(prev_bias, NEG), cur_bias], axis=1)
        else:
            bias_ref[h] = cur_bias

    def block_rows(step):
        if n_blocks == 1:
            r, j = step, 0
        else:
            r, j = step // n_blocks, step % n_blocks
        if dil == 1:
            cur = pl.ds(pl.multiple_of(j * BLOCK, BLOCK), BLOCK)
            prev = pl.ds(pl.multiple_of(jnp.maximum(j - 1, 0) * BLOCK, BLOCK), BLOCK)
        else:
            cur = pl.ds(r + j * (BLOCK * dil), BLOCK, stride=dil)
            prev = pl.ds(r + jnp.maximum(j - 1, 0) * (BLOCK * dil), BLOCK, stride=dil)
        table = jnp.where(j == 0, n_heads, 0) if use_prev else 0
        return cur, prev, table

    blocks_per_iter = _HEADS_PER_ITER // (2 * n_pairs)
    assert (dil * n_blocks) % blocks_per_iter == 0

    def body(it, carry):
        blocks = [block_rows(it * blocks_per_iter + b) for b in range(blocks_per_iter)]
        scores, values = {}, {}
        for b, (cur, prev, _) in enumerate(blocks):
            for p in range(n_pairs):
                k_ref, v_ref = k_refs[k_pair[p]], v_refs[k_pair[p]]
                q = q_refs[p][cur, :].astype(BF16)
                keys = k_ref[cur, :].astype(BF16)
                vals = v_ref[cur, :].astype(BF16)
                if use_prev:
                    keys = jnp.concatenate([k_ref[prev, :].astype(BF16), keys], axis=0)
                    vals = jnp.concatenate([v_ref[prev, :].astype(BF16), vals], axis=0)
                one = jnp.ones_like(vals)
                values[b, p, 0] = jnp.where(lo_keys, vals, one)
                values[b, p, 1] = jnp.where(lo_keys, one, vals)
                for half in range(2):
                    qm = jnp.where(lo if half == 0 else jnp.logical_not(lo), q, jnp.zeros_like(q))
                    scores[b, p, half] = _dot_nt(qm, keys)
        probs, row_max, sink_terms = {}, {}, {}
        for (b, p, half), s in scores.items():
            head = 2 * p + half
            s = s + bias_ref[blocks[b][2] + head]
            m = jnp.max(s, axis=-1, keepdims=True)
            if has_sink:
                sink = sinks_ref[head] * LOG2E
                m = jnp.maximum(m, sink)
                sink_terms[b, p, half] = jnp.exp2(sink - m)
            probs[b, p, half], row_max[b, p, half] = jnp.exp2(s - m).astype(BF16), m
        for b, (cur, _, _) in enumerate(blocks):
            for p in range(n_pairs):
                acc = [_dot(probs[b, p, half], values[b, p, half]) for half in range(2)]
                o_refs[p][cur, :] = jnp.where(lo, acc[0], acc[1]).astype(o_refs[p].dtype)
                denom = jnp.where(lo, acc[1], acc[0])
                if has_sink:
                    denom = denom + jnp.where(lo, sink_terms[b, p, 1], sink_terms[b, p, 0])
                den_refs[p][cur, :] = denom
                if has_max:
                    max_refs[p][cur, :] = jnp.where(lo, row_max[b, p, 0], row_max[b, p, 1])
        return carry

    lax.fori_loop(0, dil * n_blocks // blocks_per_iter, body, 0)


def _attn(slopes, sinks, q_arr, k_arr, v_arr, q_idx, k_idx, v_idx, *, n_seq, seq, k_pair, dil, head0,
          out_dtype, has_max, name):
    has_sink = sinks is not None
    n_pairs = len(q_idx)

    def pair_spec(idx):
        return pl.BlockSpec((None, seq, LANES), lambda n: (n, 0, idx))

    view = lambda a: a.reshape(n_seq, seq, a.shape[-1])
    operands = [slopes] + ([sinks] if has_sink else [])
    operands += [view(q_arr)] * n_pairs + [view(k_arr)] * len(k_idx) + [view(v_arr)] * len(v_idx)
    in_specs = [_smem_spec()] * (2 if has_sink else 1)
    in_specs += [pair_spec(i) for i in tuple(q_idx) + tuple(k_idx) + tuple(v_idx)]
    out_shape = [jax.ShapeDtypeStruct((n_seq, seq, LANES), out_dtype)] * n_pairs
    out_shape += [jax.ShapeDtypeStruct((n_seq, seq, LANES), F32)] * (n_pairs * (2 if has_max else 1))
    outs = pl.pallas_call(
        functools.partial(_attn_kernel, n_pairs=n_pairs, k_pair=k_pair, dil=dil, seq=seq, head0=head0,
                          has_sink=has_sink, has_max=has_max),
        grid=(n_seq,),
        in_specs=in_specs,
        out_specs=[pair_spec(0)] * len(out_shape),
        out_shape=out_shape,
        scratch_shapes=[pltpu.VMEM((2 * n_pairs, BLOCK, BLOCK) if seq == BLOCK * dil else
                                   (4 * n_pairs, BLOCK, 2 * BLOCK), F32)],
        compiler_params=_params(1),
        name=name,
    )(*operands)
    outs = [o.reshape(n_seq * seq, LANES) for o in outs]
    return outs[:n_pairs], outs[n_pairs:2 * n_pairs], outs[2 * n_pairs:]


def _rows_where(row_iota, values):
    out = jnp.zeros(row_iota.shape, F32)
    for r, v in enumerate(values):
        out = jnp.where(row_iota == r, v, out)
    return out


def _shift_cache(cache_ref, tail_ref, out_ref, n, length, rows):
    shifted = pltpu.roll(cache_ref[rows, :], length - 1, axis=1)
    new_col = pltpu.roll(tail_ref[rows, :], (LANES - 1) - n, axis=1)
    lane = lax.broadcasted_iota(jnp.int32, new_col.shape, 1)
    last = jnp.where(lane == LANES - 1, new_col, shifted[:, length - LANES:])
    out_ref[rows, :] = last if length == LANES else jnp.concatenate([shifted[:, :length - LANES], last], axis=1)


def _sample_stages(slopes_ref, sinks_ref, qa_ref, kva_ref, b0_ref, b12_ref,
                   ta_ref, tb0_ref, tb1_ref, tb2_ref, ca_ref, cb0_ref, cb1_ref, cb2_ref,
                   oa_ref, ob_ref, lse_ref, na_ref, nb0_ref, nb1_ref, nb2_ref):
    n = pl.program_id(0)
    row = pl.ds(n, 1)
    sub = lax.broadcasted_iota(jnp.int32, (8, LANES), 0)
    lane = lax.broadcasted_iota(jnp.int32, (8, LANES), 1)
    half = lane // HEAD_DIM
    kv_head = sub // A_GROUP
    sub_b = lax.broadcasted_iota(jnp.int32, (8, B_GW), 0)
    head_mask = (lax.broadcasted_iota(jnp.int32, (8, B_GW), 1) // HEAD_DIM) == sub_b
    b_refs = ((cb0_ref, tb0_ref, nb0_ref), (cb1_ref, tb1_ref, nb1_ref), (cb2_ref, tb2_ref, nb2_ref))
    state = {}

    def scores():
        q_row = qa_ref[row, :]
        q_blk = jnp.zeros((8, LANES), F32)
        for r in range(A_Q_HEADS):
            k, g = divmod(r, A_GROUP)
            chunk = q_row[:, (r // 2) * LANES:(r // 2 + 1) * LANES]
            if g % 2 != k:
                chunk = pltpu.roll(chunk, HEAD_DIM, axis=1)
            q_blk = jnp.where((sub == r) & (half == k), jnp.broadcast_to(chunk, (8, LANES)), q_blk)
        kva = kva_ref[row, :]
        k_new = jnp.where(lane[:1] < HEAD_DIM, kva[:, 0:LANES], kva[:, LANES:2 * LANES])
        v_new = jnp.where(lane[:1] < HEAD_DIM, kva[:, 2 * LANES:3 * LANES], kva[:, 3 * LANES:])
        s = _dot(q_blk.astype(BF16), ca_ref[0:A_KVW, :].astype(BF16))
        state["a"] = (s, jnp.sum(q_blk * k_new, axis=-1, keepdims=True), v_new)
        for g, (c_ref, _, _) in enumerate(b_refs):
            qkv = b0_ref[row, :] if g == 0 else b12_ref[row, (g - 1) * 3 * B_GW:g * 3 * B_GW]
            q_row, k_new, v_new = qkv[:, 0:B_GW], qkv[:, B_GW:2 * B_GW], qkv[:, 2 * B_GW:]
            q_blk = jnp.where(head_mask, jnp.broadcast_to(q_row, (8, B_GW)), 0.0)
            s = _dot(q_blk.astype(BF16), c_ref[0:B_GW, :].astype(BF16))
            state["b", g] = (s, jnp.sum(q_blk * k_new, axis=-1, keepdims=True), v_new)

    def softmaxes():
        s, s_new, v_new = state["a"]
        slope = _rows_where(sub[:, :1], [slopes_ref[r] * LOG2E for r in range(A_Q_HEADS)])
        sink = _rows_where(sub[:, :1], [sinks_ref[r] * LOG2E for r in range(A_Q_HEADS)])
        s = s - slope * (BLOCK - lane).astype(F32)
        m = jnp.maximum(jnp.maximum(jnp.max(s, axis=-1, keepdims=True), s_new), sink)
        e = jnp.exp2(s - m)
        e_new = jnp.exp2(s_new - m)
        denom = jnp.sum(e, axis=-1, keepdims=True) + e_new + jnp.exp2(sink - m)
        state["a"] = (e.astype(BF16), e_new, v_new, m, denom)
        for g, (length, dil) in enumerate(B_PATTERNS):
            s, s_new, v_new = state["b", g]
            head0 = A_Q_HEADS + g * B_HEADS_PER_GROUP
            sub_l = lax.broadcasted_iota(jnp.int32, (8, length), 0)
            col = lax.broadcasted_iota(jnp.int32, (8, length), 1)
            slope = _rows_where(sub_l, [slopes_ref[head0 + h] * LOG2E for h in range(B_HEADS_PER_GROUP)])
            s = jnp.where((col & (dil - 1)) == 0, s - slope * (length - col).astype(F32), NEG)
            m = jnp.maximum(jnp.max(s, axis=-1, keepdims=True), s_new)
            e = jnp.exp2(s - m)
            e_new = jnp.exp2(s_new - m)
            denom = jnp.sum(e, axis=-1, keepdims=True) + e_new
            state["b", g] = (e.astype(BF16), e_new, v_new, m, denom)

    def values():
        e, e_new, v_new, m, denom = state["a"]
        acc = _dot_nt(e, ca_ref[A_KVW:2 * A_KVW, :].astype(BF16))
        out = jnp.where(half == kv_head, (acc + e_new * v_new) / denom, 0.0)
        for c in range(A_W // LANES):
            k = c // 2
            pieces = []
            for hh in range(2):
                r = k * A_GROUP + 2 * (c % 2) + hh
                piece = out[r:r + 1, :]
                pieces.append(piece if hh == k else pltpu.roll(piece, HEAD_DIM, axis=1))
            oa_ref[:, c * LANES:(c + 1) * LANES] = jnp.where(lane[:1] < HEAD_DIM, pieces[0], pieces[1])
        for g, (c_ref, _, _) in enumerate(b_refs):
            e, e_new, v_new, m, denom = state["b", g]
            acc = _dot_nt(e, c_ref[B_GW:2 * B_GW, :].astype(BF16))
            out = jnp.where(head_mask, (acc + e_new * v_new) / denom, 0.0)
            ob_ref[g] = jnp.sum(out, axis=0, keepdims=True)
            lse = jnp.where(head_mask, m + jnp.log2(denom), 0.0)
            lse_ref[g] = jnp.sum(lse, axis=0, keepdims=True)

    def shift(c_ref, t_ref, n_ref, length, piece, n_pieces):
        rows_per = c_ref.shape[0] // n_pieces
        return lambda: _shift_cache(c_ref, t_ref, n_ref, n, length, slice(piece * rows_per, (piece + 1) * rows_per))

    def shift_small():
        shift(ca_ref, ta_ref, na_ref, BLOCK, 0, 1)()
        shift(cb0_ref, tb0_ref, nb0_ref, B_PATTERNS[0][0], 0, 1)()

    shift_b1 = shift(cb1_ref, tb1_ref, nb1_ref, B_PATTERNS[1][0], 0, 1)
    shift_b2 = [shift(cb2_ref, tb2_ref, nb2_ref, B_PATTERNS[2][0], piece, _N_SHIFT_PIECES)
                for piece in range(_N_SHIFT_PIECES)]
    return [shift_small, scores, shift_b1, softmaxes, shift_b2[0], values] + shift_b2[1:]


_N_SHIFT_PIECES = 4


_N_FFN_INPUTS = 5
_N_SAMPLE_INPUTS = 14
_N_SAMPLE_OUTPUTS = 7


def _ffn_sample_kernel(*refs):
    ffn_in, refs = refs[:_N_FFN_INPUTS], refs[_N_FFN_INPUTS:]
    sample_in, refs = refs[:_N_SAMPLE_INPUTS], refs[_N_SAMPLE_INPUTS:]
    o_ref, sample_out, (act_ref,) = refs[0], refs[1:1 + _N_SAMPLE_OUTPUTS], refs[1 + _N_SAMPLE_OUTPUTS:]
    x_ref, gain_ref, wg_ref, wu_ref, wd_ref = ffn_in
    o_ref[...] = _swiglu_residual(x_ref[...], gain_ref[...], wg_ref, wu_ref, wd_ref, act_ref,
                                  side_work=_sample_stages(*sample_in, *sample_out))


def _ffn_and_sample(x, gain, wg, wu, wd, slopes, sinks, qa, kva, b0, b12, tails, caches):
    n_seq = qa.shape[0]
    t = x.shape[0]
    assert t % n_seq == 0
    tm = t // n_seq
    full = lambda a: pl.BlockSpec(a.shape, lambda n: (0,) * a.ndim)
    per_seq = lambda a: pl.BlockSpec((None,) + a.shape[1:], lambda n: (n, 0, 0))
    small = [qa, kva, b0, b12] + list(tails)
    assert 2 + len(small) + len(caches) == _N_SAMPLE_INPUTS
    out_shape = [jax.ShapeDtypeStruct((t, D_MODEL), F32),
                 jax.ShapeDtypeStruct((n_seq, 1, A_W), F32),
                 jax.ShapeDtypeStruct((B_N_GROUPS, n_seq, 1, B_GW), F32),
                 jax.ShapeDtypeStruct((B_N_GROUPS, n_seq, 1, B_GW), F32)]
    out_shape += [jax.ShapeDtypeStruct(c.shape, F32) for c in caches]
    out_specs = [pl.BlockSpec((tm, D_MODEL), lambda n: (n, 0)),
                 pl.BlockSpec((None, 1, A_W), lambda n: (n, 0, 0)),
                 pl.BlockSpec((B_N_GROUPS, None, 1, B_GW), lambda n: (0, n, 0, 0)),
                 pl.BlockSpec((B_N_GROUPS, None, 1, B_GW), lambda n: (0, n, 0, 0))]
    out_specs += [per_seq(c) for c in caches]
    x1, oa, ob, lse, *new_caches = pl.pallas_call(
        _ffn_sample_kernel,
        grid=(n_seq,),
        in_specs=[pl.BlockSpec((tm, D_MODEL), lambda n: (n, 0)),
                  _const_spec((1, D_MODEL)), _const_spec((D_MODEL, D_FF)),
                  _const_spec((D_MODEL, D_FF)), _const_spec((D_FF, D_MODEL))]
                 + [_smem_spec(), _smem_spec()] + [full(a) for a in small] + [per_seq(c) for c in caches],
        out_specs=out_specs,
        out_shape=out_shape,
        scratch_shapes=[pltpu.VMEM((tm, D_FF), BF16)],
        compiler_params=_params(1),
        name="ffn1_sample",
    )(x, gain, wg, wu, wd, slopes, sinks, *small, *caches)
    return (x1, oa.reshape(n_seq, A_W), ob.reshape(B_N_GROUPS, n_seq, B_GW),
            lse.reshape(B_N_GROUPS, n_seq, B_GW), *new_caches)


_N_A_PAIRS = A_W // LANES
_N_B_PAIRS = B_GW // LANES


def _merge_kernel(*refs):
    refs = list(refs)
    take = lambda count: [refs.pop(0) for _ in range(count)]
    (x_ref,), oa_refs, da_refs = take(1), take(_N_A_PAIRS), take(_N_A_PAIRS)
    n_b = B_N_GROUPS * _N_B_PAIRS
    ob_refs, db_refs, mb_refs = take(n_b), take(n_b), take(n_b)
    gates_ref, wua_ref, wub_ref, wo_ref, gain_ref, wg_ref, wu_ref, wd_ref, y_ref, act_ref = refs

    unswap = lambda ref: pltpu.roll(ref[...], HEAD_DIM, axis=1)
    oa_pairs = [(o_ref[...].astype(F32) / unswap(d_ref)).astype(BF16) for o_ref, d_ref in zip(oa_refs, da_refs)]
    ob_pairs = []
    for p in range(_N_B_PAIRS):
        idx = [g * _N_B_PAIRS + p for g in range(B_N_GROUPS)]
        maxes = [mb_refs[i][...] for i in idx]
        top = functools.reduce(jnp.maximum, maxes)
        weights = [jnp.exp2(m - top) for m in maxes]
        num = sum(w * ob_refs[i][...] for w, i in zip(weights, idx))
        den = sum(w * unswap(db_refs[i]) for w, i in zip(weights, idx))
        ob_pairs.append((num / den).astype(BF16))
    ua = _dot(jnp.concatenate(oa_pairs, axis=1), wua_ref[...])
    ub = _dot(jnp.concatenate(ob_pairs, axis=1), wub_ref[...])
    gate_a = gates_ref[:, :D_MODEL].astype(F32)
    gate_b = gates_ref[:, D_MODEL:].astype(F32)
    mixed = (gate_a * ua + gate_b * ub).astype(BF16)
    x = x_ref[...] + _dot(mixed, wo_ref[...])
    y_ref[...] = _swiglu_residual(x, gain_ref[...], wg_ref, wu_ref, wd_ref, act_ref)


def _merge(x, mixer_a, mixer_b, gates, wua, wub, wo, gain, wg, wu, wd, tm):
    t = x.shape[0]
    row_spec = lambda w: pl.BlockSpec((tm, w), lambda i: (i, 0))
    pair_inputs = [a for part in tuple(mixer_a) + tuple(mixer_b) for a in part]
    n_pair_inputs = len(pair_inputs)
    assert n_pair_inputs == 2 * _N_A_PAIRS + 3 * B_N_GROUPS * _N_B_PAIRS
    return pl.pallas_call(
        _merge_kernel,
        grid=(t // tm,),
        in_specs=[row_spec(D_MODEL)] + [row_spec(LANES)] * n_pair_inputs + [row_spec(2 * D_MODEL),
                  _const_spec((A_W, D_MODEL)), _const_spec((B_GW, D_MODEL)), _const_spec((D_MODEL, D_MODEL)),
                  _const_spec((1, D_MODEL)), _const_spec((D_MODEL, D_FF)), _const_spec((D_MODEL, D_FF)),
                  _const_spec((D_FF, D_MODEL))],
        out_specs=row_spec(D_MODEL),
        out_shape=jax.ShapeDtypeStruct((t, D_MODEL), F32),
        scratch_shapes=[pltpu.VMEM((tm, D_FF), BF16)],
        compiler_params=_params(1),
        name="merge_ffn2",
    )(x, *pair_inputs, gates, wua, wub, wo, gain, wg, wu, wd)


def _cache_view(cache):
    _, n, length, two, h, d = cache.shape
    return jnp.transpose(cache, (0, 1, 3, 4, 5, 2)).reshape(n, two * h * d, length)


def _state_view(rows_by_len, heads):
    n, _, length = rows_by_len.shape
    return jnp.transpose(rows_by_len.reshape(1, n, 2, heads, HEAD_DIM, length), (0, 1, 5, 2, 3, 4))


def kernel(x_prompt, x_sample, cache_a_kv, cache_b1_kv, cache_b2_kv, cache_b3_kv, norm_ffn1, w1_gate, w1_up,
           w1_down, norm_mix, w_in, q_norm_a, k_norm_a, q_norm_b, k_norm_b, sinks_a, w_up_a, w_up_b, w_o,
           norm_ffn2, w2_gate, w2_up, w2_down):
    assert x_prompt.shape[-1] == D_MODEL and w_in.shape == (1, D_MODEL, IN_W)
    batch, seq, _ = x_prompt.shape
    dec = x_sample.shape[0]
    assert x_sample.shape[1] == 1 and seq % (BLOCK * B_PATTERNS[-1][1]) == 0
    assert cache_a_kv.shape[2] == BLOCK
    assert all(c.shape[2] == w for c, (w, _) in zip((cache_b1_kv, cache_b2_kv, cache_b3_kv), B_PATTERNS))

    bf = lambda w: w[0].astype(BF16)
    wg1, wu1, wd1, wg2, wu2, wd2 = map(bf, (w1_gate, w1_up, w1_down, w2_gate, w2_up, w2_down))
    w_in_b, wua, wub, wo = map(bf, (w_in, w_up_a, w_up_b, w_o))

    i = jnp.arange(1, N_ALIBI_HEADS + 1, dtype=F32)
    slopes = jnp.exp2(-8.0 * i / N_ALIBI_HEADS)
    sinks = sinks_a[0].reshape(A_Q_HEADS).astype(F32)
    ones64 = jnp.ones((HEAD_DIM,), F32)
    q_scale = ATTN_SCALE * LOG2E
    qkgain = jnp.concatenate([
        jnp.tile(q_norm_a[0] * q_scale, A_Q_HEADS), jnp.tile(k_norm_a[0], A_KV_HEADS),
        jnp.tile(ones64, A_KV_HEADS),
        jnp.tile(q_norm_b[0] * q_scale, B_N_GROUPS * B_HEADS_PER_GROUP),
        jnp.tile(k_norm_b[0], B_N_GROUPS * B_HEADS_PER_GROUP),
        jnp.tile(ones64, B_N_GROUPS * B_HEADS_PER_GROUP)]).reshape(1, QKV_W).astype(F32)
    head_of = jnp.arange(CHUNK) // HEAD_DIM
    ones_bd = (head_of[:, None] == head_of[None, :]).astype(BF16)

    def proj(x1, n_seq, s, tm):
        return _proj(x1, norm_mix, w_in_b, qkgain, ones_bd, n_seq, s, tm)

    def back(x1, mixer_a, mixer_b, gates, tm):
        return _merge(x1, mixer_a, mixer_b, gates, wua, wub, wo, norm_ffn2, wg2, wu2, wd2, tm)

    x1s = _ffn(x_sample.reshape(dec, D_MODEL), norm_ffn1, wg1, wu1, wd1, dec)
    qa_s, kva_s, b0_s, b12_s, gates_s, *tails_s = proj(x1s, 1, dec, dec)
    caches = [_cache_view(c) for c in (cache_a_kv, cache_b1_kv, cache_b2_kv, cache_b3_kv)]
    x1p, oa_s, ob_s, lse_s, na, nb0, nb1, nb2 = _ffn_and_sample(
        x_prompt.reshape(batch * seq, D_MODEL), norm_ffn1, wg1, wu1, wd1,
        slopes, sinks, qa_s.astype(F32), kva_s.astype(F32), b0_s.astype(F32), b12_s,
        [t[0] for t in tails_s], caches)

    qa, kva, b0, b12, gates, ta, tb0, tb1, tb2 = proj(x1p, batch, seq, 512)
    common = dict(n_seq=batch, seq=seq)
    oa, da, _ = _attn(slopes, sinks, qa, kva, kva, (0, 1, 2, 3), (0, 1), (2, 3), k_pair=(0, 0, 1, 1), dil=1,
                      head0=0, out_dtype=BF16, has_max=False, name="attn_a", **common)
    mixer_b = ([], [], [])
    for g, (_, dil) in enumerate(B_PATTERNS):
        src = b0 if g == 0 else b12
        first = 0 if g == 0 else (g - 1) * 3 * _N_B_PAIRS
        parts = _attn(slopes, None, src, src, src, (first, first + 1), (first + 2, first + 3),
                      (first + 4, first + 5), k_pair=(0, 1), dil=dil,
                      head0=A_Q_HEADS + g * B_HEADS_PER_GROUP, out_dtype=F32, has_max=True,
                      name=f"attn_b{g}", **common)
        for acc, part in zip(mixer_b, parts):
            acc += part
    y_prompt = back(x1p, (oa, da), mixer_b, gates, 512).reshape(batch, seq, D_MODEL)

    pairs = lambda a: [a[..., p * LANES:(p + 1) * LANES] for p in range(a.shape[-1] // LANES)]
    one = jnp.ones((dec, LANES), F32)
    ob_s = [pair for g in range(B_N_GROUPS) for pair in pairs(ob_s[g])]
    lse_s = [pair for g in range(B_N_GROUPS) for pair in pairs(lse_s[g])]
    y_sample = back(x1s, (pairs(oa_s.astype(BF16)), [one] * _N_A_PAIRS), (ob_s, [one] * len(ob_s), lse_s),
                    gates_s, dec).reshape(dec, 1, D_MODEL)

    return (y_prompt, y_sample,
            _state_view(ta, A_KV_HEADS), _state_view(tb0, B_HEADS_PER_GROUP),
            _state_view(tb1, B_HEADS_PER_GROUP), _state_view(tb2, B_HEADS_PER_GROUP),
            _state_view(na, A_KV_HEADS), _state_view(nb0, B_HEADS_PER_GROUP),
            _state_view(nb1, B_HEADS_PER_GROUP), _state_view(nb2, B_HEADS_PER_GROUP))
```

```python
import functools
import math

import jax
import jax.numpy as jnp
from jax import lax
from jax.experimental import pallas as pl
from jax.experimental.pallas import tpu as pltpu

D_MODEL = 1024
D_FF = 2816
HEAD_DIM = 64
A_Q_HEADS = 8
A_KV_HEADS = 2
A_GROUP = A_Q_HEADS // A_KV_HEADS
B_PATTERNS = ((128, 1), (512, 4), (2048, 16))
B_HEADS_PER_GROUP = 4
B_N_GROUPS = 3
N_ALIBI_HEADS = A_Q_HEADS + B_N_GROUPS * B_HEADS_PER_GROUP
BLOCK = 128
EPS = 1e-6
ATTN_SCALE = HEAD_DIM ** -0.5
LOG2E = math.log2(math.e)
LN2 = math.log(2.0)
A_W = A_Q_HEADS * HEAD_DIM
A_KVW = A_KV_HEADS * HEAD_DIM
B_GW = B_HEADS_PER_GROUP * HEAD_DIM
B_W = B_N_GROUPS * B_GW
QKV_W = A_W + 2 * A_KVW + 3 * B_W
IN_W = QKV_W + 2 * D_MODEL

LANES = 128
CHUNK = 256
NEG = -1e30
VMEM_LIMIT = 56 * 1024 * 1024

F32 = jnp.float32
BF16 = jnp.bfloat16


def _const_spec(shape):
    nd = len(shape)
    return pl.BlockSpec(shape, lambda *_: (0,) * nd, pipeline_mode=pl.Buffered(1))


def _smem_spec():
    return pl.BlockSpec(memory_space=pltpu.SMEM)


def _params(n_axes):
    return pltpu.CompilerParams(
        dimension_semantics=("arbitrary",) * n_axes, vmem_limit_bytes=VMEM_LIMIT)


def _rmsnorm(x, g):
    return x * lax.rsqrt(jnp.mean(x * x, axis=-1, keepdims=True) + EPS) * g


def _dot(a, b):
    return jnp.dot(a, b, preferred_element_type=F32)


def _dot_nt(a, b):
    return lax.dot_general(a, b, (((1,), (1,)), ((), ())), preferred_element_type=F32)


def _swiglu_residual(x, gain, wg_ref, wu_ref, wd_ref, act_ref, side_work=()):
    n_chunks = D_FF // CHUNK
    assert len(side_work) <= n_chunks
    h = _rmsnorm(x, gain).astype(BF16)
    for c in range(n_chunks):
        if c < len(side_work):
            side_work[c]()
        sl = slice(c * CHUNK, (c + 1) * CHUNK)
        g = _dot(h, wg_ref[:, sl])
        u = _dot(h, wu_ref[:, sl])
        act_ref[:, sl] = (g * jax.nn.sigmoid(g) * u).astype(BF16)
    return x + 0.5 * _dot(act_ref[...], wd_ref[...])


def _ffn_kernel(x_ref, gain_ref, wg_ref, wu_ref, wd_ref, o_ref, act_ref):
    o_ref[...] = _swiglu_residual(x_ref[...], gain_ref[...], wg_ref, wu_ref, wd_ref, act_ref)


def _ffn(x, gain, wg, wu, wd, tm):
    t = x.shape[0]
    return pl.pallas_call(
        _ffn_kernel,
        grid=(t // tm,),
        in_specs=[pl.BlockSpec((tm, D_MODEL), lambda i: (i, 0)),
                  _const_spec((1, D_MODEL)), _const_spec((D_MODEL, D_FF)),
                  _const_spec((D_MODEL, D_FF)), _const_spec((D_FF, D_MODEL))],
        out_specs=pl.BlockSpec((tm, D_MODEL), lambda i: (i, 0)),
        out_shape=jax.ShapeDtypeStruct((t, D_MODEL), F32),
        scratch_shapes=[pltpu.VMEM((tm, D_FF), BF16)],
        compiler_params=_params(1),
        name="ffn1",
    )(x, gain, wg, wu, wd)


_N_QKV_CHUNKS = QKV_W // CHUNK
_N_CHUNKS = IN_W // CHUNK


def _tail_plan(tail, tm, n_tiles):
    if tail >= tm:
        return tm, n_tiles - tail // tm
    return tail, n_tiles - 1


_N_PROJ_INPUTS = 5
_N_PROJ_OUTPUTS = 9
_BF16_ROWS = 16


def _proj_kernel(*refs, tm, n_tiles, tails):
    n_riders = (len(refs) - _N_PROJ_INPUTS - _N_PROJ_OUTPUTS) // 2
    x_ref, gain_ref, w_ref, qkgain_ref, ones_ref = refs[:_N_PROJ_INPUTS]
    rider_in = refs[_N_PROJ_INPUTS:_N_PROJ_INPUTS + n_riders]
    outs = refs[_N_PROJ_INPUTS + n_riders:]
    qa_ref, kva_ref, b0_ref, b12_ref, gates_ref, ta_ref, tb0_ref, tb1_ref, tb2_ref = outs[:_N_PROJ_OUTPUTS]
    for src, dst in zip(rider_in, outs[_N_PROJ_OUTPUTS:]):
        dst[...] = src[...].astype(BF16)
    h = _rmsnorm(x_ref[...], gain_ref[...]).astype(BF16)
    lane = lax.broadcasted_iota(jnp.int32, (tm, CHUNK), 1)
    lane1 = lax.broadcasted_iota(jnp.int32, (tm, LANES), 1)

    def z_chunk(c):
        return _dot(h, w_ref[:, c * CHUNK:(c + 1) * CHUNK])

    def head_norm(z, c):
        ss = _dot((z * z).astype(BF16), ones_ref[...])
        return z * lax.rsqrt(ss * (1.0 / HEAD_DIM) + EPS) * qkgain_ref[:, c * CHUNK:(c + 1) * CHUNK]

    def write_tail(t_ref, row0, y, tail):
        block_w, _ = _tail_plan(tail, tm, n_tiles)
        data = y if block_w == tm else y[tm - block_w:, :]
        t_ref[row0:row0 + CHUNK, :] = data.T

    tb_refs = (tb0_ref, tb1_ref, tb2_ref)

    def consume(c, z):
        if c < 2:
            qa_ref[:, c * CHUNK:(c + 1) * CHUNK] = head_norm(z, c).astype(BF16)
        elif c == 2:
            y = jnp.where(lane < A_KVW, head_norm(z, c), z)
            write_tail(ta_ref, 0, y, tails[0])
            for part in range(2):
                pair = y[:, part * LANES:(part + 1) * LANES]
                swapped = pltpu.roll(pair, HEAD_DIM, axis=1)
                base = part * 2 * LANES
                kva_ref[:, base:base + LANES] = jnp.where(lane1 < HEAD_DIM, pair, swapped).astype(BF16)
                kva_ref[:, base + LANES:base + 2 * LANES] = jnp.where(lane1 < HEAD_DIM, swapped, pair).astype(BF16)
        elif c < _N_QKV_CHUNKS:
            kind, g = divmod(c - 3, B_N_GROUPS)
            y = z if kind == 2 else head_norm(z, c)
            if g == 0:
                b0_ref[:, kind * CHUNK:(kind + 1) * CHUNK] = y.astype(BF16)
            else:
                col = ((g - 1) * 3 + kind) * CHUNK
                b12_ref[:, col:col + CHUNK] = y
            if kind > 0:
                write_tail(tb_refs[g], (kind - 1) * CHUNK, y, tails[1 + g])
        else:
            col = (c - _N_QKV_CHUNKS) * CHUNK
            gates_ref[:, col:col + CHUNK] = jax.nn.sigmoid(z).astype(BF16)

    z_next = z_chunk(0)
    for c in range(_N_CHUNKS):
        z = z_next
        if c + 1 < _N_CHUNKS:
            z_next = z_chunk(c + 1)
        consume(c, z)


def _rider_specs(weights, n_steps, n_tiles):
    specs = []
    for w in weights:
        rows, cols = w.shape
        n_blocks = n_steps
        while rows % (n_blocks * _BF16_ROWS):
            assert n_blocks % 2 == 0, (rows, n_steps)
            n_blocks //= 2
        per = n_steps // n_blocks
        specs.append(pl.BlockSpec((rows // n_blocks, cols), lambda n, j, per=per: ((n * n_tiles + j) // per, 0)))
    return specs


def _proj(x, gain, w_in, qkgain, ones_bd, n_seq, seq, tm, riders=()):
    t = n_seq * seq
    n_tiles = seq // tm
    rider_specs = _rider_specs(riders, n_seq * n_tiles, n_tiles)
    tails = (min(128, seq),) + tuple(min(w, seq) for w, _ in B_PATTERNS)
    tail_rows = (2 * A_KVW, 2 * B_GW, 2 * B_GW, 2 * B_GW)

    def tail_spec(rows, tail):
        block_w, first = _tail_plan(tail, tm, n_tiles)
        return pl.BlockSpec((None, rows, block_w), lambda n, j: (n, 0, jnp.maximum(j - first, 0)))

    row_spec = lambda w: pl.BlockSpec((tm, w), lambda n, j: (n * n_tiles + j, 0))
    out_shape = [
        jax.ShapeDtypeStruct((t, A_W), BF16),
        jax.ShapeDtypeStruct((t, 4 * LANES), BF16),
        jax.ShapeDtypeStruct((t, 3 * B_GW), BF16),
        jax.ShapeDtypeStruct((t, 6 * B_GW), F32),
        jax.ShapeDtypeStruct((t, 2 * D_MODEL), BF16),
    ] + [jax.ShapeDtypeStruct((n_seq, r, tl), F32) for r, tl in zip(tail_rows, tails)]
    out_specs = [row_spec(A_W), row_spec(4 * LANES), row_spec(3 * B_GW), row_spec(6 * B_GW),
                 row_spec(2 * D_MODEL)] + [tail_spec(r, tl) for r, tl in zip(tail_rows, tails)]
    assert len(out_shape) == _N_PROJ_OUTPUTS
    out_shape += [jax.ShapeDtypeStruct(w.shape, BF16) for w in riders]
    outs = pl.pallas_call(
        functools.partial(_proj_kernel, tm=tm, n_tiles=n_tiles, tails=tails),
        grid=(n_seq, n_tiles),
        in_specs=[row_spec(D_MODEL), _const_spec((1, D_MODEL)), _const_spec((D_MODEL, IN_W)),
                  _const_spec((1, QKV_W)), _const_spec((CHUNK, CHUNK))] + rider_specs,
        out_specs=out_specs + _rider_specs(riders, n_seq * n_tiles, n_tiles),
        out_shape=out_shape,
        compiler_params=_params(2),
        name="proj",
    )(x, gain, w_in, qkgain, ones_bd, *riders)
    return outs[:_N_PROJ_OUTPUTS], outs[_N_PROJ_OUTPUTS:]


_HEADS_PER_ITER = 32


def _attn_kernel(*refs, n_pairs, k_pair, dil, seq, head0, has_sink, has_max):
    refs = list(refs)
    slopes_ref = refs.pop(0)
    sinks_ref = refs.pop(0) if has_sink else None
    n_kv = max(k_pair) + 1
    take = lambda count: [refs.pop(0) for _ in range(count)]
    q_refs, k_refs, v_refs, o_refs, den_refs = take(n_pairs), take(n_kv), take(n_kv), take(n_pairs), take(n_pairs)
    max_refs = take(n_pairs) if has_max else None
    (bias_ref,) = refs

    n_blocks = seq // (BLOCK * dil)
    use_prev = n_blocks > 1
    n_heads = 2 * n_pairs
    qi = lax.broadcasted_iota(jnp.int32, (BLOCK, BLOCK), 0)
    ki = lax.broadcasted_iota(jnp.int32, (BLOCK, BLOCK), 1)
    d_cur = qi - ki
    lo = ki < HEAD_DIM
    lo_keys = lax.broadcasted_iota(jnp.int32, ((2 if use_prev else 1) * BLOCK, LANES), 1) < HEAD_DIM

    for h in range(n_heads):
        slope = slopes_ref[head0 + h] * (float(dil) * LOG2E)
        cur_bias = jnp.where(d_cur >= 0, -slope * d_cur.astype(F32), NEG)
        if use_prev:
            prev_bias = jnp.where(d_cur <= 0, -slope * (d_cur + BLOCK).astype(F32), NEG)
            bias_ref[h] = jnp.concatenate([prev_bias, cur_bias], axis=1)
            bias_ref[n_heads + h] = jnp.concatenate([jnp.full_like(prev_bias, NEG), cur_bias], axis=1)
        else:
            bias_ref[h] = cur_bias

    def block_rows(step):
        if n_blocks == 1:
            r, j = step, 0
        else:
            r, j = step // n_blocks, step % n_blocks
        if dil == 1:
            cur = pl.ds(pl.multiple_of(j * BLOCK, BLOCK), BLOCK)
            prev = pl.ds(pl.multiple_of(jnp.maximum(j - 1, 0) * BLOCK, BLOCK), BLOCK)
        else:
            cur = pl.ds(r + j * (BLOCK * dil), BLOCK, stride=dil)
            prev = pl.ds(r + jnp.maximum(j - 1, 0) * (BLOCK * dil), BLOCK, stride=dil)
        table = jnp.where(j == 0, n_heads, 0) if use_prev else 0
        return cur, prev, table

    blocks_per_iter = _HEADS_PER_ITER // (2 * n_pairs)
    assert (dil * n_blocks) % blocks_per_iter == 0

    def body(it, carry):
        blocks = [block_rows(it * blocks_per_iter + b) for b in range(blocks_per_iter)]
        scores, values = {}, {}
        for b, (cur, prev, _) in enumerate(blocks):
            for p in range(n_pairs):
                k_ref, v_ref = k_refs[k_pair[p]], v_refs[k_pair[p]]
                q = q_refs[p][cur, :].astype(BF16)
                keys = k_ref[cur, :].astype(BF16)
                vals = v_ref[cur, :].astype(BF16)
                if use_prev:
                    keys = jnp.concatenate([k_ref[prev, :].astype(BF16), keys], axis=0)
                    vals = jnp.concatenate([v_ref[prev, :].astype(BF16), vals], axis=0)
                one = jnp.ones_like(vals)
                values[b, p, 0] = jnp.where(lo_keys, vals, one)
                values[b, p, 1] = jnp.where(lo_keys, one, vals)
                for half in range(2):
                    qm = jnp.where(lo if half == 0 else jnp.logical_not(lo), q, jnp.zeros_like(q))
                    scores[b, p, half] = _dot_nt(qm, keys)
        probs, row_max, sink_terms = {}, {}, {}
        for (b, p, half), s in scores.items():
            head = 2 * p + half
            s = s + bias_ref[blocks[b][2] + head]
            m = jnp.max(s, axis=-1, keepdims=True)
            if has_sink:
                sink = sinks_ref[head] * LOG2E
                m = jnp.maximum(m, sink)
                sink_terms[b, p, half] = jnp.exp2(sink - m)
            probs[b, p, half], row_max[b, p, half] = jnp.exp2(s - m).astype(BF16), m
        for b, (cur, _, _) in enumerate(blocks):
            for p in range(n_pairs):
                acc = [_dot(probs[b, p, half], values[b, p, half]) for half in range(2)]
                o_refs[p][cur, :] = jnp.where(lo, acc[0], acc[1]).astype(o_refs[p].dtype)
                denom = jnp.where(lo, acc[1], acc[0])
                if has_sink:
                    denom = denom + jnp.where(lo, sink_terms[b, p, 1], sink_terms[b, p, 0])
                den_refs[p][cur, :] = denom
                if has_max:
                    max_refs[p][cur, :] = jnp.where(lo, row_max[b, p, 0], row_max[b, p, 1])
        return carry

    lax.fori_loop(0, dil * n_blocks // blocks_per_iter, body, 0)


def _attn(slopes, sinks, q_arr, k_arr, v_arr, q_idx, k_idx, v_idx, *, n_seq, seq, k_pair, dil, head0,
          out_dtype, has_max, name):
    has_sink = sinks is not None
    n_pairs = len(q_idx)

    def pair_spec(idx):
        return pl.BlockSpec((None, seq, LANES), lambda n: (n, 0, idx))

    view = lambda a: a.reshape(n_seq, seq, a.shape[-1])
    operands = [slopes] + ([sinks] if has_sink else [])
    operands += [view(q_arr)] * n_pairs + [view(k_arr)] * len(k_idx) + [view(v_arr)] * len(v_idx)
    in_specs = [_smem_spec()] * (2 if has_sink else 1)
    in_specs += [pair_spec(i) for i in tuple(q_idx) + tuple(k_idx) + tuple(v_idx)]
    out_shape = [jax.ShapeDtypeStruct((n_seq, seq, LANES), out_dtype)] * n_pairs
    out_shape += [jax.ShapeDtypeStruct((n_seq, seq, LANES), F32)] * (n_pairs * (2 if has_max else 1))
    outs = pl.pallas_call(
        functools.partial(_attn_kernel, n_pairs=n_pairs, k_pair=k_pair, dil=dil, seq=seq, head0=head0,
                          has_sink=has_sink, has_max=has_max),
        grid=(n_seq,),
        in_specs=in_specs,
        out_specs=[pair_spec(0)] * len(out_shape),
        out_shape=out_shape,
        scratch_shapes=[pltpu.VMEM((2 * n_pairs, BLOCK, BLOCK) if seq == BLOCK * dil else
                                   (4 * n_pairs, BLOCK, 2 * BLOCK), F32)],
        compiler_params=_params(1),
        name=name,
    )(*operands)
    outs = [o.reshape(n_seq * seq, LANES) for o in outs]
    return outs[:n_pairs], outs[n_pairs:2 * n_pairs], outs[2 * n_pairs:]


def _rows_where(row_iota, values):
    out = jnp.zeros(row_iota.shape, F32)
    for r, v in enumerate(values):
        out = jnp.where(row_iota == r, v, out)
    return out


def _shift_cache(cache_ref, tail_ref, out_ref, n, length, rows):
    shifted = pltpu.roll(cache_ref[rows, :], length - 1, axis=1)
    new_col = pltpu.roll(tail_ref[rows, :], (LANES - 1) - n, axis=1)
    lane = lax.broadcasted_iota(jnp.int32, new_col.shape, 1)
    last = jnp.where(lane == LANES - 1, new_col, shifted[:, length - LANES:])
    out_ref[rows, :] = last if length == LANES else jnp.concatenate([shifted[:, :length - LANES], last], axis=1)


def _sample_stages(slopes_ref, sinks_ref, qa_ref, kva_ref, b0_ref, b12_ref,
                   ta_ref, tb0_ref, tb1_ref, tb2_ref, ca_ref, cb0_ref, cb1_ref, cb2_ref,
                   oa_ref, ob_ref, lse_ref, na_ref, nb0_ref, nb1_ref, nb2_ref):
    n = pl.program_id(0)
    row = pl.ds(n, 1)
    sub = lax.broadcasted_iota(jnp.int32, (8, LANES), 0)
    lane = lax.broadcasted_iota(jnp.int32, (8, LANES), 1)
    half = lane // HEAD_DIM
    kv_head = sub // A_GROUP
    sub_b = lax.broadcasted_iota(jnp.int32, (8, B_GW), 0)
    head_mask = (lax.broadcasted_iota(jnp.int32, (8, B_GW), 1) // HEAD_DIM) == sub_b
    b_refs = ((cb0_ref, tb0_ref, nb0_ref), (cb1_ref, tb1_ref, nb1_ref), (cb2_ref, tb2_ref, nb2_ref))
    state = {}

    def scores():
        q_row = qa_ref[row, :]
        q_blk = jnp.zeros((8, LANES), F32)
        for r in range(A_Q_HEADS):
            k, g = divmod(r, A_GROUP)
            chunk = q_row[:, (r // 2) * LANES:(r // 2 + 1) * LANES]
            if g % 2 != k:
                chunk = pltpu.roll(chunk, HEAD_DIM, axis=1)
            q_blk = jnp.where((sub == r) & (half == k), jnp.broadcast_to(chunk, (8, LANES)), q_blk)
        kva = kva_ref[row, :]
        k_new = jnp.where(lane[:1] < HEAD_DIM, kva[:, 0:LANES], kva[:, LANES:2 * LANES])
        v_new = jnp.where(lane[:1] < HEAD_DIM, kva[:, 2 * LANES:3 * LANES], kva[:, 3 * LANES:])
        s = _dot(q_blk.astype(BF16), ca_ref[0:A_KVW, :].astype(BF16))
        state["a"] = (s, jnp.sum(q_blk * k_new, axis=-1, keepdims=True), v_new)
        for g, (c_ref, _, _) in enumerate(b_refs):
            qkv = b0_ref[row, :] if g == 0 else b12_ref[row, (g - 1) * 3 * B_GW:g * 3 * B_GW]
            q_row, k_new, v_new = qkv[:, 0:B_GW], qkv[:, B_GW:2 * B_GW], qkv[:, 2 * B_GW:]
            q_blk = jnp.where(head_mask, jnp.broadcast_to(q_row, (8, B_GW)), 0.0)
            if g in _VPU_GROUPS:
                q_col = jnp.broadcast_to(q_row, (8, B_GW)).T[:, 0:1]
                prod = (c_ref[0:B_GW, :] * q_col).reshape(B_HEADS_PER_GROUP, HEAD_DIM, c_ref.shape[1])
                s4 = jnp.sum(prod, axis=1)
                s = jnp.concatenate([s4, jnp.zeros_like(s4)], axis=0)
            else:
                s = _dot(q_blk.astype(BF16), c_ref[0:B_GW, :].astype(BF16))
            state["b", g] = (s, jnp.sum(q_blk * k_new, axis=-1, keepdims=True), v_new)

    def softmaxes():
        s, s_new, v_new = state["a"]
        slope = _rows_where(sub[:, :1], [slopes_ref[r] * LOG2E for r in range(A_Q_HEADS)])
        sink = _rows_where(sub[:, :1], [sinks_ref[r] * LOG2E for r in range(A_Q_HEADS)])
        s = s - slope * (BLOCK - lane).astype(F32)
        m = jnp.maximum(jnp.maximum(jnp.max(s, axis=-1, keepdims=True), s_new), sink)
        e = jnp.exp2(s - m)
        e_new = jnp.exp2(s_new - m)
        denom = jnp.sum(e, axis=-1, keepdims=True) + e_new + jnp.exp2(sink - m)
        state["a"] = (e.astype(BF16), e_new, v_new, m, denom)
        for g, (length, dil) in enumerate(B_PATTERNS):
            s, s_new, v_new = state["b", g]
            head0 = A_Q_HEADS + g * B_HEADS_PER_GROUP
            sub_l = lax.broadcasted_iota(jnp.int32, (8, length), 0)
            col = lax.broadcasted_iota(jnp.int32, (8, length), 1)
            slope = _rows_where(sub_l, [slopes_ref[head0 + h] * LOG2E for h in range(B_HEADS_PER_GROUP)])
            s = jnp.where((col & (dil - 1)) == 0, s - slope * (length - col).astype(F32), NEG)
            m = jnp.maximum(jnp.max(s, axis=-1, keepdims=True), s_new)
            e = jnp.exp2(s - m)
            e_new = jnp.exp2(s_new - m)
            denom = jnp.sum(e, axis=-1, keepdims=True) + e_new
            state["b", g] = (e if g in _VPU_GROUPS else e.astype(BF16), e_new, v_new, m, denom)

    def values():
        e, e_new, v_new, m, denom = state["a"]
        acc = _dot_nt(e, ca_ref[A_KVW:2 * A_KVW, :].astype(BF16))
        out = jnp.where(half == kv_head, (acc + e_new * v_new) / denom, 0.0)
        for c in range(A_W // LANES):
            k = c // 2
            pieces = []
            for hh in range(2):
                r = k * A_GROUP + 2 * (c % 2) + hh
                piece = out[r:r + 1, :]
                pieces.append(piece if hh == k else pltpu.roll(piece, HEAD_DIM, axis=1))
            oa_ref[:, c * LANES:(c + 1) * LANES] = jnp.where(lane[:1] < HEAD_DIM, pieces[0], pieces[1])
        for g, (c_ref, _, _) in enumerate(b_refs):
            e, e_new, v_new, m, denom = state["b", g]
            if g in _VPU_GROUPS:
                length = c_ref.shape[1]
                weights = jnp.broadcast_to(e[0:B_HEADS_PER_GROUP, None, :], (B_HEADS_PER_GROUP, HEAD_DIM, length))
                acc_col = jnp.sum(c_ref[B_GW:2 * B_GW, :] * weights.reshape(B_GW, length), axis=1, keepdims=True)
                acc = jnp.broadcast_to(jnp.broadcast_to(acc_col, (B_GW, 8)).T[0:1, :], (8, B_GW))
            else:
                acc = _dot_nt(e, c_ref[B_GW:2 * B_GW, :].astype(BF16))
            out = jnp.where(head_mask, (acc + e_new * v_new) / denom, 0.0)
            ob_ref[g] = jnp.sum(out, axis=0, keepdims=True)
            lse = jnp.where(head_mask, m + jnp.log2(denom), 0.0)
            lse_ref[g] = jnp.sum(lse, axis=0, keepdims=True)

    def shift(c_ref, t_ref, n_ref, length, piece, n_pieces):
        rows_per = c_ref.shape[0] // n_pieces
        return lambda: _shift_cache(c_ref, t_ref, n_ref, n, length, slice(piece * rows_per, (piece + 1) * rows_per))

    def shift_small():
        shift(ca_ref, ta_ref, na_ref, BLOCK, 0, 1)()
        shift(cb0_ref, tb0_ref, nb0_ref, B_PATTERNS[0][0], 0, 1)()

    shift_b1 = shift(cb1_ref, tb1_ref, nb1_ref, B_PATTERNS[1][0], 0, 1)
    shift_b2 = [shift(cb2_ref, tb2_ref, nb2_ref, B_PATTERNS[2][0], piece, _N_SHIFT_PIECES)
                for piece in range(_N_SHIFT_PIECES)]
    return [scores, shift_small, softmaxes, shift_b1, values] + shift_b2


_VPU_GROUPS = (0, 1, 2)
_N_SHIFT_PIECES = 4


_N_FFN_INPUTS = 5
_N_SAMPLE_INPUTS = 14
_N_SAMPLE_OUTPUTS = 7


def _ffn_sample_kernel(*refs):
    ffn_in, refs = refs[:_N_FFN_INPUTS], refs[_N_FFN_INPUTS:]
    sample_in, refs = refs[:_N_SAMPLE_INPUTS], refs[_N_SAMPLE_INPUTS:]
    o_ref, sample_out, (act_ref,) = refs[0], refs[1:1 + _N_SAMPLE_OUTPUTS], refs[1 + _N_SAMPLE_OUTPUTS:]
    x_ref, gain_ref, wg_ref, wu_ref, wd_ref = ffn_in
    o_ref[...] = _swiglu_residual(x_ref[...], gain_ref[...], wg_ref, wu_ref, wd_ref, act_ref,
                                  side_work=_sample_stages(*sample_in, *sample_out))


def _ffn_and_sample(x, gain, wg, wu, wd, slopes, sinks, qa, kva, b0, b12, tails, caches):
    n_seq = qa.shape[0]
    t = x.shape[0]
    assert t % n_seq == 0
    tm = t // n_seq
    full = lambda a: pl.BlockSpec(a.shape, lambda n: (0,) * a.ndim)
    per_seq = lambda a: pl.BlockSpec((None,) + a.shape[1:], lambda n: (n, 0, 0))
    small = [qa, kva, b0, b12] + list(tails)
    assert 2 + len(small) + len(caches) == _N_SAMPLE_INPUTS
    out_shape = [jax.ShapeDtypeStruct((t, D_MODEL), F32),
                 jax.ShapeDtypeStruct((n_seq, 1, A_W), F32),
                 jax.ShapeDtypeStruct((B_N_GROUPS, n_seq, 1, B_GW), F32),
                 jax.ShapeDtypeStruct((B_N_GROUPS, n_seq, 1, B_GW), F32)]
    out_shape += [jax.ShapeDtypeStruct(c.shape, F32) for c in caches]
    out_specs = [pl.BlockSpec((tm, D_MODEL), lambda n: (n, 0)),
                 pl.BlockSpec((None, 1, A_W), lambda n: (n, 0, 0)),
                 pl.BlockSpec((B_N_GROUPS, None, 1, B_GW), lambda n: (0, n, 0, 0)),
                 pl.BlockSpec((B_N_GROUPS, None, 1, B_GW), lambda n: (0, n, 0, 0))]
    out_specs += [per_seq(c) for c in caches]
    x1, oa, ob, lse, *new_caches = pl.pallas_call(
        _ffn_sample_kernel,
        grid=(n_seq,),
        in_specs=[pl.BlockSpec((tm, D_MODEL), lambda n: (n, 0)),
                  _const_spec((1, D_MODEL)), _const_spec((D_MODEL, D_FF)),
                  _const_spec((D_MODEL, D_FF)), _const_spec((D_FF, D_MODEL))]
                 + [_smem_spec(), _smem_spec()] + [full(a) for a in small] + [per_seq(c) for c in caches],
        out_specs=out_specs,
        out_shape=out_shape,
        scratch_shapes=[pltpu.VMEM((tm, D_FF), BF16)],
        compiler_params=_params(1),
        name="ffn1_sample",
    )(x, gain, wg, wu, wd, slopes, sinks, *small, *caches)
    return (x1, oa.reshape(n_seq, A_W), ob.reshape(B_N_GROUPS, n_seq, B_GW),
            lse.reshape(B_N_GROUPS, n_seq, B_GW), *new_caches)


_N_A_PAIRS = A_W // LANES
_N_B_PAIRS = B_GW // LANES


def _merge_kernel(*refs):
    refs = list(refs)
    take = lambda count: [refs.pop(0) for _ in range(count)]
    (x_ref,), oa_refs, da_refs = take(1), take(_N_A_PAIRS), take(_N_A_PAIRS)
    n_b = B_N_GROUPS * _N_B_PAIRS
    ob_refs, db_refs, mb_refs = take(n_b), take(n_b), take(n_b)
    gates_ref, wua_ref, wub_ref, wo_ref, gain_ref, wg_ref, wu_ref, wd_ref, y_ref, act_ref = refs

    unswap = lambda ref: pltpu.roll(ref[...], HEAD_DIM, axis=1)
    oa_pairs = [(o_ref[...].astype(F32) / unswap(d_ref)).astype(BF16) for o_ref, d_ref in zip(oa_refs, da_refs)]
    ob_pairs = []
    for p in range(_N_B_PAIRS):
        idx = [g * _N_B_PAIRS + p for g in range(B_N_GROUPS)]
        maxes = [mb_refs[i][...] for i in idx]
        top = functools.reduce(jnp.maximum, maxes)
        weights = [jnp.exp2(m - top) for m in maxes]
        num = sum(w * ob_refs[i][...] for w, i in zip(weights, idx))
        den = sum(w * unswap(db_refs[i]) for w, i in zip(weights, idx))
        ob_pairs.append((num / den).astype(BF16))
    ua = _dot(jnp.concatenate(oa_pairs, axis=1), wua_ref[...])
    ub = _dot(jnp.concatenate(ob_pairs, axis=1), wub_ref[...])
    gate_a = gates_ref[:, :D_MODEL].astype(F32)
    gate_b = gates_ref[:, D_MODEL:].astype(F32)
    mixed = (gate_a * ua + gate_b * ub).astype(BF16)
    x = x_ref[...] + _dot(mixed, wo_ref[...])
    y_ref[...] = _swiglu_residual(x, gain_ref[...], wg_ref, wu_ref, wd_ref, act_ref)


def _merge(x, mixer_a, mixer_b, gates, wua, wub, wo, gain, wg, wu, wd, tm):
    t = x.shape[0]
    row_spec = lambda w: pl.BlockSpec((tm, w), lambda i: (i, 0))
    pair_inputs = [a for part in tuple(mixer_a) + tuple(mixer_b) for a in part]
    n_pair_inputs = len(pair_inputs)
    assert n_pair_inputs == 2 * _N_A_PAIRS + 3 * B_N_GROUPS * _N_B_PAIRS
    return pl.pallas_call(
        _merge_kernel,
        grid=(t // tm,),
        in_specs=[row_spec(D_MODEL)] + [row_spec(LANES)] * n_pair_inputs + [row_spec(2 * D_MODEL),
                  _const_spec((A_W, D_MODEL)), _const_spec((B_GW, D_MODEL)), _const_spec((D_MODEL, D_MODEL)),
                  _const_spec((1, D_MODEL)), _const_spec((D_MODEL, D_FF)), _const_spec((D_MODEL, D_FF)),
                  _const_spec((D_FF, D_MODEL))],
        out_specs=row_spec(D_MODEL),
        out_shape=jax.ShapeDtypeStruct((t, D_MODEL), F32),
        scratch_shapes=[pltpu.VMEM((tm, D_FF), BF16)],
        compiler_params=_params(1),
        name="merge_ffn2",
    )(x, *pair_inputs, gates, wua, wub, wo, gain, wg, wu, wd)


def _cache_view(cache):
    _, n, length, two, h, d = cache.shape
    return jnp.transpose(cache, (0, 1, 3, 4, 5, 2)).reshape(n, two * h * d, length)


def _state_view(rows_by_len, heads):
    n, _, length = rows_by_len.shape
    return jnp.transpose(rows_by_len.reshape(1, n, 2, heads, HEAD_DIM, length), (0, 1, 5, 2, 3, 4))


def kernel(x_prompt, x_sample, cache_a_kv, cache_b1_kv, cache_b2_kv, cache_b3_kv, norm_ffn1, w1_gate, w1_up,
           w1_down, norm_mix, w_in, q_norm_a, k_norm_a, q_norm_b, k_norm_b, sinks_a, w_up_a, w_up_b, w_o,
           norm_ffn2, w2_gate, w2_up, w2_down):
    assert x_prompt.shape[-1] == D_MODEL and w_in.shape == (1, D_MODEL, IN_W)
    batch, seq, _ = x_prompt.shape
    dec = x_sample.shape[0]
    assert x_sample.shape[1] == 1 and seq % (BLOCK * B_PATTERNS[-1][1]) == 0
    assert cache_a_kv.shape[2] == BLOCK
    assert all(c.shape[2] == w for c, (w, _) in zip((cache_b1_kv, cache_b2_kv, cache_b3_kv), B_PATTERNS))

    bf = lambda w: w[0].astype(BF16)
    wg1, wu1, wd1, w_in_b = map(bf, (w1_gate, w1_up, w1_down, w_in))
    late_weights = [w[0] for w in (w_up_a, w_up_b, w_o, w2_gate, w2_up, w2_down)]

    i = jnp.arange(1, N_ALIBI_HEADS + 1, dtype=F32)
    slopes = jnp.exp2(-8.0 * i / N_ALIBI_HEADS)
    sinks = sinks_a[0].reshape(A_Q_HEADS).astype(F32)
    ones64 = jnp.ones((HEAD_DIM,), F32)
    q_scale = ATTN_SCALE * LOG2E
    qkgain = jnp.concatenate([
        jnp.tile(q_norm_a[0] * q_scale, A_Q_HEADS), jnp.tile(k_norm_a[0], A_KV_HEADS),
        jnp.tile(ones64, A_KV_HEADS),
        jnp.tile(q_norm_b[0] * q_scale, B_N_GROUPS * B_HEADS_PER_GROUP),
        jnp.tile(k_norm_b[0], B_N_GROUPS * B_HEADS_PER_GROUP),
        jnp.tile(ones64, B_N_GROUPS * B_HEADS_PER_GROUP)]).reshape(1, QKV_W).astype(F32)
    head_of = jnp.arange(CHUNK) // HEAD_DIM
    ones_bd = (head_of[:, None] == head_of[None, :]).astype(BF16)

    def proj(x1, n_seq, s, tm, riders=()):
        return _proj(x1, norm_mix, w_in_b, qkgain, ones_bd, n_seq, s, tm, riders)

    x1s = _ffn(x_sample.reshape(dec, D_MODEL), norm_ffn1, wg1, wu1, wd1, dec)
    (qa_s, kva_s, b0_s, b12_s, gates_s, *tails_s), _ = proj(x1s, 1, dec, dec)
    caches = [_cache_view(c) for c in (cache_a_kv, cache_b1_kv, cache_b2_kv, cache_b3_kv)]
    x1p, oa_s, ob_s, lse_s, na, nb0, nb1, nb2 = _ffn_and_sample(
        x_prompt.reshape(batch * seq, D_MODEL), norm_ffn1, wg1, wu1, wd1,
        slopes, sinks, qa_s.astype(F32), kva_s.astype(F32), b0_s.astype(F32), b12_s,
        [t[0] for t in tails_s], caches)

    (qa, kva, b0, b12, gates, ta, tb0, tb1, tb2), (wua, wub, wo, wg2, wu2, wd2) = proj(
        x1p, batch, seq, 512, late_weights)

    def back(x1, mixer_a, mixer_b, gates, tm):
        return _merge(x1, mixer_a, mixer_b, gates, wua, wub, wo, norm_ffn2, wg2, wu2, wd2, tm)

    common = dict(n_seq=batch, seq=seq)
    oa, da, _ = _attn(slopes, sinks, qa, kva, kva, (0, 1, 2, 3), (0, 1), (2, 3), k_pair=(0, 0, 1, 1), dil=1,
                      head0=0, out_dtype=BF16, has_max=False, name="attn_a", **common)
    mixer_b = ([], [], [])
    for g, (_, dil) in enumerate(B_PATTERNS):
        src = b0 if g == 0 else b12
        first = 0 if g == 0 else (g - 1) * 3 * _N_B_PAIRS
        parts = _attn(slopes, None, src, src, src, (first, first + 1), (first + 2, first + 3),
                      (first + 4, first + 5), k_pair=(0, 1), dil=dil,
                      head0=A_Q_HEADS + g * B_HEADS_PER_GROUP, out_dtype=F32, has_max=True,
                      name=f"attn_b{g}", **common)
        for acc, part in zip(mixer_b, parts):
            acc += part
    y_prompt = back(x1p, (oa, da), mixer_b, gates, 512).reshape(batch, seq, D_MODEL)

    pairs = lambda a: [a[..., p * LANES:(p + 1) * LANES] for p in range(a.shape[-1] // LANES)]
    one = jnp.ones((dec, LANES), F32)
    ob_s = [pair for g in range(B_N_GROUPS) for pair in pairs(ob_s[g])]
    lse_s = [pair for g in range(B_N_GROUPS) for pair in pairs(lse_s[g])]
    y_sample = back(x1s, (pairs(oa_s.astype(BF16)), [one] * _N_A_PAIRS), (ob_s, [one] * len(ob_s), lse_s),
                    gates_s, dec).reshape(dec, 1, D_MODEL)

    return (y_prompt, y_sample,
            _state_view(ta, A_KV_HEADS), _state_view(tb0, B_HEADS_PER_GROUP),
            _state_view(tb1, B_HEADS_PER_GROUP), _state_view(tb2, B_HEADS_PER_GROUP),
            _state_view(na, A_KV_HEADS), _state_view(nb0, B_HEADS_PER_GROUP),
            _state_view(nb1, B_HEADS_PER_GROUP), _state_view(nb2, B_HEADS_PER_GROUP))
```

```python
import functools
import math

import jax
import jax.numpy as jnp
from jax import lax
from jax.experimental import pallas as pl
from jax.experimental.pallas import tpu as pltpu

D_MODEL = 1024
D_FF = 2816
HEAD_DIM = 64
A_Q_HEADS = 8
A_KV_HEADS = 2
A_GROUP = A_Q_HEADS // A_KV_HEADS
B_PATTERNS = ((128, 1), (512, 4), (2048, 16))
B_HEADS_PER_GROUP = 4
B_N_GROUPS = 3
N_ALIBI_HEADS = A_Q_HEADS + B_N_GROUPS * B_HEADS_PER_GROUP
BLOCK = 128
EPS = 1e-6
ATTN_SCALE = HEAD_DIM ** -0.5
LOG2E = math.log2(math.e)
A_W = A_Q_HEADS * HEAD_DIM
A_KVW = A_KV_HEADS * HEAD_DIM
B_GW = B_HEADS_PER_GROUP * HEAD_DIM
B_W = B_N_GROUPS * B_GW
QKV_W = A_W + 2 * A_KVW + 3 * B_W
IN_W = QKV_W + 2 * D_MODEL

LANES = 128
CHUNK = 256
NEG = -1e30
VMEM_LIMIT = 56 * 1024 * 1024

F32 = jnp.float32
BF16 = jnp.bfloat16


def _const_spec(shape):
    nd = len(shape)
    return pl.BlockSpec(shape, lambda *_: (0,) * nd, pipeline_mode=pl.Buffered(1))


def _smem_spec():
    return pl.BlockSpec(memory_space=pltpu.SMEM)


def _params(n_axes):
    return pltpu.CompilerParams(
        dimension_semantics=("arbitrary",) * n_axes, vmem_limit_bytes=VMEM_LIMIT)


def _rmsnorm(x, g):
    return x * lax.rsqrt(jnp.mean(x * x, axis=-1, keepdims=True) + EPS) * g


def _dot(a, b):
    return jnp.dot(a, b, preferred_element_type=F32)


def _dot_nt(a, b):
    return lax.dot_general(a, b, (((1,), (1,)), ((), ())), preferred_element_type=F32)


def _swiglu_residual(x, gain, wg_ref, wu_ref, wd_ref, act_ref, side_work=()):
    n_chunks = D_FF // CHUNK
    assert len(side_work) <= n_chunks
    h = _rmsnorm(x, gain).astype(BF16)
    for c in range(n_chunks):
        if c < len(side_work):
            side_work[c]()
        sl = slice(c * CHUNK, (c + 1) * CHUNK)
        g = _dot(h, wg_ref[:, sl])
        u = _dot(h, wu_ref[:, sl])
        act_ref[:, sl] = (g * jax.nn.sigmoid(g) * u).astype(BF16)
    return x + 0.5 * _dot(act_ref[...], wd_ref[...])


def _ffn_kernel(x_ref, gain_ref, wg_ref, wu_ref, wd_ref, o_ref, act_ref):
    o_ref[...] = _swiglu_residual(x_ref[...], gain_ref[...], wg_ref, wu_ref, wd_ref, act_ref)


def _ffn(x, gain, wg, wu, wd, tm):
    t = x.shape[0]
    return pl.pallas_call(
        _ffn_kernel,
        grid=(t // tm,),
        in_specs=[pl.BlockSpec((tm, D_MODEL), lambda i: (i, 0)),
                  _const_spec((1, D_MODEL)), _const_spec((D_MODEL, D_FF)),
                  _const_spec((D_MODEL, D_FF)), _const_spec((D_FF, D_MODEL))],
        out_specs=pl.BlockSpec((tm, D_MODEL), lambda i: (i, 0)),
        out_shape=jax.ShapeDtypeStruct((t, D_MODEL), F32),
        scratch_shapes=[pltpu.VMEM((tm, D_FF), BF16)],
        compiler_params=_params(1),
        name="ffn1",
    )(x, gain, wg, wu, wd)


_N_QKV_CHUNKS = QKV_W // CHUNK
_N_CHUNKS = IN_W // CHUNK
_LAST_CHUNK = 9


def _tail_plan(tail, tm, n_tiles):
    if tail >= tm:
        return tm, n_tiles - tail // tm
    return tail, n_tiles - 1


_N_PROJ_INPUTS = 5
_N_PROJ_OUTPUTS = 9
_BF16_ROWS = 16


def _proj_kernel(*refs, tm, n_tiles, tails):
    n_riders = (len(refs) - _N_PROJ_INPUTS - _N_PROJ_OUTPUTS) // 2
    x_ref, gain_ref, w_ref, qkgain_ref, ones_ref = refs[:_N_PROJ_INPUTS]
    rider_in = refs[_N_PROJ_INPUTS:_N_PROJ_INPUTS + n_riders]
    outs = refs[_N_PROJ_INPUTS + n_riders:]
    qa_ref, kva_ref, b0_ref, b12_ref, gates_ref, ta_ref, tb0_ref, tb1_ref, tb2_ref = outs[:_N_PROJ_OUTPUTS]
    for src, dst in zip(rider_in, outs[_N_PROJ_OUTPUTS:]):
        dst[...] = src[...].astype(BF16)
    h = _rmsnorm(x_ref[...], gain_ref[...]).astype(BF16)
    lane = lax.broadcasted_iota(jnp.int32, (tm, CHUNK), 1)
    lane1 = lax.broadcasted_iota(jnp.int32, (tm, LANES), 1)

    def z_chunk(c):
        return _dot(h, w_ref[:, c * CHUNK:(c + 1) * CHUNK])

    def head_norm(z, c):
        ss = _dot((z * z).astype(BF16), ones_ref[...])
        return z * lax.rsqrt(ss * (1.0 / HEAD_DIM) + EPS) * qkgain_ref[:, c * CHUNK:(c + 1) * CHUNK]

    def write_tail(t_ref, row0, y, tail):
        block_w, _ = _tail_plan(tail, tm, n_tiles)
        data = y if block_w == tm else y[tm - block_w:, :]
        t_ref[row0:row0 + CHUNK, :] = data.T

    tb_refs = (tb0_ref, tb1_ref, tb2_ref)

    def consume(c, z):
        if c < 2:
            qa_ref[:, c * CHUNK:(c + 1) * CHUNK] = head_norm(z, c).astype(BF16)
        elif c == 2:
            y = jnp.where(lane < A_KVW, head_norm(z, c), z)
            write_tail(ta_ref, 0, y, tails[0])
            for part in range(2):
                pair = y[:, part * LANES:(part + 1) * LANES]
                swapped = pltpu.roll(pair, HEAD_DIM, axis=1)
                base = part * 2 * LANES
                kva_ref[:, base:base + LANES] = jnp.where(lane1 < HEAD_DIM, pair, swapped).astype(BF16)
                kva_ref[:, base + LANES:base + 2 * LANES] = jnp.where(lane1 < HEAD_DIM, swapped, pair).astype(BF16)
        elif c < _N_QKV_CHUNKS:
            kind, g = divmod(c - 3, B_N_GROUPS)
            y = z if kind == 2 else head_norm(z, c)
            if g == 0:
                b0_ref[:, kind * CHUNK:(kind + 1) * CHUNK] = y.astype(BF16)
            else:
                col = ((g - 1) * 3 + kind) * CHUNK
                b12_ref[:, col:col + CHUNK] = y
            if kind > 0:
                write_tail(tb_refs[g], (kind - 1) * CHUNK, y, tails[1 + g])
        else:
            col = (c - _N_QKV_CHUNKS) * CHUNK
            gates_ref[:, col:col + CHUNK] = jax.nn.sigmoid(z).astype(BF16)

    order = list(range(_N_QKV_CHUNKS, _N_CHUNKS)) + [c for c in range(_N_QKV_CHUNKS) if c != _LAST_CHUNK]
    order.append(_LAST_CHUNK)
    z_next = z_chunk(order[0])
    for i, c in enumerate(order):
        z = z_next
        if i + 1 < len(order):
            z_next = z_chunk(order[i + 1])
        consume(c, z)


def _rider_specs(weights, n_steps, n_tiles):
    specs = []
    for w in weights:
        rows, cols = w.shape
        n_blocks = n_steps
        while rows % (n_blocks * _BF16_ROWS):
            assert n_blocks % 2 == 0, (rows, n_steps)
            n_blocks //= 2
        per = n_steps // n_blocks
        specs.append(pl.BlockSpec((rows // n_blocks, cols), lambda n, j, per=per: ((n * n_tiles + j) // per, 0)))
    return specs


def _proj(x, gain, w_in, qkgain, ones_bd, n_seq, seq, tm, riders=()):
    t = n_seq * seq
    n_tiles = seq // tm
    rider_specs = _rider_specs(riders, n_seq * n_tiles, n_tiles)
    tails = (min(128, seq),) + tuple(min(w, seq) for w, _ in B_PATTERNS)
    tail_rows = (2 * A_KVW, 2 * B_GW, 2 * B_GW, 2 * B_GW)

    def tail_spec(rows, tail):
        block_w, first = _tail_plan(tail, tm, n_tiles)
        return pl.BlockSpec((None, rows, block_w), lambda n, j: (n, 0, jnp.maximum(j - first, 0)))

    row_spec = lambda w: pl.BlockSpec((tm, w), lambda n, j: (n * n_tiles + j, 0))
    out_shape = [
        jax.ShapeDtypeStruct((t, A_W), BF16),
        jax.ShapeDtypeStruct((t, 4 * LANES), BF16),
        jax.ShapeDtypeStruct((t, 3 * B_GW), BF16),
        jax.ShapeDtypeStruct((t, 6 * B_GW), F32),
        jax.ShapeDtypeStruct((t, 2 * D_MODEL), BF16),
    ] + [jax.ShapeDtypeStruct((n_seq, r, tl), F32) for r, tl in zip(tail_rows, tails)]
    out_specs = [row_spec(A_W), row_spec(4 * LANES), row_spec(3 * B_GW), row_spec(6 * B_GW),
                 row_spec(2 * D_MODEL)] + [tail_spec(r, tl) for r, tl in zip(tail_rows, tails)]
    assert len(out_shape) == _N_PROJ_OUTPUTS
    out_shape += [jax.ShapeDtypeStruct(w.shape, BF16) for w in riders]
    outs = pl.pallas_call(
        functools.partial(_proj_kernel, tm=tm, n_tiles=n_tiles, tails=tails),
        grid=(n_seq, n_tiles),
        in_specs=[row_spec(D_MODEL), _const_spec((1, D_MODEL)), _const_spec((D_MODEL, IN_W)),
                  _const_spec((1, QKV_W)), _const_spec((CHUNK, CHUNK))] + rider_specs,
        out_specs=out_specs + _rider_specs(riders, n_seq * n_tiles, n_tiles),
        out_shape=out_shape,
        compiler_params=_params(2),
        name="proj",
    )(x, gain, w_in, qkgain, ones_bd, *riders)
    return outs[:_N_PROJ_OUTPUTS], outs[_N_PROJ_OUTPUTS:]


_HEADS_PER_ITER = 32


def _attn_kernel(*refs, n_pairs, k_pair, dil, seq, head0):
    refs = list(refs)
    slopes_ref = refs.pop(0)
    n_kv = max(k_pair) + 1
    take = lambda count: [refs.pop(0) for _ in range(count)]
    q_refs, k_refs, v_refs = take(n_pairs), take(n_kv), take(n_kv)
    o_refs, den_refs, max_refs = take(n_pairs), take(n_pairs), take(n_pairs)
    (bias_ref,) = refs

    n_blocks = seq // (BLOCK * dil)
    use_prev = n_blocks > 1
    n_heads = 2 * n_pairs
    qi = lax.broadcasted_iota(jnp.int32, (BLOCK, BLOCK), 0)
    ki = lax.broadcasted_iota(jnp.int32, (BLOCK, BLOCK), 1)
    d_cur = qi - ki
    lo = ki < HEAD_DIM
    lo_keys = lax.broadcasted_iota(jnp.int32, ((2 if use_prev else 1) * BLOCK, LANES), 1) < HEAD_DIM

    for h in range(n_heads):
        slope = slopes_ref[head0 + h] * (float(dil) * LOG2E)
        cur_bias = jnp.where(d_cur >= 0, -slope * d_cur.astype(F32), NEG)
        if use_prev:
            prev_bias = jnp.where(d_cur <= 0, -slope * (d_cur + BLOCK).astype(F32), NEG)
            bias_ref[h] = jnp.concatenate([prev_bias, cur_bias], axis=1)
            bias_ref[n_heads + h] = jnp.concatenate([jnp.full_like(prev_bias, NEG), cur_bias], axis=1)
        else:
            bias_ref[h] = cur_bias

    def block_rows(step):
        if n_blocks == 1:
            r, j = step, 0
        else:
            r, j = step // n_blocks, step % n_blocks
        if dil == 1:
            cur = pl.ds(pl.multiple_of(j * BLOCK, BLOCK), BLOCK)
            prev = pl.ds(pl.multiple_of(jnp.maximum(j - 1, 0) * BLOCK, BLOCK), BLOCK)
        else:
            cur = pl.ds(r + j * (BLOCK * dil), BLOCK, stride=dil)
            prev = pl.ds(r + jnp.maximum(j - 1, 0) * (BLOCK * dil), BLOCK, stride=dil)
        table = jnp.where(j == 0, n_heads, 0) if use_prev else 0
        return cur, prev, table

    blocks_per_iter = _HEADS_PER_ITER // (2 * n_pairs)
    assert (dil * n_blocks) % blocks_per_iter == 0

    def body(it, carry):
        blocks = [block_rows(it * blocks_per_iter + b) for b in range(blocks_per_iter)]
        scores, values = {}, {}
        for b, (cur, prev, _) in enumerate(blocks):
            for p in range(n_pairs):
                k_ref, v_ref = k_refs[k_pair[p]], v_refs[k_pair[p]]
                q = q_refs[p][cur, :].astype(BF16)
                keys = k_ref[cur, :].astype(BF16)
                vals = v_ref[cur, :].astype(BF16)
                if use_prev:
                    keys = jnp.concatenate([k_ref[prev, :].astype(BF16), keys], axis=0)
                    vals = jnp.concatenate([v_ref[prev, :].astype(BF16), vals], axis=0)
                one = jnp.ones_like(vals)
                values[b, p, 0] = jnp.where(lo_keys, vals, one)
                values[b, p, 1] = jnp.where(lo_keys, one, vals)
                for half in range(2):
                    qm = jnp.where(lo if half == 0 else jnp.logical_not(lo), q, jnp.zeros_like(q))
                    scores[b, p, half] = _dot_nt(qm, keys)
        probs, row_max = {}, {}
        for (b, p, half), s in scores.items():
            s = s + bias_ref[blocks[b][2] + 2 * p + half]
            m = jnp.max(s, axis=-1, keepdims=True)
            probs[b, p, half], row_max[b, p, half] = jnp.exp2(s - m).astype(BF16), m
        for b, (cur, _, _) in enumerate(blocks):
            for p in range(n_pairs):
                acc = [_dot(probs[b, p, half], values[b, p, half]) for half in range(2)]
                o_refs[p][cur, :] = jnp.where(lo, acc[0], acc[1]).astype(o_refs[p].dtype)
                den_refs[p][cur, :] = jnp.where(lo, acc[1], acc[0])
                max_refs[p][cur, :] = jnp.where(lo, row_max[b, p, 0], row_max[b, p, 1])
        return carry

    lax.fori_loop(0, dil * n_blocks // blocks_per_iter, body, 0)


def _attn(slopes, q_arr, k_arr, v_arr, q_idx, k_idx, v_idx, *, n_seq, seq, k_pair, dil, head0, out_dtype, name):
    n_pairs = len(q_idx)

    def pair_spec(idx):
        return pl.BlockSpec((None, seq, LANES), lambda n: (n, 0, idx))

    view = lambda a: a.reshape(n_seq, seq, a.shape[-1])
    operands = [slopes] + [view(q_arr)] * n_pairs + [view(k_arr)] * len(k_idx) + [view(v_arr)] * len(v_idx)
    in_specs = [_smem_spec()] + [pair_spec(i) for i in tuple(q_idx) + tuple(k_idx) + tuple(v_idx)]
    out_shape = [jax.ShapeDtypeStruct((n_seq, seq, LANES), out_dtype)] * n_pairs
    out_shape += [jax.ShapeDtypeStruct((n_seq, seq, LANES), F32)] * (2 * n_pairs)
    outs = pl.pallas_call(
        functools.partial(_attn_kernel, n_pairs=n_pairs, k_pair=k_pair, dil=dil, seq=seq, head0=head0),
        grid=(n_seq,),
        in_specs=in_specs,
        out_specs=[pair_spec(0)] * len(out_shape),
        out_shape=out_shape,
        scratch_shapes=[pltpu.VMEM((2 * n_pairs, BLOCK, BLOCK) if seq == BLOCK * dil else
                                   (4 * n_pairs, BLOCK, 2 * BLOCK), F32)],
        compiler_params=_params(1),
        name=name,
    )(*operands)
    outs = [o.reshape(n_seq * seq, LANES) for o in outs]
    return outs[:n_pairs], outs[n_pairs:2 * n_pairs], outs[2 * n_pairs:]


def _rows_where(row_iota, values):
    out = jnp.zeros(row_iota.shape, F32)
    for r, v in enumerate(values):
        out = jnp.where(row_iota == r, v, out)
    return out


def _shift_cache(cache_ref, tail_ref, out_ref, n, length, rows):
    shifted = pltpu.roll(cache_ref[rows, :], length - 1, axis=1)
    new_col = pltpu.roll(tail_ref[rows, :], (LANES - 1) - n, axis=1)
    lane = lax.broadcasted_iota(jnp.int32, new_col.shape, 1)
    last = jnp.where(lane == LANES - 1, new_col, shifted[:, length - LANES:])
    out_ref[rows, :] = last if length == LANES else jnp.concatenate([shifted[:, :length - LANES], last], axis=1)


def _sample_stages(slopes_ref, sinks_ref, qa_ref, kva_ref, b0_ref, b12_ref,
                   ta_ref, tb0_ref, tb1_ref, tb2_ref, ca_ref, cb0_ref, cb1_ref, cb2_ref,
                   oa_ref, ob_ref, lse_ref, na_ref, nb0_ref, nb1_ref, nb2_ref):
    n = pl.program_id(0)
    row = pl.ds(n, 1)
    sub = lax.broadcasted_iota(jnp.int32, (8, LANES), 0)
    lane = lax.broadcasted_iota(jnp.int32, (8, LANES), 1)
    half = lane // HEAD_DIM
    kv_head = sub // A_GROUP
    sub_b = lax.broadcasted_iota(jnp.int32, (8, B_GW), 0)
    head_mask = (lax.broadcasted_iota(jnp.int32, (8, B_GW), 1) // HEAD_DIM) == sub_b
    b_refs = ((cb0_ref, tb0_ref, nb0_ref), (cb1_ref, tb1_ref, nb1_ref), (cb2_ref, tb2_ref, nb2_ref))
    state = {}

    def scores():
        q_row = qa_ref[row, :]
        q_blk = jnp.zeros((8, LANES), F32)
        for r in range(A_Q_HEADS):
            k, g = divmod(r, A_GROUP)
            chunk = q_row[:, (r // 2) * LANES:(r // 2 + 1) * LANES]
            if g % 2 != k:
                chunk = pltpu.roll(chunk, HEAD_DIM, axis=1)
            q_blk = jnp.where((sub == r) & (half == k), jnp.broadcast_to(chunk, (8, LANES)), q_blk)
        kva = kva_ref[row, :]
        k_new = jnp.where(lane[:1] < HEAD_DIM, kva[:, 0:LANES], kva[:, LANES:2 * LANES])
        v_new = jnp.where(lane[:1] < HEAD_DIM, kva[:, 2 * LANES:3 * LANES], kva[:, 3 * LANES:])
        s = _dot(q_blk.astype(BF16), ca_ref[0:A_KVW, :].astype(BF16))
        state["a"] = (s, jnp.sum(q_blk * k_new, axis=-1, keepdims=True), v_new)
        for g, (c_ref, _, _) in enumerate(b_refs):
            qkv = b0_ref[row, :] if g == 0 else b12_ref[row, (g - 1) * 3 * B_GW:g * 3 * B_GW]
            q_row, k_new, v_new = qkv[:, 0:B_GW], qkv[:, B_GW:2 * B_GW], qkv[:, 2 * B_GW:]
            q_blk = jnp.where(head_mask, jnp.broadcast_to(q_row, (8, B_GW)), 0.0)
            q_col = jnp.broadcast_to(q_row, (8, B_GW)).T[:, 0:1]
            prod = (c_ref[0:B_GW, :] * q_col).reshape(B_HEADS_PER_GROUP, HEAD_DIM, c_ref.shape[1])
            s4 = jnp.sum(prod, axis=1)
            s = jnp.concatenate([s4, jnp.zeros_like(s4)], axis=0)
            state["b", g] = (s, jnp.sum(q_blk * k_new, axis=-1, keepdims=True), v_new)

    def softmaxes():
        s, s_new, v_new = state["a"]
        slope = _rows_where(sub[:, :1], [slopes_ref[r] * LOG2E for r in range(A_Q_HEADS)])
        sink = _rows_where(sub[:, :1], [sinks_ref[r] * LOG2E for r in range(A_Q_HEADS)])
        s = s - slope * (BLOCK - lane).astype(F32)
        m = jnp.maximum(jnp.maximum(jnp.max(s, axis=-1, keepdims=True), s_new), sink)
        e = jnp.exp2(s - m)
        e_new = jnp.exp2(s_new - m)
        denom = jnp.sum(e, axis=-1, keepdims=True) + e_new + jnp.exp2(sink - m)
        state["a"] = (e.astype(BF16), e_new, v_new, m, denom)
        for g, (length, dil) in enumerate(B_PATTERNS):
            s, s_new, v_new = state["b", g]
            head0 = A_Q_HEADS + g * B_HEADS_PER_GROUP
            sub_l = lax.broadcasted_iota(jnp.int32, (8, length), 0)
            col = lax.broadcasted_iota(jnp.int32, (8, length), 1)
            slope = _rows_where(sub_l, [slopes_ref[head0 + h] * LOG2E for h in range(B_HEADS_PER_GROUP)])
            s = jnp.where((col & (dil - 1)) == 0, s - slope * (length - col).astype(F32), NEG)
            m = jnp.maximum(jnp.max(s, axis=-1, keepdims=True), s_new)
            e = jnp.exp2(s - m)
            e_new = jnp.exp2(s_new - m)
            denom = jnp.sum(e, axis=-1, keepdims=True) + e_new
            state["b", g] = (e, e_new, v_new, m, denom)

    def values():
        e, e_new, v_new, m, denom = state["a"]
        acc = _dot_nt(e, ca_ref[A_KVW:2 * A_KVW, :].astype(BF16))
        out = jnp.where(half == kv_head, (acc + e_new * v_new) / denom, 0.0)
        for c in range(A_W // LANES):
            k = c // 2
            pieces = []
            for hh in range(2):
                r = k * A_GROUP + 2 * (c % 2) + hh
                piece = out[r:r + 1, :]
                pieces.append(piece if hh == k else pltpu.roll(piece, HEAD_DIM, axis=1))
            oa_ref[:, c * LANES:(c + 1) * LANES] = jnp.where(lane[:1] < HEAD_DIM, pieces[0], pieces[1])
        for g, (c_ref, _, _) in enumerate(b_refs):
            e, e_new, v_new, m, denom = state["b", g]
            length = c_ref.shape[1]
            weights = jnp.broadcast_to(e[0:B_HEADS_PER_GROUP, None, :], (B_HEADS_PER_GROUP, HEAD_DIM, length))
            acc_col = jnp.sum(c_ref[B_GW:2 * B_GW, :] * weights.reshape(B_GW, length), axis=1, keepdims=True)
            acc = jnp.broadcast_to(jnp.broadcast_to(acc_col, (B_GW, 8)).T[0:1, :], (8, B_GW))
            out = jnp.where(head_mask, (acc + e_new * v_new) / denom, 0.0)
            ob_ref[g] = jnp.sum(out, axis=0, keepdims=True)
            lse = jnp.where(head_mask, m + jnp.log2(denom), 0.0)
            lse_ref[g] = jnp.sum(lse, axis=0, keepdims=True)

    def shift(c_ref, t_ref, n_ref, length, piece, n_pieces):
        rows_per = c_ref.shape[0] // n_pieces
        return lambda: _shift_cache(c_ref, t_ref, n_ref, n, length, slice(piece * rows_per, (piece + 1) * rows_per))

    def shift_small():
        shift(ca_ref, ta_ref, na_ref, BLOCK, 0, 1)()
        shift(cb0_ref, tb0_ref, nb0_ref, B_PATTERNS[0][0], 0, 1)()

    shift_b1 = shift(cb1_ref, tb1_ref, nb1_ref, B_PATTERNS[1][0], 0, 1)
    shift_b2 = [shift(cb2_ref, tb2_ref, nb2_ref, B_PATTERNS[2][0], piece, _N_SHIFT_PIECES)
                for piece in range(_N_SHIFT_PIECES)]
    return [scores, shift_small, softmaxes, shift_b1, values] + shift_b2


_N_SHIFT_PIECES = 4


_N_FFN_INPUTS = 5
_N_SAMPLE_INPUTS = 14
_N_SAMPLE_OUTPUTS = 7


def _ffn_sample_kernel(*refs):
    ffn_in, refs = refs[:_N_FFN_INPUTS], refs[_N_FFN_INPUTS:]
    sample_in, refs = refs[:_N_SAMPLE_INPUTS], refs[_N_SAMPLE_INPUTS:]
    o_ref, sample_out, (act_ref,) = refs[0], refs[1:1 + _N_SAMPLE_OUTPUTS], refs[1 + _N_SAMPLE_OUTPUTS:]
    x_ref, gain_ref, wg_ref, wu_ref, wd_ref = ffn_in
    o_ref[...] = _swiglu_residual(x_ref[...], gain_ref[...], wg_ref, wu_ref, wd_ref, act_ref,
                                  side_work=_sample_stages(*sample_in, *sample_out))


def _ffn_and_sample(x, gain, wg, wu, wd, slopes, sinks, qa, kva, b0, b12, tails, caches):
    n_seq = qa.shape[0]
    t = x.shape[0]
    assert t % n_seq == 0
    tm = t // n_seq
    full = lambda a: pl.BlockSpec(a.shape, lambda n: (0,) * a.ndim)
    per_seq = lambda a: pl.BlockSpec((None,) + a.shape[1:], lambda n: (n, 0, 0))
    small = [qa, kva, b0, b12] + list(tails)
    assert 2 + len(small) + len(caches) == _N_SAMPLE_INPUTS
    out_shape = [jax.ShapeDtypeStruct((t, D_MODEL), F32),
                 jax.ShapeDtypeStruct((n_seq, 1, A_W), F32),
                 jax.ShapeDtypeStruct((B_N_GROUPS, n_seq, 1, B_GW), F32),
                 jax.ShapeDtypeStruct((B_N_GROUPS, n_seq, 1, B_GW), F32)]
    out_shape += [jax.ShapeDtypeStruct(c.shape, F32) for c in caches]
    out_specs = [pl.BlockSpec((tm, D_MODEL), lambda n: (n, 0)),
                 pl.BlockSpec((None, 1, A_W), lambda n: (n, 0, 0)),
                 pl.BlockSpec((B_N_GROUPS, None, 1, B_GW), lambda n: (0, n, 0, 0)),
                 pl.BlockSpec((B_N_GROUPS, None, 1, B_GW), lambda n: (0, n, 0, 0))]
    out_specs += [per_seq(c) for c in caches]
    x1, oa, ob, lse, *new_caches = pl.pallas_call(
        _ffn_sample_kernel,
        grid=(n_seq,),
        in_specs=[pl.BlockSpec((tm, D_MODEL), lambda n: (n, 0)),
                  _const_spec((1, D_MODEL)), _const_spec((D_MODEL, D_FF)),
                  _const_spec((D_MODEL, D_FF)), _const_spec((D_FF, D_MODEL))]
                 + [_smem_spec(), _smem_spec()] + [full(a) for a in small] + [per_seq(c) for c in caches],
        out_specs=out_specs,
        out_shape=out_shape,
        scratch_shapes=[pltpu.VMEM((tm, D_FF), BF16)],
        compiler_params=_params(1),
        name="ffn1_sample",
    )(x, gain, wg, wu, wd, slopes, sinks, *small, *caches)
    return (x1, oa.reshape(n_seq, A_W), ob.reshape(B_N_GROUPS, n_seq, B_GW),
            lse.reshape(B_N_GROUPS, n_seq, B_GW), *new_caches)


_N_A_PAIRS = A_W // LANES
_N_B_PAIRS = B_GW // LANES


def _merge_kernel(*refs):
    refs = list(refs)
    take = lambda count: [refs.pop(0) for _ in range(count)]
    (x_ref,), oa_refs, da_refs, ma_refs = take(1), take(_N_A_PAIRS), take(_N_A_PAIRS), take(_N_A_PAIRS)
    n_b = B_N_GROUPS * _N_B_PAIRS
    ob_refs, db_refs, mb_refs = take(n_b), take(n_b), take(n_b)
    sinks_ref, gates_ref, wua_ref, wub_ref, wo_ref, gain_ref, wg_ref, wu_ref, wd_ref, y_ref, act_ref = refs

    unswap = lambda ref: pltpu.roll(ref[...], HEAD_DIM, axis=1)
    oa_pairs = []
    for p, (o_ref, d_ref, m_ref) in enumerate(zip(oa_refs, da_refs, ma_refs)):
        denom = unswap(d_ref) + jnp.exp2(sinks_ref[:, p * LANES:(p + 1) * LANES] - m_ref[...])
        oa_pairs.append((o_ref[...].astype(F32) / denom).astype(BF16))
    ob_pairs = []
    for p in range(_N_B_PAIRS):
        idx = [g * _N_B_PAIRS + p for g in range(B_N_GROUPS)]
        maxes = [mb_refs[i][...] for i in idx]
        top = functools.reduce(jnp.maximum, maxes)
        weights = [jnp.exp2(m - top) for m in maxes]
        num = sum(w * ob_refs[i][...] for w, i in zip(weights, idx))
        den = sum(w * unswap(db_refs[i]) for w, i in zip(weights, idx))
        ob_pairs.append((num / den).astype(BF16))
    ua = _dot(jnp.concatenate(oa_pairs, axis=1), wua_ref[...])
    ub = _dot(jnp.concatenate(ob_pairs, axis=1), wub_ref[...])
    gate_a = gates_ref[:, :D_MODEL].astype(F32)
    gate_b = gates_ref[:, D_MODEL:].astype(F32)
    mixed = (gate_a * ua + gate_b * ub).astype(BF16)
    x = x_ref[...] + _dot(mixed, wo_ref[...])
    y_ref[...] = _swiglu_residual(x, gain_ref[...], wg_ref, wu_ref, wd_ref, act_ref)


def _merge(x, mixer_a, mixer_b, sink_lanes, gates, wua, wub, wo, gain, wg, wu, wd, tm):
    t = x.shape[0]
    row_spec = lambda w: pl.BlockSpec((tm, w), lambda i: (i, 0))
    pair_inputs = [a for part in tuple(mixer_a) + tuple(mixer_b) for a in part]
    n_pair_inputs = len(pair_inputs)
    assert n_pair_inputs == 3 * _N_A_PAIRS + 3 * B_N_GROUPS * _N_B_PAIRS
    return pl.pallas_call(
        _merge_kernel,
        grid=(t // tm,),
        in_specs=[row_spec(D_MODEL)] + [row_spec(LANES)] * n_pair_inputs + [_const_spec((1, A_W)),
                  row_spec(2 * D_MODEL), _const_spec((A_W, D_MODEL)), _const_spec((B_GW, D_MODEL)), _const_spec((D_MODEL, D_MODEL)),
                  _const_spec((1, D_MODEL)), _const_spec((D_MODEL, D_FF)), _const_spec((D_MODEL, D_FF)),
                  _const_spec((D_FF, D_MODEL))],
        out_specs=row_spec(D_MODEL),
        out_shape=jax.ShapeDtypeStruct((t, D_MODEL), F32),
        scratch_shapes=[pltpu.VMEM((tm, D_FF), BF16)],
        compiler_params=_params(1),
        name="merge_ffn2",
    )(x, *pair_inputs, sink_lanes, gates, wua, wub, wo, gain, wg, wu, wd)


def _cache_view(cache):
    _, n, length, two, h, d = cache.shape
    return jnp.transpose(cache, (0, 1, 3, 4, 5, 2)).reshape(n, two * h * d, length)


def _state_view(rows_by_len, heads):
    n, _, length = rows_by_len.shape
    return jnp.transpose(rows_by_len.reshape(1, n, 2, heads, HEAD_DIM, length), (0, 1, 5, 2, 3, 4))


def kernel(x_prompt, x_sample, cache_a_kv, cache_b1_kv, cache_b2_kv, cache_b3_kv, norm_ffn1, w1_gate, w1_up,
           w1_down, norm_mix, w_in, q_norm_a, k_norm_a, q_norm_b, k_norm_b, sinks_a, w_up_a, w_up_b, w_o,
           norm_ffn2, w2_gate, w2_up, w2_down):
    assert x_prompt.shape[-1] == D_MODEL and w_in.shape == (1, D_MODEL, IN_W)
    batch, seq, _ = x_prompt.shape
    dec = x_sample.shape[0]
    assert x_sample.shape[1] == 1 and seq % (BLOCK * B_PATTERNS[-1][1]) == 0
    assert cache_a_kv.shape[2] == BLOCK
    assert all(c.shape[2] == w for c, (w, _) in zip((cache_b1_kv, cache_b2_kv, cache_b3_kv), B_PATTERNS))

    bf = lambda w: w[0].astype(BF16)
    wg1, wu1, wd1, w_in_b = map(bf, (w1_gate, w1_up, w1_down, w_in))
    late_weights = [w[0] for w in (w_up_a, w_up_b, w_o, w2_gate, w2_up, w2_down)]

    i = jnp.arange(1, N_ALIBI_HEADS + 1, dtype=F32)
    slopes = jnp.exp2(-8.0 * i / N_ALIBI_HEADS)
    sinks = sinks_a[0].reshape(A_Q_HEADS).astype(F32)
    ones64 = jnp.ones((HEAD_DIM,), F32)
    q_scale = ATTN_SCALE * LOG2E
    qkgain = jnp.concatenate([
        jnp.tile(q_norm_a[0] * q_scale, A_Q_HEADS), jnp.tile(k_norm_a[0], A_KV_HEADS),
        jnp.tile(ones64, A_KV_HEADS),
        jnp.tile(q_norm_b[0] * q_scale, B_N_GROUPS * B_HEADS_PER_GROUP),
        jnp.tile(k_norm_b[0], B_N_GROUPS * B_HEADS_PER_GROUP),
        jnp.tile(ones64, B_N_GROUPS * B_HEADS_PER_GROUP)]).reshape(1, QKV_W).astype(F32)
    head_of = jnp.arange(CHUNK) // HEAD_DIM
    ones_bd = (head_of[:, None] == head_of[None, :]).astype(BF16)

    def proj(x1, n_seq, s, tm, riders=()):
        return _proj(x1, norm_mix, w_in_b, qkgain, ones_bd, n_seq, s, tm, riders)

    x1s = _ffn(x_sample.reshape(dec, D_MODEL), norm_ffn1, wg1, wu1, wd1, dec)
    (qa_s, kva_s, b0_s, b12_s, gates_s, *tails_s), _ = proj(x1s, 1, dec, dec)
    caches = [_cache_view(c) for c in (cache_a_kv, cache_b1_kv, cache_b2_kv, cache_b3_kv)]
    x1p, oa_s, ob_s, lse_s, na, nb0, nb1, nb2 = _ffn_and_sample(
        x_prompt.reshape(batch * seq, D_MODEL), norm_ffn1, wg1, wu1, wd1,
        slopes, sinks, qa_s.astype(F32), kva_s.astype(F32), b0_s.astype(F32), b12_s,
        [t[0] for t in tails_s], caches)

    (qa, kva, b0, b12, gates, ta, tb0, tb1, tb2), (wua, wub, wo, wg2, wu2, wd2) = proj(
        x1p, batch, seq, 512, late_weights)

    def back(x1, mixer_a, mixer_b, sink_lanes, gates, tm):
        return _merge(x1, mixer_a, mixer_b, sink_lanes, gates, wua, wub, wo, norm_ffn2, wg2, wu2, wd2, tm)

    common = dict(n_seq=batch, seq=seq)
    mixer_a = _attn(slopes, qa, kva, kva, (0, 1, 2, 3), (0, 1), (2, 3), k_pair=(0, 0, 1, 1), dil=1,
                    head0=0, out_dtype=BF16, name="attn_a", **common)
    mixer_b = ([], [], [])
    for g, (_, dil) in enumerate(B_PATTERNS):
        src = b0 if g == 0 else b12
        first = 0 if g == 0 else (g - 1) * 3 * _N_B_PAIRS
        parts = _attn(slopes, src, src, src, (first, first + 1), (first + 2, first + 3), (first + 4, first + 5),
                      k_pair=(0, 1), dil=dil, head0=A_Q_HEADS + g * B_HEADS_PER_GROUP, out_dtype=F32,
                      name=f"attn_b{g}", **common)
        for acc, part in zip(mixer_b, parts):
            acc += part
    sink_lanes = jnp.repeat(sinks * LOG2E, HEAD_DIM).reshape(1, A_W)
    y_prompt = back(x1p, mixer_a, mixer_b, sink_lanes, gates, 512).reshape(batch, seq, D_MODEL)

    pairs = lambda a: [a[..., p * LANES:(p + 1) * LANES] for p in range(a.shape[-1] // LANES)]
    one, zero = jnp.ones((dec, LANES), F32), jnp.zeros((dec, LANES), F32)
    ob_s = [pair for g in range(B_N_GROUPS) for pair in pairs(ob_s[g])]
    lse_s = [pair for g in range(B_N_GROUPS) for pair in pairs(lse_s[g])]
    y_sample = back(x1s, (pairs(oa_s.astype(BF16)), [one] * _N_A_PAIRS, [zero] * _N_A_PAIRS),
                    (ob_s, [one] * len(ob_s), lse_s), jnp.full((1, A_W), NEG, F32), gates_s,
                    dec).reshape(dec, 1, D_MODEL)

    return (y_prompt, y_sample,
            _state_view(ta, A_KV_HEADS), _state_view(tb0, B_HEADS_PER_GROUP),
            _state_view(tb1, B_HEADS_PER_GROUP), _state_view(tb2, B_HEADS_PER_GROUP),
            _state_view(na, A_KV_HEADS), _state_view(nb0, B_HEADS_PER_GROUP),
            _state_view(nb1, B_HEADS_PER_GROUP), _state_view(nb2, B_HEADS_PER_GROUP))
```

```python
import functools
import math

import jax
import jax.numpy as jnp
from jax import lax
from jax.experimental import pallas as pl
from jax.experimental.pallas import tpu as pltpu

D_MODEL = 1024
D_FF = 2816
HEAD_DIM = 64
A_Q_HEADS = 8
A_KV_HEADS = 2
A_GROUP = A_Q_HEADS // A_KV_HEADS
B_PATTERNS = ((128, 1), (512, 4), (2048, 16))
B_HEADS_PER_GROUP = 4
B_N_GROUPS = 3
N_ALIBI_HEADS = A_Q_HEADS + B_N_GROUPS * B_HEADS_PER_GROUP
BLOCK = 128
EPS = 1e-6
ATTN_SCALE = HEAD_DIM ** -0.5
LOG2E = math.log2(math.e)
A_W = A_Q_HEADS * HEAD_DIM
A_KVW = A_KV_HEADS * HEAD_DIM
B_GW = B_HEADS_PER_GROUP * HEAD_DIM
B_W = B_N_GROUPS * B_GW
QKV_W = A_W + 2 * A_KVW + 3 * B_W
IN_W = QKV_W + 2 * D_MODEL

LANES = 128
CHUNK = 256
NEG = -1e30
VMEM_LIMIT = 56 * 1024 * 1024

F32 = jnp.float32
BF16 = jnp.bfloat16


def _const_spec(shape):
    nd = len(shape)
    return pl.BlockSpec(shape, lambda *_: (0,) * nd, pipeline_mode=pl.Buffered(1))


def _smem_spec():
    return pl.BlockSpec(memory_space=pltpu.SMEM)


def _params(n_axes):
    return pltpu.CompilerParams(
        dimension_semantics=("arbitrary",) * n_axes, vmem_limit_bytes=VMEM_LIMIT)


def _dot(a, b):
    return jnp.dot(a, b, preferred_element_type=F32)


def _dot_nt(a, b):
    return lax.dot_general(a, b, (((1,), (1,)), ((), ())), preferred_element_type=F32)


def _swiglu_residual(x, gain, wg_ref, wu_ref, wd_ref, act_ref, side_work=()):
    n_chunks = D_FF // CHUNK
    assert len(side_work) <= n_chunks
    h = (x * gain).astype(BF16)
    inv_rms = lax.rsqrt(jnp.mean(x * x, axis=-1, keepdims=True) + EPS)
    for c in range(n_chunks):
        if c < len(side_work):
            side_work[c]()
        sl = slice(c * CHUNK, (c + 1) * CHUNK)
        g = _dot(h, wg_ref[:, sl]) * inv_rms
        u = _dot(h, wu_ref[:, sl]) * inv_rms
        act_ref[:, sl] = (g * jax.nn.sigmoid(g) * u).astype(BF16)
    return x + 0.5 * _dot(act_ref[...], wd_ref[...])


def _ffn_kernel(x_ref, gain_ref, wg_ref, wu_ref, wd_ref, o_ref, act_ref):
    o_ref[...] = _swiglu_residual(x_ref[...], gain_ref[...], wg_ref, wu_ref, wd_ref, act_ref)


def _ffn(x, gain, wg, wu, wd, tm):
    t = x.shape[0]
    return pl.pallas_call(
        _ffn_kernel,
        grid=(t // tm,),
        in_specs=[pl.BlockSpec((tm, D_MODEL), lambda i: (i, 0)),
                  _const_spec((1, D_MODEL)), _const_spec((D_MODEL, D_FF)),
                  _const_spec((D_MODEL, D_FF)), _const_spec((D_FF, D_MODEL))],
        out_specs=pl.BlockSpec((tm, D_MODEL), lambda i: (i, 0)),
        out_shape=jax.ShapeDtypeStruct((t, D_MODEL), F32),
        scratch_shapes=[pltpu.VMEM((tm, D_FF), BF16)],
        compiler_params=_params(1),
        name="ffn1",
    )(x, gain, wg, wu, wd)


_N_QKV_CHUNKS = QKV_W // CHUNK
_N_CHUNKS = IN_W // CHUNK
_LAST_CHUNK = 9


def _tail_plan(tail, tm, n_tiles):
    if tail >= tm:
        return tm, n_tiles - tail // tm
    return tail, n_tiles - 1


_N_PROJ_INPUTS = 5
_N_PROJ_OUTPUTS = 9
_BF16_ROWS = 16


def _proj_kernel(*refs, tm, n_tiles, tails):
    n_riders = (len(refs) - _N_PROJ_INPUTS - _N_PROJ_OUTPUTS) // 2
    x_ref, gain_ref, w_ref, qkgain_ref, ones_ref = refs[:_N_PROJ_INPUTS]
    rider_in = refs[_N_PROJ_INPUTS:_N_PROJ_INPUTS + n_riders]
    outs = refs[_N_PROJ_INPUTS + n_riders:]
    qa_ref, kva_ref, b0_ref, b12_ref, gates_ref, ta_ref, tb0_ref, tb1_ref, tb2_ref = outs[:_N_PROJ_OUTPUTS]
    for src, dst in zip(rider_in, outs[_N_PROJ_OUTPUTS:]):
        dst[...] = src[...].astype(BF16)
    x = x_ref[...]
    h = (x * gain_ref[...]).astype(BF16)
    inv_rms = lax.rsqrt(jnp.mean(x * x, axis=-1, keepdims=True) + EPS)
    lane = lax.broadcasted_iota(jnp.int32, (tm, CHUNK), 1)
    lane1 = lax.broadcasted_iota(jnp.int32, (tm, LANES), 1)

    def z_chunk(c):
        return _dot(h, w_ref[:, c * CHUNK:(c + 1) * CHUNK]) * inv_rms

    def head_norm(z, c):
        ss = _dot((z * z).astype(BF16), ones_ref[...])
        return z * lax.rsqrt(ss * (1.0 / HEAD_DIM) + EPS) * qkgain_ref[:, c * CHUNK:(c + 1) * CHUNK]

    def write_tail(t_ref, row0, y, tail):
        block_w, _ = _tail_plan(tail, tm, n_tiles)
        data = y if block_w == tm else y[tm - block_w:, :]
        t_ref[row0:row0 + CHUNK, :] = data.T

    tb_refs = (tb0_ref, tb1_ref, tb2_ref)

    def consume(c, z):
        if c < 2:
            qa_ref[:, c * CHUNK:(c + 1) * CHUNK] = head_norm(z, c).astype(BF16)
        elif c == 2:
            y = jnp.where(lane < A_KVW, head_norm(z, c), z)
            write_tail(ta_ref, 0, y, tails[0])
            for part in range(2):
                pair = y[:, part * LANES:(part + 1) * LANES]
                swapped = pltpu.roll(pair, HEAD_DIM, axis=1)
                base = part * 2 * LANES
                kva_ref[:, base:base + LANES] = jnp.where(lane1 < HEAD_DIM, pair, swapped).astype(BF16)
                kva_ref[:, base + LANES:base + 2 * LANES] = jnp.where(lane1 < HEAD_DIM, swapped, pair).astype(BF16)
        elif c < _N_QKV_CHUNKS:
            kind, g = divmod(c - 3, B_N_GROUPS)
            y = z if kind == 2 else head_norm(z, c)
            if g == 0:
                b0_ref[:, kind * CHUNK:(kind + 1) * CHUNK] = y.astype(BF16)
            else:
                col = ((g - 1) * 3 + kind) * CHUNK
                b12_ref[:, col:col + CHUNK] = y
            if kind > 0:
                write_tail(tb_refs[g], (kind - 1) * CHUNK, y, tails[1 + g])
        else:
            col = (c - _N_QKV_CHUNKS) * CHUNK
            gates_ref[:, col:col + CHUNK] = jax.nn.sigmoid(z).astype(BF16)

    order = list(range(_N_QKV_CHUNKS, _N_CHUNKS)) + [c for c in range(_N_QKV_CHUNKS) if c != _LAST_CHUNK]
    order.append(_LAST_CHUNK)
    z_next = z_chunk(order[0])
    for i, c in enumerate(order):
        z = z_next
        if i + 1 < len(order):
            z_next = z_chunk(order[i + 1])
        consume(c, z)


def _rider_specs(weights, n_steps, n_tiles):
    specs = []
    for w in weights:
        rows, cols = w.shape
        n_blocks = n_steps
        while rows % (n_blocks * _BF16_ROWS):
            assert n_blocks % 2 == 0, (rows, n_steps)
            n_blocks //= 2
        per = n_steps // n_blocks
        specs.append(pl.BlockSpec((rows // n_blocks, cols), lambda n, j, per=per: ((n * n_tiles + j) // per, 0)))
    return specs


def _proj(x, gain, w_in, qkgain, ones_bd, n_seq, seq, tm, riders=()):
    t = n_seq * seq
    n_tiles = seq // tm
    rider_specs = _rider_specs(riders, n_seq * n_tiles, n_tiles)
    tails = (min(128, seq),) + tuple(min(w, seq) for w, _ in B_PATTERNS)
    tail_rows = (2 * A_KVW, 2 * B_GW, 2 * B_GW, 2 * B_GW)

    def tail_spec(rows, tail):
        block_w, first = _tail_plan(tail, tm, n_tiles)
        return pl.BlockSpec((None, rows, block_w), lambda n, j: (n, 0, jnp.maximum(j - first, 0)))

    row_spec = lambda w: pl.BlockSpec((tm, w), lambda n, j: (n * n_tiles + j, 0))
    out_shape = [
        jax.ShapeDtypeStruct((t, A_W), BF16),
        jax.ShapeDtypeStruct((t, 4 * LANES), BF16),
        jax.ShapeDtypeStruct((t, 3 * B_GW), BF16),
        jax.ShapeDtypeStruct((t, 6 * B_GW), F32),
        jax.ShapeDtypeStruct((t, 2 * D_MODEL), BF16),
    ] + [jax.ShapeDtypeStruct((n_seq, r, tl), F32) for r, tl in zip(tail_rows, tails)]
    out_specs = [row_spec(A_W), row_spec(4 * LANES), row_spec(3 * B_GW), row_spec(6 * B_GW),
                 row_spec(2 * D_MODEL)] + [tail_spec(r, tl) for r, tl in zip(tail_rows, tails)]
    assert len(out_shape) == _N_PROJ_OUTPUTS
    out_shape += [jax.ShapeDtypeStruct(w.shape, BF16) for w in riders]
    outs = pl.pallas_call(
        functools.partial(_proj_kernel, tm=tm, n_tiles=n_tiles, tails=tails),
        grid=(n_seq, n_tiles),
        in_specs=[row_spec(D_MODEL), _const_spec((1, D_MODEL)), _const_spec((D_MODEL, IN_W)),
                  _const_spec((1, QKV_W)), _const_spec((CHUNK, CHUNK))] + rider_specs,
        out_specs=out_specs + _rider_specs(riders, n_seq * n_tiles, n_tiles),
        out_shape=out_shape,
        compiler_params=_params(2),
        name="proj",
    )(x, gain, w_in, qkgain, ones_bd, *riders)
    return outs[:_N_PROJ_OUTPUTS], outs[_N_PROJ_OUTPUTS:]


_HEADS_PER_ITER = 32


def _attn_kernel(*refs, n_pairs, k_pair, dil, seq, head0):
    refs = list(refs)
    slopes_ref = refs.pop(0)
    n_kv = max(k_pair) + 1
    take = lambda count: [refs.pop(0) for _ in range(count)]
    q_refs, k_refs, v_refs = take(n_pairs), take(n_kv), take(n_kv)
    o_refs, den_refs, max_refs = take(n_pairs), take(n_pairs), take(n_pairs)
    (bias_ref,) = refs

    n_blocks = seq // (BLOCK * dil)
    use_prev = n_blocks > 1
    n_heads = 2 * n_pairs
    qi = lax.broadcasted_iota(jnp.int32, (BLOCK, BLOCK), 0)
    ki = lax.broadcasted_iota(jnp.int32, (BLOCK, BLOCK), 1)
    d_cur = qi - ki
    lo = ki < HEAD_DIM
    lo_keys = lax.broadcasted_iota(jnp.int32, ((2 if use_prev else 1) * BLOCK, LANES), 1) < HEAD_DIM

    for h in range(n_heads):
        slope = slopes_ref[head0 + h] * (float(dil) * LOG2E)
        cur_bias = jnp.where(d_cur >= 0, -slope * d_cur.astype(F32), NEG)
        if use_prev:
            prev_bias = jnp.where(d_cur <= 0, -slope * (d_cur + BLOCK).astype(F32), NEG)
            bias_ref[h] = jnp.concatenate([prev_bias, cur_bias], axis=1)
            bias_ref[n_heads + h] = jnp.concatenate([jnp.full_like(prev_bias, NEG), cur_bias], axis=1)
        else:
            bias_ref[h] = cur_bias

    def block_rows(step):
        if n_blocks == 1:
            r, j = step, 0
        else:
            r, j = step // n_blocks, step % n_blocks
        if dil == 1:
            cur = pl.ds(pl.multiple_of(j * BLOCK, BLOCK), BLOCK)
            prev = pl.ds(pl.multiple_of(jnp.maximum(j - 1, 0) * BLOCK, BLOCK), BLOCK)
        else:
            cur = pl.ds(r + j * (BLOCK * dil), BLOCK, stride=dil)
            prev = pl.ds(r + jnp.maximum(j - 1, 0) * (BLOCK * dil), BLOCK, stride=dil)
        table = jnp.where(j == 0, n_heads, 0) if use_prev else 0
        return cur, prev, table

    blocks_per_iter = _HEADS_PER_ITER // (2 * n_pairs)
    assert (dil * n_blocks) % blocks_per_iter == 0

    def body(it, carry):
        blocks = [block_rows(it * blocks_per_iter + b) for b in range(blocks_per_iter)]
        scores, values = {}, {}
        for b, (cur, prev, _) in enumerate(blocks):
            for p in range(n_pairs):
                k_ref, v_ref = k_refs[k_pair[p]], v_refs[k_pair[p]]
                q = q_refs[p][cur, :].astype(BF16)
                keys = k_ref[cur, :].astype(BF16)
                vals = v_ref[cur, :].astype(BF16)
                if use_prev:
                    keys = jnp.concatenate([k_ref[prev, :].astype(BF16), keys], axis=0)
                    vals = jnp.concatenate([v_ref[prev, :].astype(BF16), vals], axis=0)
                one = jnp.ones_like(vals)
                values[b, p, 0] = jnp.where(lo_keys, vals, one)
                values[b, p, 1] = jnp.where(lo_keys, one, vals)
                for half in range(2):
                    qm = jnp.where(lo if half == 0 else jnp.logical_not(lo), q, jnp.zeros_like(q))
                    scores[b, p, half] = _dot_nt(qm, keys)
        probs, row_max = {}, {}
        for (b, p, half), s in scores.items():
            s = s + bias_ref[blocks[b][2] + 2 * p + half]
            m = jnp.max(s, axis=-1, keepdims=True)
            probs[b, p, half], row_max[b, p, half] = jnp.exp2(s - m).astype(BF16), m
        for b, (cur, _, _) in enumerate(blocks):
            for p in range(n_pairs):
                acc = [_dot(probs[b, p, half], values[b, p, half]) for half in range(2)]
                o_refs[p][cur, :] = jnp.where(lo, acc[0], acc[1]).astype(o_refs[p].dtype)
                den_refs[p][cur, :] = jnp.where(lo, acc[1], acc[0])
                max_refs[p][cur, :] = jnp.where(lo, row_max[b, p, 0], row_max[b, p, 1])
        return carry

    lax.fori_loop(0, dil * n_blocks // blocks_per_iter, body, 0)


def _attn(slopes, q_arr, k_arr, v_arr, q_idx, k_idx, v_idx, *, n_seq, seq, k_pair, dil, head0, out_dtype, name):
    n_pairs = len(q_idx)

    def pair_spec(idx):
        return pl.BlockSpec((None, seq, LANES), lambda n: (n, 0, idx))

    view = lambda a: a.reshape(n_seq, seq, a.shape[-1])
    operands = [slopes] + [view(q_arr)] * n_pairs + [view(k_arr)] * len(k_idx) + [view(v_arr)] * len(v_idx)
    in_specs = [_smem_spec()] + [pair_spec(i) for i in tuple(q_idx) + tuple(k_idx) + tuple(v_idx)]
    out_shape = [jax.ShapeDtypeStruct((n_seq, seq, LANES), out_dtype)] * n_pairs
    out_shape += [jax.ShapeDtypeStruct((n_seq, seq, LANES), F32)] * (2 * n_pairs)
    outs = pl.pallas_call(
        functools.partial(_attn_kernel, n_pairs=n_pairs, k_pair=k_pair, dil=dil, seq=seq, head0=head0),
        grid=(n_seq,),
        in_specs=in_specs,
        out_specs=[pair_spec(0)] * len(out_shape),
        out_shape=out_shape,
        scratch_shapes=[pltpu.VMEM((2 * n_pairs, BLOCK, BLOCK) if seq == BLOCK * dil else
                                   (4 * n_pairs, BLOCK, 2 * BLOCK), F32)],
        compiler_params=_params(1),
        name=name,
    )(*operands)
    outs = [o.reshape(n_seq * seq, LANES) for o in outs]
    return outs[:n_pairs], outs[n_pairs:2 * n_pairs], outs[2 * n_pairs:]


def _rows_where(row_iota, values):
    out = jnp.zeros(row_iota.shape, F32)
    for r, v in enumerate(values):
        out = jnp.where(row_iota == r, v, out)
    return out


def _shift_cache(cache_ref, tail_ref, out_ref, n, length, rows):
    shifted = pltpu.roll(cache_ref[rows, :], length - 1, axis=1)
    new_col = pltpu.roll(tail_ref[rows, :], (LANES - 1) - n, axis=1)
    lane = lax.broadcasted_iota(jnp.int32, new_col.shape, 1)
    last = jnp.where(lane == LANES - 1, new_col, shifted[:, length - LANES:])
    out_ref[rows, :] = last if length == LANES else jnp.concatenate([shifted[:, :length - LANES], last], axis=1)


def _sample_stages(slopes_ref, sinks_ref, qa_ref, kva_ref, b0_ref, b12_ref,
                   ta_ref, tb0_ref, tb1_ref, tb2_ref, ca_ref, cb0_ref, cb1_ref, cb2_ref,
                   oa_ref, ob_ref, lse_ref, na_ref, nb0_ref, nb1_ref, nb2_ref):
    n = pl.program_id(0)
    row = pl.ds(n, 1)
    sub = lax.broadcasted_iota(jnp.int32, (8, LANES), 0)
    lane = lax.broadcasted_iota(jnp.int32, (8, LANES), 1)
    half = lane // HEAD_DIM
    kv_head = sub // A_GROUP
    sub_b = lax.broadcasted_iota(jnp.int32, (8, B_GW), 0)
    head_mask = (lax.broadcasted_iota(jnp.int32, (8, B_GW), 1) // HEAD_DIM) == sub_b
    b_refs = ((cb0_ref, tb0_ref, nb0_ref), (cb1_ref, tb1_ref, nb1_ref), (cb2_ref, tb2_ref, nb2_ref))
    state = {}

    def scores():
        q_row = qa_ref[row, :]
        q_blk = jnp.zeros((8, LANES), F32)
        for r in range(A_Q_HEADS):
            k, g = divmod(r, A_GROUP)
            chunk = q_row[:, (r // 2) * LANES:(r // 2 + 1) * LANES]
            if g % 2 != k:
                chunk = pltpu.roll(chunk, HEAD_DIM, axis=1)
            q_blk = jnp.where((sub == r) & (half == k), jnp.broadcast_to(chunk, (8, LANES)), q_blk)
        kva = kva_ref[row, :]
        k_new = jnp.where(lane[:1] < HEAD_DIM, kva[:, 0:LANES], kva[:, LANES:2 * LANES])
        v_new = jnp.where(lane[:1] < HEAD_DIM, kva[:, 2 * LANES:3 * LANES], kva[:, 3 * LANES:])
        s = _dot(q_blk.astype(BF16), ca_ref[0:A_KVW, :].astype(BF16))
        state["a"] = (s, jnp.sum(q_blk * k_new, axis=-1, keepdims=True), v_new)
        for g, (c_ref, _, _) in enumerate(b_refs):
            qkv = b0_ref[row, :] if g == 0 else b12_ref[row, (g - 1) * 3 * B_GW:g * 3 * B_GW]
            q_row, k_new, v_new = qkv[:, 0:B_GW], qkv[:, B_GW:2 * B_GW], qkv[:, 2 * B_GW:]
            q_blk = jnp.where(head_mask, jnp.broadcast_to(q_row, (8, B_GW)), 0.0)
            q_col = jnp.broadcast_to(q_row, (8, B_GW)).T[:, 0:1]
            prod = (c_ref[0:B_GW, :] * q_col).reshape(B_HEADS_PER_GROUP, HEAD_DIM, c_ref.shape[1])
            s4 = jnp.sum(prod, axis=1)
            s = jnp.concatenate([s4, jnp.zeros_like(s4)], axis=0)
            state["b", g] = (s, jnp.sum(q_blk * k_new, axis=-1, keepdims=True), v_new)

    def softmaxes():
        s, s_new, v_new = state["a"]
        slope = _rows_where(sub[:, :1], [slopes_ref[r] * LOG2E for r in range(A_Q_HEADS)])
        sink = _rows_where(sub[:, :1], [sinks_ref[r] * LOG2E for r in range(A_Q_HEADS)])
        s = s - slope * (BLOCK - lane).astype(F32)
        m = jnp.maximum(jnp.maximum(jnp.max(s, axis=-1, keepdims=True), s_new), sink)
        e = jnp.exp2(s - m)
        e_new = jnp.exp2(s_new - m)
        denom = jnp.sum(e, axis=-1, keepdims=True) + e_new + jnp.exp2(sink - m)
        state["a"] = (e.astype(BF16), e_new, v_new, m, denom)
        for g, (length, dil) in enumerate(B_PATTERNS):
            s, s_new, v_new = state["b", g]
            head0 = A_Q_HEADS + g * B_HEADS_PER_GROUP
            sub_l = lax.broadcasted_iota(jnp.int32, (8, length), 0)
            col = lax.broadcasted_iota(jnp.int32, (8, length), 1)
            slope = _rows_where(sub_l, [slopes_ref[head0 + h] * LOG2E for h in range(B_HEADS_PER_GROUP)])
            s = jnp.where((col & (dil - 1)) == 0, s - slope * (length - col).astype(F32), NEG)
            m = jnp.maximum(jnp.max(s, axis=-1, keepdims=True), s_new)
            e = jnp.exp2(s - m)
            e_new = jnp.exp2(s_new - m)
            denom = jnp.sum(e, axis=-1, keepdims=True) + e_new
            state["b", g] = (e, e_new, v_new, m, denom)

    def values():
        e, e_new, v_new, m, denom = state["a"]
        acc = _dot_nt(e, ca_ref[A_KVW:2 * A_KVW, :].astype(BF16))
        out = jnp.where(half == kv_head, (acc + e_new * v_new) / denom, 0.0)
        for c in range(A_W // LANES):
            k = c // 2
            pieces = []
            for hh in range(2):
                r = k * A_GROUP + 2 * (c % 2) + hh
                piece = out[r:r + 1, :]
                pieces.append(piece if hh == k else pltpu.roll(piece, HEAD_DIM, axis=1))
            oa_ref[:, c * LANES:(c + 1) * LANES] = jnp.where(lane[:1] < HEAD_DIM, pieces[0], pieces[1])
        for g, (c_ref, _, _) in enumerate(b_refs):
            e, e_new, v_new, m, denom = state["b", g]
            length = c_ref.shape[1]
            weights = jnp.broadcast_to(e[0:B_HEADS_PER_GROUP, None, :], (B_HEADS_PER_GROUP, HEAD_DIM, length))
            acc_col = jnp.sum(c_ref[B_GW:2 * B_GW, :] * weights.reshape(B_GW, length), axis=1, keepdims=True)
            acc = jnp.broadcast_to(jnp.broadcast_to(acc_col, (B_GW, 8)).T[0:1, :], (8, B_GW))
            out = jnp.where(head_mask, (acc + e_new * v_new) / denom, 0.0)
            ob_ref[g] = jnp.sum(out, axis=0, keepdims=True)
            lse = jnp.where(head_mask, m + jnp.log2(denom), 0.0)
            lse_ref[g] = jnp.sum(lse, axis=0, keepdims=True)

    def shift(c_ref, t_ref, n_ref, length, piece, n_pieces):
        rows_per = c_ref.shape[0] // n_pieces
        return lambda: _shift_cache(c_ref, t_ref, n_ref, n, length, slice(piece * rows_per, (piece + 1) * rows_per))

    def shift_small():
        shift(ca_ref, ta_ref, na_ref, BLOCK, 0, 1)()
        shift(cb0_ref, tb0_ref, nb0_ref, B_PATTERNS[0][0], 0, 1)()

    shift_b1 = shift(cb1_ref, tb1_ref, nb1_ref, B_PATTERNS[1][0], 0, 1)
    shift_b2 = [shift(cb2_ref, tb2_ref, nb2_ref, B_PATTERNS[2][0], piece, _N_SHIFT_PIECES)
                for piece in range(_N_SHIFT_PIECES)]
    return [scores, shift_small, softmaxes, shift_b1, values] + shift_b2


_N_SHIFT_PIECES = 4


_N_FFN_INPUTS = 5
_N_SAMPLE_INPUTS = 14
_N_SAMPLE_OUTPUTS = 7


def _ffn_sample_kernel(*refs):
    ffn_in, refs = refs[:_N_FFN_INPUTS], refs[_N_FFN_INPUTS:]
    sample_in, refs = refs[:_N_SAMPLE_INPUTS], refs[_N_SAMPLE_INPUTS:]
    o_ref, sample_out, (act_ref,) = refs[0], refs[1:1 + _N_SAMPLE_OUTPUTS], refs[1 + _N_SAMPLE_OUTPUTS:]
    x_ref, gain_ref, wg_ref, wu_ref, wd_ref = ffn_in
    o_ref[...] = _swiglu_residual(x_ref[...], gain_ref[...], wg_ref, wu_ref, wd_ref, act_ref,
                                  side_work=_sample_stages(*sample_in, *sample_out))


def _ffn_and_sample(x, gain, wg, wu, wd, slopes, sinks, qa, kva, b0, b12, tails, caches):
    n_seq = qa.shape[0]
    t = x.shape[0]
    assert t % n_seq == 0
    tm = t // n_seq
    full = lambda a: pl.BlockSpec(a.shape, lambda n: (0,) * a.ndim)
    per_seq = lambda a: pl.BlockSpec((None,) + a.shape[1:], lambda n: (n, 0, 0))
    small = [qa, kva, b0, b12] + list(tails)
    assert 2 + len(small) + len(caches) == _N_SAMPLE_INPUTS
    out_shape = [jax.ShapeDtypeStruct((t, D_MODEL), F32),
                 jax.ShapeDtypeStruct((n_seq, 1, A_W), F32),
                 jax.ShapeDtypeStruct((B_N_GROUPS, n_seq, 1, B_GW), F32),
                 jax.ShapeDtypeStruct((B_N_GROUPS, n_seq, 1, B_GW), F32)]
    out_shape += [jax.ShapeDtypeStruct(c.shape, F32) for c in caches]
    out_specs = [pl.BlockSpec((tm, D_MODEL), lambda n: (n, 0)),
                 pl.BlockSpec((None, 1, A_W), lambda n: (n, 0, 0)),
                 pl.BlockSpec((B_N_GROUPS, None, 1, B_GW), lambda n: (0, n, 0, 0)),
                 pl.BlockSpec((B_N_GROUPS, None, 1, B_GW), lambda n: (0, n, 0, 0))]
    out_specs += [per_seq(c) for c in caches]
    x1, oa, ob, lse, *new_caches = pl.pallas_call(
        _ffn_sample_kernel,
        grid=(n_seq,),
        in_specs=[pl.BlockSpec((tm, D_MODEL), lambda n: (n, 0)),
                  _const_spec((1, D_MODEL)), _const_spec((D_MODEL, D_FF)),
                  _const_spec((D_MODEL, D_FF)), _const_spec((D_FF, D_MODEL))]
                 + [_smem_spec(), _smem_spec()] + [full(a) for a in small] + [per_seq(c) for c in caches],
        out_specs=out_specs,
        out_shape=out_shape,
        scratch_shapes=[pltpu.VMEM((tm, D_FF), BF16)],
        compiler_params=_params(1),
        name="ffn1_sample",
    )(x, gain, wg, wu, wd, slopes, sinks, *small, *caches)
    return (x1, oa.reshape(n_seq, A_W), ob.reshape(B_N_GROUPS, n_seq, B_GW),
            lse.reshape(B_N_GROUPS, n_seq, B_GW), *new_caches)


_N_A_PAIRS = A_W // LANES
_N_B_PAIRS = B_GW // LANES


def _merge_kernel(*refs):
    refs = list(refs)
    take = lambda count: [refs.pop(0) for _ in range(count)]
    (x_ref,), oa_refs, da_refs, ma_refs = take(1), take(_N_A_PAIRS), take(_N_A_PAIRS), take(_N_A_PAIRS)
    n_b = B_N_GROUPS * _N_B_PAIRS
    ob_refs, db_refs, mb_refs = take(n_b), take(n_b), take(n_b)
    sinks_ref, gates_ref, wua_ref, wub_ref, wo_ref, gain_ref, wg_ref, wu_ref, wd_ref, y_ref, act_ref = refs

    unswap = lambda ref: pltpu.roll(ref[...], HEAD_DIM, axis=1)
    oa_pairs = []
    for p, (o_ref, d_ref, m_ref) in enumerate(zip(oa_refs, da_refs, ma_refs)):
        denom = unswap(d_ref) + jnp.exp2(sinks_ref[:, p * LANES:(p + 1) * LANES] - m_ref[...])
        oa_pairs.append((o_ref[...].astype(F32) / denom).astype(BF16))
    ob_pairs = []
    for p in range(_N_B_PAIRS):
        idx = [g * _N_B_PAIRS + p for g in range(B_N_GROUPS)]
        maxes = [mb_refs[i][...] for i in idx]
        top = functools.reduce(jnp.maximum, maxes)
        weights = [jnp.exp2(m - top) for m in maxes]
        num = sum(w * ob_refs[i][...] for w, i in zip(weights, idx))
        den = sum(w * unswap(db_refs[i]) for w, i in zip(weights, idx))
        ob_pairs.append((num / den).astype(BF16))
    ua = _dot(jnp.concatenate(oa_pairs, axis=1), wua_ref[...])
    ub = _dot(jnp.concatenate(ob_pairs, axis=1), wub_ref[...])
    gate_a = gates_ref[:, :D_MODEL].astype(F32)
    gate_b = gates_ref[:, D_MODEL:].astype(F32)
    mixed = (gate_a * ua + gate_b * ub).astype(BF16)
    x = x_ref[...] + _dot(mixed, wo_ref[...])
    y_ref[...] = _swiglu_residual(x, gain_ref[...], wg_ref, wu_ref, wd_ref, act_ref)


def _merge(x, mixer_a, mixer_b, sink_lanes, gates, wua, wub, wo, gain, wg, wu, wd, tm):
    t = x.shape[0]
    row_spec = lambda w: pl.BlockSpec((tm, w), lambda i: (i, 0))
    pair_inputs = [a for part in tuple(mixer_a) + tuple(mixer_b) for a in part]
    n_pair_inputs = len(pair_inputs)
    assert n_pair_inputs == 3 * _N_A_PAIRS + 3 * B_N_GROUPS * _N_B_PAIRS
    return pl.pallas_call(
        _merge_kernel,
        grid=(t // tm,),
        in_specs=[row_spec(D_MODEL)] + [row_spec(LANES)] * n_pair_inputs + [_const_spec((1, A_W)),
                  row_spec(2 * D_MODEL), _const_spec((A_W, D_MODEL)), _const_spec((B_GW, D_MODEL)),
                  _const_spec((D_MODEL, D_MODEL)),
                  _const_spec((1, D_MODEL)), _const_spec((D_MODEL, D_FF)), _const_spec((D_MODEL, D_FF)),
                  _const_spec((D_FF, D_MODEL))],
        out_specs=row_spec(D_MODEL),
        out_shape=jax.ShapeDtypeStruct((t, D_MODEL), F32),
        scratch_shapes=[pltpu.VMEM((tm, D_FF), BF16)],
        compiler_params=_params(1),
        name="merge_ffn2",
    )(x, *pair_inputs, sink_lanes, gates, wua, wub, wo, gain, wg, wu, wd)


def _cache_view(cache):
    _, n, length, two, h, d = cache.shape
    return jnp.transpose(cache, (0, 1, 3, 4, 5, 2)).reshape(n, two * h * d, length)


def _state_view(rows_by_len, heads):
    n, _, length = rows_by_len.shape
    return jnp.transpose(rows_by_len.reshape(1, n, 2, heads, HEAD_DIM, length), (0, 1, 5, 2, 3, 4))


def kernel(x_prompt, x_sample, cache_a_kv, cache_b1_kv, cache_b2_kv, cache_b3_kv, norm_ffn1, w1_gate, w1_up,
           w1_down, norm_mix, w_in, q_norm_a, k_norm_a, q_norm_b, k_norm_b, sinks_a, w_up_a, w_up_b, w_o,
           norm_ffn2, w2_gate, w2_up, w2_down):
    assert x_prompt.shape[-1] == D_MODEL and w_in.shape == (1, D_MODEL, IN_W)
    batch, seq, _ = x_prompt.shape
    dec = x_sample.shape[0]
    assert x_sample.shape[1] == 1 and seq % (BLOCK * B_PATTERNS[-1][1]) == 0
    assert cache_a_kv.shape[2] == BLOCK
    assert all(c.shape[2] == w for c, (w, _) in zip((cache_b1_kv, cache_b2_kv, cache_b3_kv), B_PATTERNS))

    bf = lambda w: w[0].astype(BF16)
    wg1, wu1, wd1, w_in_b = map(bf, (w1_gate, w1_up, w1_down, w_in))
    late_weights = [w[0] for w in (w_up_a, w_up_b, w_o, w2_gate, w2_up, w2_down)]

    i = jnp.arange(1, N_ALIBI_HEADS + 1, dtype=F32)
    slopes = jnp.exp2(-8.0 * i / N_ALIBI_HEADS)
    sinks = sinks_a[0].reshape(A_Q_HEADS).astype(F32)
    ones64 = jnp.ones((HEAD_DIM,), F32)
    q_scale = ATTN_SCALE * LOG2E
    qkgain = jnp.concatenate([
        jnp.tile(q_norm_a[0] * q_scale, A_Q_HEADS), jnp.tile(k_norm_a[0], A_KV_HEADS),
        jnp.tile(ones64, A_KV_HEADS),
        jnp.tile(q_norm_b[0] * q_scale, B_N_GROUPS * B_HEADS_PER_GROUP),
        jnp.tile(k_norm_b[0], B_N_GROUPS * B_HEADS_PER_GROUP),
        jnp.tile(ones64, B_N_GROUPS * B_HEADS_PER_GROUP)]).reshape(1, QKV_W).astype(F32)
    head_of = jnp.arange(CHUNK) // HEAD_DIM
    ones_bd = (head_of[:, None] == head_of[None, :]).astype(BF16)

    def proj(x1, n_seq, s, tm, riders=()):
        return _proj(x1, norm_mix, w_in_b, qkgain, ones_bd, n_seq, s, tm, riders)

    x1s = _ffn(x_sample.reshape(dec, D_MODEL), norm_ffn1, wg1, wu1, wd1, dec)
    (qa_s, kva_s, b0_s, b12_s, gates_s, *tails_s), _ = proj(x1s, 1, dec, dec)
    caches = [_cache_view(c) for c in (cache_a_kv, cache_b1_kv, cache_b2_kv, cache_b3_kv)]
    x1p, oa_s, ob_s, lse_s, na, nb0, nb1, nb2 = _ffn_and_sample(
        x_prompt.reshape(batch * seq, D_MODEL), norm_ffn1, wg1, wu1, wd1,
        slopes, sinks, qa_s.astype(F32), kva_s.astype(F32), b0_s.astype(F32), b12_s,
        [t[0] for t in tails_s], caches)

    (qa, kva, b0, b12, gates, ta, tb0, tb1, tb2), (wua, wub, wo, wg2, wu2, wd2) = proj(
        x1p, batch, seq, 512, late_weights)

    def back(x1, mixer_a, mixer_b, sink_lanes, gates, tm):
        return _merge(x1, mixer_a, mixer_b, sink_lanes, gates, wua, wub, wo, norm_ffn2, wg2, wu2, wd2, tm)

    common = dict(n_seq=batch, seq=seq)
    mixer_a = _attn(slopes, qa, kva, kva, (0, 1, 2, 3), (0, 1), (2, 3), k_pair=(0, 0, 1, 1), dil=1,
                    head0=0, out_dtype=BF16, name="attn_a", **common)
    mixer_b = ([], [], [])
    for g, (_, dil) in enumerate(B_PATTERNS):
        src = b0 if g == 0 else b12
        first = 0 if g == 0 else (g - 1) * 3 * _N_B_PAIRS
        parts = _attn(slopes, src, src, src, (first, first + 1), (first + 2, first + 3), (first + 4, first + 5),
                      k_pair=(0, 1), dil=dil, head0=A_Q_HEADS + g * B_HEADS_PER_GROUP, out_dtype=F32,
                      name=f"attn_b{g}", **common)
        for acc, part in zip(mixer_b, parts):
            acc += part
    sink_lanes = jnp.repeat(sinks * LOG2E, HEAD_DIM).reshape(1, A_W)
    y_prompt = back(x1p, mixer_a, mixer_b, sink_lanes, gates, 512).reshape(batch, seq, D_MODEL)

    pairs = lambda a: [a[..., p * LANES:(p + 1) * LANES] for p in range(a.shape[-1] // LANES)]
    one, zero = jnp.ones((dec, LANES), F32), jnp.zeros((dec, LANES), F32)
    ob_s = [pair for g in range(B_N_GROUPS) for pair in pairs(ob_s[g])]
    lse_s = [pair for g in range(B_N_GROUPS) for pair in pairs(lse_s[g])]
    y_sample = back(x1s, (pairs(oa_s.astype(BF16)), [one] * _N_A_PAIRS, [zero] * _N_A_PAIRS),
                    (ob_s, [one] * len(ob_s), lse_s), jnp.full((1, A_W), NEG, F32), gates_s,
                    dec).reshape(dec, 1, D_MODEL)

    return (y_prompt, y_sample,
            _state_view(ta, A_KV_HEADS), _state_view(tb0, B_HEADS_PER_GROUP),
            _state_view(tb1, B_HEADS_PER_GROUP), _state_view(tb2, B_HEADS_PER_GROUP),
            _state_view(na, A_KV_HEADS), _state_view(nb0, B_HEADS_PER_GROUP),
            _state_view(nb1, B_HEADS_PER_GROUP), _state_view(nb2, B_HEADS_PER_GROUP))
```

```python
import functools
import math

import jax
import jax.numpy as jnp
from jax import lax
from jax.experimental import pallas as pl
from jax.experimental.pallas import tpu as pltpu

D_MODEL = 1024
D_FF = 2816
HEAD_DIM = 64
A_Q_HEADS = 8
A_KV_HEADS = 2
A_GROUP = A_Q_HEADS // A_KV_HEADS
B_PATTERNS = ((128, 1), (512, 4), (2048, 16))
B_HEADS_PER_GROUP = 4
B_N_GROUPS = 3
N_ALIBI_HEADS = A_Q_HEADS + B_N_GROUPS * B_HEADS_PER_GROUP
BLOCK = 128
EPS = 1e-6
ATTN_SCALE = HEAD_DIM ** -0.5
LOG2E = math.log2(math.e)
A_W = A_Q_HEADS * HEAD_DIM
A_KVW = A_KV_HEADS * HEAD_DIM
B_GW = B_HEADS_PER_GROUP * HEAD_DIM
B_W = B_N_GROUPS * B_GW
QKV_W = A_W + 2 * A_KVW + 3 * B_W
IN_W = QKV_W + 2 * D_MODEL

LANES = 128
CHUNK = 256
NEG = -1e30
VMEM_LIMIT = 56 * 1024 * 1024

F32 = jnp.float32
BF16 = jnp.bfloat16


def _const_spec(shape):
    nd = len(shape)
    return pl.BlockSpec(shape, lambda *_: (0,) * nd, pipeline_mode=pl.Buffered(1))


def _smem_spec():
    return pl.BlockSpec(memory_space=pltpu.SMEM)


def _params(n_axes):
    return pltpu.CompilerParams(
        dimension_semantics=("arbitrary",) * n_axes, vmem_limit_bytes=VMEM_LIMIT)


def _dot(a, b):
    return jnp.dot(a, b, preferred_element_type=F32)


def _dot_nt(a, b):
    return lax.dot_general(a, b, (((1,), (1,)), ((), ())), preferred_element_type=F32)


def _swiglu_residual(x, gain, wg_ref, wu_ref, wd_ref, act_ref, side_work=()):
    n_chunks = D_FF // CHUNK
    assert len(side_work) <= n_chunks
    h = (x * gain).astype(BF16)
    inv_rms = lax.rsqrt(jnp.mean(x * x, axis=-1, keepdims=True) + EPS)
    for c in range(n_chunks):
        if c < len(side_work):
            side_work[c]()
        sl = slice(c * CHUNK, (c + 1) * CHUNK)
        g = _dot(h, wg_ref[:, sl]) * inv_rms
        u = _dot(h, wu_ref[:, sl]) * inv_rms
        act_ref[:, sl] = (g * jax.nn.sigmoid(g) * u).astype(BF16)
    return x + 0.5 * _dot(act_ref[...], wd_ref[...])


_WIN_CAST_BLOCKS = 8


def _ffn_cast_kernel(x_ref, gain_ref, wg_ref, wu_ref, wd_ref, win_ref,
                     y_ref, wg_out, wu_out, wd_out, win_out, h_ref, inv_rms_ref, acc_ref):
    c = pl.program_id(0)

    @pl.when(c == 0)
    def _():
        x = x_ref[...]
        h_ref[...] = (x * gain_ref[...]).astype(BF16)
        inv_rms = lax.rsqrt(jnp.mean(x * x, axis=-1, keepdims=True) + EPS)
        inv_rms_ref[...] = jnp.broadcast_to(inv_rms, inv_rms_ref.shape)
        acc_ref[...] = jnp.zeros(acc_ref.shape, F32)

    wg, wu, wd = wg_ref[...].astype(BF16), wu_ref[...].astype(BF16), wd_ref[...].astype(BF16)
    wg_out[...], wu_out[...], wd_out[...] = wg, wu, wd
    win_out[...] = win_ref[...].astype(BF16)
    inv_rms = inv_rms_ref[:, 0:1]
    g = _dot(h_ref[...], wg) * inv_rms
    u = _dot(h_ref[...], wu) * inv_rms
    acc_ref[...] += _dot((g * jax.nn.sigmoid(g) * u).astype(BF16), wd)

    @pl.when(c == pl.num_programs(0) - 1)
    def _():
        y_ref[...] = x_ref[...] + 0.5 * acc_ref[...]


def _ffn_cast(x, gain, wg, wu, wd, w_in):
    t = x.shape[0]
    n_chunks = D_FF // CHUNK
    assert n_chunks >= _WIN_CAST_BLOCKS and D_MODEL % (_WIN_CAST_BLOCKS * _BF16_ROWS) == 0
    col_spec = lambda: pl.BlockSpec((D_MODEL, CHUNK), lambda c: (0, c))
    row_spec = lambda: pl.BlockSpec((CHUNK, D_MODEL), lambda c: (c, 0))
    win_spec = lambda: pl.BlockSpec((D_MODEL // _WIN_CAST_BLOCKS, IN_W),
                                    lambda c: (jnp.minimum(c, _WIN_CAST_BLOCKS - 1), 0))
    return pl.pallas_call(
        _ffn_cast_kernel,
        grid=(n_chunks,),
        in_specs=[_const_spec((t, D_MODEL)), _const_spec((1, D_MODEL)), col_spec(), col_spec(), row_spec(),
                  win_spec()],
        out_specs=[pl.BlockSpec((t, D_MODEL), lambda c: (0, 0)), col_spec(), col_spec(), row_spec(), win_spec()],
        out_shape=[jax.ShapeDtypeStruct((t, D_MODEL), F32), jax.ShapeDtypeStruct(wg.shape, BF16),
                   jax.ShapeDtypeStruct(wu.shape, BF16), jax.ShapeDtypeStruct(wd.shape, BF16),
                   jax.ShapeDtypeStruct(w_in.shape, BF16)],
        scratch_shapes=[pltpu.VMEM((t, D_MODEL), BF16), pltpu.VMEM((t, LANES), F32), pltpu.VMEM((t, D_MODEL), F32)],
        compiler_params=_params(1),
        name="ffn1_cast",
    )(x, gain, wg, wu, wd, w_in)


_N_QKV_CHUNKS = QKV_W // CHUNK
_N_CHUNKS = IN_W // CHUNK
_LAST_CHUNK = 9


def _tail_plan(tail, tm, n_tiles):
    if tail >= tm:
        return tm, n_tiles - tail // tm
    return tail, n_tiles - 1


_N_PROJ_INPUTS = 5
_N_PROJ_OUTPUTS = 9
_BF16_ROWS = 16


def _proj_kernel(*refs, tm, n_tiles, tails):
    n_riders = (len(refs) - _N_PROJ_INPUTS - _N_PROJ_OUTPUTS) // 2
    x_ref, gain_ref, w_ref, qkgain_ref, ones_ref = refs[:_N_PROJ_INPUTS]
    rider_in = refs[_N_PROJ_INPUTS:_N_PROJ_INPUTS + n_riders]
    outs = refs[_N_PROJ_INPUTS + n_riders:]
    qa_ref, kva_ref, b0_ref, b12_ref, gates_ref, ta_ref, tb0_ref, tb1_ref, tb2_ref = outs[:_N_PROJ_OUTPUTS]
    for src, dst in zip(rider_in, outs[_N_PROJ_OUTPUTS:]):
        dst[...] = src[...].astype(BF16)
    x = x_ref[...]
    h = (x * gain_ref[...]).astype(BF16)
    inv_rms = lax.rsqrt(jnp.mean(x * x, axis=-1, keepdims=True) + EPS)
    lane = lax.broadcasted_iota(jnp.int32, (tm, CHUNK), 1)
    lane1 = lax.broadcasted_iota(jnp.int32, (tm, LANES), 1)

    def z_chunk(c):
        return _dot(h, w_ref[:, c * CHUNK:(c + 1) * CHUNK]) * inv_rms

    def head_norm(z, c):
        ss = _dot((z * z).astype(BF16), ones_ref[...])
        return z * lax.rsqrt(ss * (1.0 / HEAD_DIM) + EPS) * qkgain_ref[:, c * CHUNK:(c + 1) * CHUNK]

    def write_tail(t_ref, row0, y, tail):
        block_w, _ = _tail_plan(tail, tm, n_tiles)
        data = y if block_w == tm else y[tm - block_w:, :]
        t_ref[row0:row0 + CHUNK, :] = data.T

    tb_refs = (tb0_ref, tb1_ref, tb2_ref)

    def consume(c, z):
        if c < 2:
            qa_ref[:, c * CHUNK:(c + 1) * CHUNK] = head_norm(z, c).astype(BF16)
        elif c == 2:
            y = jnp.where(lane < A_KVW, head_norm(z, c), z)
            write_tail(ta_ref, 0, y, tails[0])
            for part in range(2):
                pair = y[:, part * LANES:(part + 1) * LANES]
                swapped = pltpu.roll(pair, HEAD_DIM, axis=1)
                base = part * 2 * LANES
                kva_ref[:, base:base + LANES] = jnp.where(lane1 < HEAD_DIM, pair, swapped).astype(BF16)
                kva_ref[:, base + LANES:base + 2 * LANES] = jnp.where(lane1 < HEAD_DIM, swapped, pair).astype(BF16)
        elif c < _N_QKV_CHUNKS:
            kind, g = divmod(c - 3, B_N_GROUPS)
            y = z if kind == 2 else head_norm(z, c)
            if g == 0:
                b0_ref[:, kind * CHUNK:(kind + 1) * CHUNK] = y.astype(BF16)
            else:
                col = ((g - 1) * 3 + kind) * CHUNK
                b12_ref[:, col:col + CHUNK] = y
            if kind > 0:
                write_tail(tb_refs[g], (kind - 1) * CHUNK, y, tails[1 + g])
        else:
            col = (c - _N_QKV_CHUNKS) * CHUNK
            gates_ref[:, col:col + CHUNK] = jax.nn.sigmoid(z).astype(BF16)

    order = list(range(_N_QKV_CHUNKS, _N_CHUNKS)) + [c for c in range(_N_QKV_CHUNKS) if c != _LAST_CHUNK]
    order.append(_LAST_CHUNK)
    z_next = z_chunk(order[0])
    for i, c in enumerate(order):
        z = z_next
        if i + 1 < len(order):
            z_next = z_chunk(order[i + 1])
        consume(c, z)


def _rider_specs(weights, n_steps, n_tiles):
    specs = []
    for w in weights:
        rows, cols = w.shape
        n_blocks = n_steps
        while rows % (n_blocks * _BF16_ROWS):
            assert n_blocks % 2 == 0, (rows, n_steps)
            n_blocks //= 2
        per = n_steps // n_blocks
        specs.append(pl.BlockSpec((rows // n_blocks, cols), lambda n, j, per=per: ((n * n_tiles + j) // per, 0)))
    return specs


def _proj(x, gain, w_in, qkgain, ones_bd, n_seq, seq, tm, riders=()):
    t = n_seq * seq
    n_tiles = seq // tm
    rider_specs = _rider_specs(riders, n_seq * n_tiles, n_tiles)
    tails = (min(128, seq),) + tuple(min(w, seq) for w, _ in B_PATTERNS)
    tail_rows = (2 * A_KVW, 2 * B_GW, 2 * B_GW, 2 * B_GW)

    def tail_spec(rows, tail):
        block_w, first = _tail_plan(tail, tm, n_tiles)
        return pl.BlockSpec((None, rows, block_w), lambda n, j: (n, 0, jnp.maximum(j - first, 0)))

    row_spec = lambda w: pl.BlockSpec((tm, w), lambda n, j: (n * n_tiles + j, 0))
    out_shape = [
        jax.ShapeDtypeStruct((t, A_W), BF16),
        jax.ShapeDtypeStruct((t, 4 * LANES), BF16),
        jax.ShapeDtypeStruct((t, 3 * B_GW), BF16),
        jax.ShapeDtypeStruct((t, 6 * B_GW), F32),
        jax.ShapeDtypeStruct((t, 2 * D_MODEL), BF16),
    ] + [jax.ShapeDtypeStruct((n_seq, r, tl), F32) for r, tl in zip(tail_rows, tails)]
    out_specs = [row_spec(A_W), row_spec(4 * LANES), row_spec(3 * B_GW), row_spec(6 * B_GW),
                 row_spec(2 * D_MODEL)] + [tail_spec(r, tl) for r, tl in zip(tail_rows, tails)]
    assert len(out_shape) == _N_PROJ_OUTPUTS
    out_shape += [jax.ShapeDtypeStruct(w.shape, BF16) for w in riders]
    outs = pl.pallas_call(
        functools.partial(_proj_kernel, tm=tm, n_tiles=n_tiles, tails=tails),
        grid=(n_seq, n_tiles),
        in_specs=[row_spec(D_MODEL), _const_spec((1, D_MODEL)), _const_spec((D_MODEL, IN_W)),
                  _const_spec((1, QKV_W)), _const_spec((CHUNK, CHUNK))] + rider_specs,
        out_specs=out_specs + _rider_specs(riders, n_seq * n_tiles, n_tiles),
        out_shape=out_shape,
        compiler_params=_params(2),
        name="proj",
    )(x, gain, w_in, qkgain, ones_bd, *riders)
    return outs[:_N_PROJ_OUTPUTS], outs[_N_PROJ_OUTPUTS:]


_HEADS_PER_ITER = 32


def _attn_kernel(*refs, n_pairs, k_pair, dil, seq, head0):
    refs = list(refs)
    slopes_ref = refs.pop(0)
    n_kv = max(k_pair) + 1
    take = lambda count: [refs.pop(0) for _ in range(count)]
    q_refs, k_refs, v_refs = take(n_pairs), take(n_kv), take(n_kv)
    o_refs, den_refs, max_refs = take(n_pairs), take(n_pairs), take(n_pairs)
    (bias_ref,) = refs

    n_blocks = seq // (BLOCK * dil)
    use_prev = n_blocks > 1
    n_heads = 2 * n_pairs
    qi = lax.broadcasted_iota(jnp.int32, (BLOCK, BLOCK), 0)
    ki = lax.broadcasted_iota(jnp.int32, (BLOCK, BLOCK), 1)
    d_cur = qi - ki
    lo = ki < HEAD_DIM
    lo_keys = lax.broadcasted_iota(jnp.int32, ((2 if use_prev else 1) * BLOCK, LANES), 1) < HEAD_DIM

    for h in range(n_heads):
        slope = slopes_ref[head0 + h] * (float(dil) * LOG2E)
        cur_bias = jnp.where(d_cur >= 0, -slope * d_cur.astype(F32), NEG)
        if use_prev:
            prev_bias = jnp.where(d_cur <= 0, -slope * (d_cur + BLOCK).astype(F32), NEG)
            bias_ref[h] = jnp.concatenate([prev_bias, cur_bias], axis=1)
            bias_ref[n_heads + h] = jnp.concatenate([jnp.full_like(prev_bias, NEG), cur_bias], axis=1)
        else:
            bias_ref[h] = cur_bias

    def block_rows(step):
        if n_blocks == 1:
            r, j = step, 0
        else:
            r, j = step // n_blocks, step % n_blocks
        if dil == 1:
            cur = pl.ds(pl.multiple_of(j * BLOCK, BLOCK), BLOCK)
            prev = pl.ds(pl.multiple_of(jnp.maximum(j - 1, 0) * BLOCK, BLOCK), BLOCK)
        else:
            cur = pl.ds(r + j * (BLOCK * dil), BLOCK, stride=dil)
            prev = pl.ds(r + jnp.maximum(j - 1, 0) * (BLOCK * dil), BLOCK, stride=dil)
        table = jnp.where(j == 0, n_heads, 0) if use_prev else 0
        return cur, prev, table

    blocks_per_iter = _HEADS_PER_ITER // (2 * n_pairs)
    assert (dil * n_blocks) % blocks_per_iter == 0

    def body(it, carry):
        blocks = [block_rows(it * blocks_per_iter + b) for b in range(blocks_per_iter)]
        scores, values = {}, {}
        for b, (cur, prev, _) in enumerate(blocks):
            for p in range(n_pairs):
                k_ref, v_ref = k_refs[k_pair[p]], v_refs[k_pair[p]]
                q = q_refs[p][cur, :].astype(BF16)
                keys = k_ref[cur, :].astype(BF16)
                vals = v_ref[cur, :].astype(BF16)
                if use_prev:
                    keys = jnp.concatenate([k_ref[prev, :].astype(BF16), keys], axis=0)
                    vals = jnp.concatenate([v_ref[prev, :].astype(BF16), vals], axis=0)
                one = jnp.ones_like(vals)
                values[b, p, 0] = jnp.where(lo_keys, vals, one)
                values[b, p, 1] = jnp.where(lo_keys, one, vals)
                for half in range(2):
                    qm = jnp.where(lo if half == 0 else jnp.logical_not(lo), q, jnp.zeros_like(q))
                    scores[b, p, half] = _dot_nt(qm, keys)
        probs, row_max = {}, {}
        for (b, p, half), s in scores.items():
            s = s + bias_ref[blocks[b][2] + 2 * p + half]
            m = jnp.max(s, axis=-1, keepdims=True)
            probs[b, p, half], row_max[b, p, half] = jnp.exp2(s - m).astype(BF16), m
        for b, (cur, _, _) in enumerate(blocks):
            for p in range(n_pairs):
                acc = [_dot(probs[b, p, half], values[b, p, half]) for half in range(2)]
                o_refs[p][cur, :] = jnp.where(lo, acc[0], acc[1]).astype(o_refs[p].dtype)
                den_refs[p][cur, :] = jnp.where(lo, acc[1], acc[0])
                max_refs[p][cur, :] = jnp.where(lo, row_max[b, p, 0], row_max[b, p, 1])
        return carry

    lax.fori_loop(0, dil * n_blocks // blocks_per_iter, body, 0)


def _attn(slopes, q_arr, k_arr, v_arr, q_idx, k_idx, v_idx, *, n_seq, seq, k_pair, dil, head0, out_dtype, name):
    n_pairs = len(q_idx)

    def pair_spec(idx):
        return pl.BlockSpec((None, seq, LANES), lambda n: (n, 0, idx))

    view = lambda a: a.reshape(n_seq, seq, a.shape[-1])
    operands = [slopes] + [view(q_arr)] * n_pairs + [view(k_arr)] * len(k_idx) + [view(v_arr)] * len(v_idx)
    in_specs = [_smem_spec()] + [pair_spec(i) for i in tuple(q_idx) + tuple(k_idx) + tuple(v_idx)]
    out_shape = [jax.ShapeDtypeStruct((n_seq, seq, LANES), out_dtype)] * n_pairs
    out_shape += [jax.ShapeDtypeStruct((n_seq, seq, LANES), F32)] * (2 * n_pairs)
    outs = pl.pallas_call(
        functools.partial(_attn_kernel, n_pairs=n_pairs, k_pair=k_pair, dil=dil, seq=seq, head0=head0),
        grid=(n_seq,),
        in_specs=in_specs,
        out_specs=[pair_spec(0)] * len(out_shape),
        out_shape=out_shape,
        scratch_shapes=[pltpu.VMEM((2 * n_pairs, BLOCK, BLOCK) if seq == BLOCK * dil else
                                   (4 * n_pairs, BLOCK, 2 * BLOCK), F32)],
        compiler_params=_params(1),
        name=name,
    )(*operands)
    outs = [o.reshape(n_seq * seq, LANES) for o in outs]
    return outs[:n_pairs], outs[n_pairs:2 * n_pairs], outs[2 * n_pairs:]


def _rows_where(row_iota, values):
    out = jnp.zeros(row_iota.shape, F32)
    for r, v in enumerate(values):
        out = jnp.where(row_iota == r, v, out)
    return out


def _shift_cache(cache_ref, tail_ref, out_ref, n, length, rows):
    shifted = pltpu.roll(cache_ref[rows, :], length - 1, axis=1)
    new_col = pltpu.roll(tail_ref[rows, :], (LANES - 1) - n, axis=1)
    lane = lax.broadcasted_iota(jnp.int32, new_col.shape, 1)
    last = jnp.where(lane == LANES - 1, new_col, shifted[:, length - LANES:])
    out_ref[rows, :] = last if length == LANES else jnp.concatenate([shifted[:, :length - LANES], last], axis=1)


def _sample_stages(slopes_ref, sinks_ref, qa_ref, kva_ref, b0_ref, b12_ref,
                   ta_ref, tb0_ref, tb1_ref, tb2_ref, ca_ref, cb0_ref, cb1_ref, cb2_ref,
                   oa_ref, ob_ref, lse_ref, na_ref, nb0_ref, nb1_ref, nb2_ref):
    n = pl.program_id(0)
    row = pl.ds(n, 1)
    sub = lax.broadcasted_iota(jnp.int32, (8, LANES), 0)
    lane = lax.broadcasted_iota(jnp.int32, (8, LANES), 1)
    half = lane // HEAD_DIM
    kv_head = sub // A_GROUP
    sub_b = lax.broadcasted_iota(jnp.int32, (8, B_GW), 0)
    head_mask = (lax.broadcasted_iota(jnp.int32, (8, B_GW), 1) // HEAD_DIM) == sub_b
    b_refs = ((cb0_ref, tb0_ref, nb0_ref), (cb1_ref, tb1_ref, nb1_ref), (cb2_ref, tb2_ref, nb2_ref))
    state = {}

    def scores():
        q_row = qa_ref[row, :]
        q_blk = jnp.zeros((8, LANES), F32)
        for r in range(A_Q_HEADS):
            k, g = divmod(r, A_GROUP)
            chunk = q_row[:, (r // 2) * LANES:(r // 2 + 1) * LANES]
            if g % 2 != k:
                chunk = pltpu.roll(chunk, HEAD_DIM, axis=1)
            q_blk = jnp.where((sub == r) & (half == k), jnp.broadcast_to(chunk, (8, LANES)), q_blk)
        kva = kva_ref[row, :]
        k_new = jnp.where(lane[:1] < HEAD_DIM, kva[:, 0:LANES], kva[:, LANES:2 * LANES])
        v_new = jnp.where(lane[:1] < HEAD_DIM, kva[:, 2 * LANES:3 * LANES], kva[:, 3 * LANES:])
        s = _dot(q_blk.astype(BF16), ca_ref[0:A_KVW, :].astype(BF16))
        state["a"] = (s, jnp.sum(q_blk * k_new, axis=-1, keepdims=True), v_new)
        for g, (c_ref, _, _) in enumerate(b_refs):
            qkv = b0_ref[row, :] if g == 0 else b12_ref[row, (g - 1) * 3 * B_GW:g * 3 * B_GW]
            q_row, k_new, v_new = qkv[:, 0:B_GW], qkv[:, B_GW:2 * B_GW], qkv[:, 2 * B_GW:]
            q_blk = jnp.where(head_mask, jnp.broadcast_to(q_row, (8, B_GW)), 0.0)
            q_col = jnp.broadcast_to(q_row, (8, B_GW)).T[:, 0:1]
            prod = (c_ref[0:B_GW, :] * q_col).reshape(B_HEADS_PER_GROUP, HEAD_DIM, c_ref.shape[1])
            s4 = jnp.sum(prod, axis=1)
            s = jnp.concatenate([s4, jnp.zeros_like(s4)], axis=0)
            state["b", g] = (s, jnp.sum(q_blk * k_new, axis=-1, keepdims=True), v_new)

    def softmaxes():
        s, s_new, v_new = state["a"]
        slope = _rows_where(sub[:, :1], [slopes_ref[r] * LOG2E for r in range(A_Q_HEADS)])
        sink = _rows_where(sub[:, :1], [sinks_ref[r] * LOG2E for r in range(A_Q_HEADS)])
        s = s - slope * (BLOCK - lane).astype(F32)
        m = jnp.maximum(jnp.maximum(jnp.max(s, axis=-1, keepdims=True), s_new), sink)
        e = jnp.exp2(s - m)
        e_new = jnp.exp2(s_new - m)
        denom = jnp.sum(e, axis=-1, keepdims=True) + e_new + jnp.exp2(sink - m)
        state["a"] = (e.astype(BF16), e_new, v_new, m, denom)
        for g, (length, dil) in enumerate(B_PATTERNS):
            s, s_new, v_new = state["b", g]
            head0 = A_Q_HEADS + g * B_HEADS_PER_GROUP
            sub_l = lax.broadcasted_iota(jnp.int32, (8, length), 0)
            col = lax.broadcasted_iota(jnp.int32, (8, length), 1)
            slope = _rows_where(sub_l, [slopes_ref[head0 + h] * LOG2E for h in range(B_HEADS_PER_GROUP)])
            s = jnp.where((col & (dil - 1)) == 0, s - slope * (length - col).astype(F32), NEG)
            m = jnp.maximum(jnp.max(s, axis=-1, keepdims=True), s_new)
            e = jnp.exp2(s - m)
            e_new = jnp.exp2(s_new - m)
            denom = jnp.sum(e, axis=-1, keepdims=True) + e_new
            state["b", g] = (e, e_new, v_new, m, denom)

    def values():
        e, e_new, v_new, m, denom = state["a"]
        acc = _dot_nt(e, ca_ref[A_KVW:2 * A_KVW, :].astype(BF16))
        out = jnp.where(half == kv_head, (acc + e_new * v_new) / denom, 0.0)
        for c in range(A_W // LANES):
            k = c // 2
            pieces = []
            for hh in range(2):
                r = k * A_GROUP + 2 * (c % 2) + hh
                piece = out[r:r + 1, :]
                pieces.append(piece if hh == k else pltpu.roll(piece, HEAD_DIM, axis=1))
            oa_ref[:, c * LANES:(c + 1) * LANES] = jnp.where(lane[:1] < HEAD_DIM, pieces[0], pieces[1])
        for g, (c_ref, _, _) in enumerate(b_refs):
            e, e_new, v_new, m, denom = state["b", g]
            length = c_ref.shape[1]
            weights = jnp.broadcast_to(e[0:B_HEADS_PER_GROUP, None, :], (B_HEADS_PER_GROUP, HEAD_DIM, length))
            acc_col = jnp.sum(c_ref[B_GW:2 * B_GW, :] * weights.reshape(B_GW, length), axis=1, keepdims=True)
            acc = jnp.broadcast_to(jnp.broadcast_to(acc_col, (B_GW, 8)).T[0:1, :], (8, B_GW))
            out = jnp.where(head_mask, (acc + e_new * v_new) / denom, 0.0)
            ob_ref[g] = jnp.sum(out, axis=0, keepdims=True)
            lse = jnp.where(head_mask, m + jnp.log2(denom), 0.0)
            lse_ref[g] = jnp.sum(lse, axis=0, keepdims=True)

    def shift(c_ref, t_ref, n_ref, length, piece, n_pieces):
        rows_per = c_ref.shape[0] // n_pieces
        return lambda: _shift_cache(c_ref, t_ref, n_ref, n, length, slice(piece * rows_per, (piece + 1) * rows_per))

    def shift_small():
        shift(ca_ref, ta_ref, na_ref, BLOCK, 0, 1)()
        shift(cb0_ref, tb0_ref, nb0_ref, B_PATTERNS[0][0], 0, 1)()

    shift_b1 = shift(cb1_ref, tb1_ref, nb1_ref, B_PATTERNS[1][0], 0, 1)
    shift_b2 = [shift(cb2_ref, tb2_ref, nb2_ref, B_PATTERNS[2][0], piece, _N_SHIFT_PIECES)
                for piece in range(_N_SHIFT_PIECES)]
    return [scores, shift_small, softmaxes, shift_b1, values] + shift_b2


_N_SHIFT_PIECES = 4


_N_FFN_INPUTS = 5
_N_SAMPLE_INPUTS = 14
_N_SAMPLE_OUTPUTS = 7


def _ffn_sample_kernel(*refs):
    ffn_in, refs = refs[:_N_FFN_INPUTS], refs[_N_FFN_INPUTS:]
    sample_in, refs = refs[:_N_SAMPLE_INPUTS], refs[_N_SAMPLE_INPUTS:]
    o_ref, sample_out, (act_ref,) = refs[0], refs[1:1 + _N_SAMPLE_OUTPUTS], refs[1 + _N_SAMPLE_OUTPUTS:]
    x_ref, gain_ref, wg_ref, wu_ref, wd_ref = ffn_in
    o_ref[...] = _swiglu_residual(x_ref[...], gain_ref[...], wg_ref, wu_ref, wd_ref, act_ref,
                                  side_work=_sample_stages(*sample_in, *sample_out))


def _ffn_and_sample(x, gain, wg, wu, wd, slopes, sinks, qa, kva, b0, b12, tails, caches):
    n_seq = qa.shape[0]
    t = x.shape[0]
    assert t % n_seq == 0
    tm = t // n_seq
    full = lambda a: pl.BlockSpec(a.shape, lambda n: (0,) * a.ndim)
    per_seq = lambda a: pl.BlockSpec((None,) + a.shape[1:], lambda n: (n, 0, 0))
    small = [qa, kva, b0, b12] + list(tails)
    assert 2 + len(small) + len(caches) == _N_SAMPLE_INPUTS
    out_shape = [jax.ShapeDtypeStruct((t, D_MODEL), F32),
                 jax.ShapeDtypeStruct((n_seq, 1, A_W), F32),
                 jax.ShapeDtypeStruct((B_N_GROUPS, n_seq, 1, B_GW), F32),
                 jax.ShapeDtypeStruct((B_N_GROUPS, n_seq, 1, B_GW), F32)]
    out_shape += [jax.ShapeDtypeStruct(c.shape, F32) for c in caches]
    out_specs = [pl.BlockSpec((tm, D_MODEL), lambda n: (n, 0)),
                 pl.BlockSpec((None, 1, A_W), lambda n: (n, 0, 0)),
                 pl.BlockSpec((B_N_GROUPS, None, 1, B_GW), lambda n: (0, n, 0, 0)),
                 pl.BlockSpec((B_N_GROUPS, None, 1, B_GW), lambda n: (0, n, 0, 0))]
    out_specs += [per_seq(c) for c in caches]
    x1, oa, ob, lse, *new_caches = pl.pallas_call(
        _ffn_sample_kernel,
        grid=(n_seq,),
        in_specs=[pl.BlockSpec((tm, D_MODEL), lambda n: (n, 0)),
                  _const_spec((1, D_MODEL)), _const_spec((D_MODEL, D_FF)),
                  _const_spec((D_MODEL, D_FF)), _const_spec((D_FF, D_MODEL))]
                 + [_smem_spec(), _smem_spec()] + [full(a) for a in small] + [per_seq(c) for c in caches],
        out_specs=out_specs,
        out_shape=out_shape,
        scratch_shapes=[pltpu.VMEM((tm, D_FF), BF16)],
        compiler_params=_params(1),
        name="ffn1_sample",
    )(x, gain, wg, wu, wd, slopes, sinks, *small, *caches)
    return (x1, oa.reshape(n_seq, A_W), ob.reshape(B_N_GROUPS, n_seq, B_GW),
            lse.reshape(B_N_GROUPS, n_seq, B_GW), *new_caches)


_N_A_PAIRS = A_W // LANES
_N_B_PAIRS = B_GW // LANES


def _merge_kernel(*refs):
    refs = list(refs)
    take = lambda count: [refs.pop(0) for _ in range(count)]
    (x_ref,), oa_refs, da_refs, ma_refs = take(1), take(_N_A_PAIRS), take(_N_A_PAIRS), take(_N_A_PAIRS)
    n_b = B_N_GROUPS * _N_B_PAIRS
    ob_refs, db_refs, mb_refs = take(n_b), take(n_b), take(n_b)
    sinks_ref, gates_ref, wua_ref, wub_ref, wo_ref, gain_ref, wg_ref, wu_ref, wd_ref, y_ref, act_ref = refs

    unswap = lambda ref: pltpu.roll(ref[...], HEAD_DIM, axis=1)
    oa_pairs = []
    for p, (o_ref, d_ref, m_ref) in enumerate(zip(oa_refs, da_refs, ma_refs)):
        denom = unswap(d_ref) + jnp.exp2(sinks_ref[:, p * LANES:(p + 1) * LANES] - m_ref[...])
        oa_pairs.append((o_ref[...].astype(F32) / denom).astype(BF16))
    ob_pairs = []
    for p in range(_N_B_PAIRS):
        idx = [g * _N_B_PAIRS + p for g in range(B_N_GROUPS)]
        maxes = [mb_refs[i][...] for i in idx]
        top = functools.reduce(jnp.maximum, maxes)
        weights = [jnp.exp2(m - top) for m in maxes]
        num = sum(w * ob_refs[i][...] for w, i in zip(weights, idx))
        den = sum(w * unswap(db_refs[i]) for w, i in zip(weights, idx))
        ob_pairs.append((num / den).astype(BF16))
    ua = _dot(jnp.concatenate(oa_pairs, axis=1), wua_ref[...])
    ub = _dot(jnp.concatenate(ob_pairs, axis=1), wub_ref[...])
    gate_a = gates_ref[:, :D_MODEL].astype(F32)
    gate_b = gates_ref[:, D_MODEL:].astype(F32)
    mixed = (gate_a * ua + gate_b * ub).astype(BF16)
    x = x_ref[...] + _dot(mixed, wo_ref[...])
    y_ref[...] = _swiglu_residual(x, gain_ref[...], wg_ref, wu_ref, wd_ref, act_ref)


def _merge(x, mixer_a, mixer_b, sink_lanes, gates, wua, wub, wo, gain, wg, wu, wd, tm):
    t = x.shape[0]
    row_spec = lambda w: pl.BlockSpec((tm, w), lambda i: (i, 0))
    pair_inputs = [a for part in tuple(mixer_a) + tuple(mixer_b) for a in part]
    n_pair_inputs = len(pair_inputs)
    assert n_pair_inputs == 3 * _N_A_PAIRS + 3 * B_N_GROUPS * _N_B_PAIRS
    return pl.pallas_call(
        _merge_kernel,
        grid=(t // tm,),
        in_specs=[row_spec(D_MODEL)] + [row_spec(LANES)] * n_pair_inputs + [_const_spec((1, A_W)),
                  row_spec(2 * D_MODEL), _const_spec((A_W, D_MODEL)), _const_spec((B_GW, D_MODEL)),
                  _const_spec((D_MODEL, D_MODEL)),
                  _const_spec((1, D_MODEL)), _const_spec((D_MODEL, D_FF)), _const_spec((D_MODEL, D_FF)),
                  _const_spec((D_FF, D_MODEL))],
        out_specs=row_spec(D_MODEL),
        out_shape=jax.ShapeDtypeStruct((t, D_MODEL), F32),
        scratch_shapes=[pltpu.VMEM((tm, D_FF), BF16)],
        compiler_params=_params(1),
        name="merge_ffn2",
    )(x, *pair_inputs, sink_lanes, gates, wua, wub, wo, gain, wg, wu, wd)


def _cache_view(cache):
    _, n, length, two, h, d = cache.shape
    return jnp.transpose(cache, (0, 1, 3, 4, 5, 2)).reshape(n, two * h * d, length)


def _state_view(rows_by_len, heads):
    n, _, length = rows_by_len.shape
    return jnp.transpose(rows_by_len.reshape(1, n, 2, heads, HEAD_DIM, length), (0, 1, 5, 2, 3, 4))


def kernel(x_prompt, x_sample, cache_a_kv, cache_b1_kv, cache_b2_kv, cache_b3_kv, norm_ffn1, w1_gate, w1_up,
           w1_down, norm_mix, w_in, q_norm_a, k_norm_a, q_norm_b, k_norm_b, sinks_a, w_up_a, w_up_b, w_o,
           norm_ffn2, w2_gate, w2_up, w2_down):
    assert x_prompt.shape[-1] == D_MODEL and w_in.shape == (1, D_MODEL, IN_W)
    batch, seq, _ = x_prompt.shape
    dec = x_sample.shape[0]
    assert x_sample.shape[1] == 1 and seq % (BLOCK * B_PATTERNS[-1][1]) == 0
    assert cache_a_kv.shape[2] == BLOCK
    assert all(c.shape[2] == w for c, (w, _) in zip((cache_b1_kv, cache_b2_kv, cache_b3_kv), B_PATTERNS))

    late_weights = [w[0] for w in (w_up_a, w_up_b, w_o, w2_gate, w2_up, w2_down)]

    i = jnp.arange(1, N_ALIBI_HEADS + 1, dtype=F32)
    slopes = jnp.exp2(-8.0 * i / N_ALIBI_HEADS)
    sinks = sinks_a[0].reshape(A_Q_HEADS).astype(F32)
    ones64 = jnp.ones((HEAD_DIM,), F32)
    q_scale = ATTN_SCALE * LOG2E
    qkgain = jnp.concatenate([
        jnp.tile(q_norm_a[0] * q_scale, A_Q_HEADS), jnp.tile(k_norm_a[0], A_KV_HEADS),
        jnp.tile(ones64, A_KV_HEADS),
        jnp.tile(q_norm_b[0] * q_scale, B_N_GROUPS * B_HEADS_PER_GROUP),
        jnp.tile(k_norm_b[0], B_N_GROUPS * B_HEADS_PER_GROUP),
        jnp.tile(ones64, B_N_GROUPS * B_HEADS_PER_GROUP)]).reshape(1, QKV_W).astype(F32)
    head_of = jnp.arange(CHUNK) // HEAD_DIM
    ones_bd = (head_of[:, None] == head_of[None, :]).astype(BF16)

    x1s, wg1, wu1, wd1, w_in_b = _ffn_cast(x_sample.reshape(dec, D_MODEL), norm_ffn1, w1_gate[0], w1_up[0],
                                           w1_down[0], w_in[0])

    def proj(x1, n_seq, s, tm, riders=()):
        return _proj(x1, norm_mix, w_in_b, qkgain, ones_bd, n_seq, s, tm, riders)

    (qa_s, kva_s, b0_s, b12_s, gates_s, *tails_s), _ = proj(x1s, 1, dec, dec)
    caches = [_cache_view(c) for c in (cache_a_kv, cache_b1_kv, cache_b2_kv, cache_b3_kv)]
    x1p, oa_s, ob_s, lse_s, na, nb0, nb1, nb2 = _ffn_and_sample(
        x_prompt.reshape(batch * seq, D_MODEL), norm_ffn1, wg1, wu1, wd1,
        slopes, sinks, qa_s.astype(F32), kva_s.astype(F32), b0_s.astype(F32), b12_s,
        [t[0] for t in tails_s], caches)

    (qa, kva, b0, b12, gates, ta, tb0, tb1, tb2), (wua, wub, wo, wg2, wu2, wd2) = proj(
        x1p, batch, seq, 512, late_weights)

    def back(x1, mixer_a, mixer_b, sink_lanes, gates, tm):
        return _merge(x1, mixer_a, mixer_b, sink_lanes, gates, wua, wub, wo, norm_ffn2, wg2, wu2, wd2, tm)

    common = dict(n_seq=batch, seq=seq)
    mixer_a = _attn(slopes, qa, kva, kva, (0, 1, 2, 3), (0, 1), (2, 3), k_pair=(0, 0, 1, 1), dil=1,
                    head0=0, out_dtype=BF16, name="attn_a", **common)
    mixer_b = ([], [], [])
    for g, (_, dil) in enumerate(B_PATTERNS):
        src = b0 if g == 0 else b12
        first = 0 if g == 0 else (g - 1) * 3 * _N_B_PAIRS
        parts = _attn(slopes, src, src, src, (first, first + 1), (first + 2, first + 3), (first + 4, first + 5),
                      k_pair=(0, 1), dil=dil, head0=A_Q_HEADS + g * B_HEADS_PER_GROUP, out_dtype=F32,
                      name=f"attn_b{g}", **common)
        for acc, part in zip(mixer_b, parts):
            acc += part
    sink_lanes = jnp.repeat(sinks * LOG2E, HEAD_DIM).reshape(1, A_W)
    y_prompt = back(x1p, mixer_a, mixer_b, sink_lanes, gates, 512).reshape(batch, seq, D_MODEL)

    pairs = lambda a: [a[..., p * LANES:(p + 1) * LANES] for p in range(a.shape[-1] // LANES)]
    one, zero = jnp.ones((dec, LANES), F32), jnp.zeros((dec, LANES), F32)
    ob_s = [pair for g in range(B_N_GROUPS) for pair in pairs(ob_s[g])]
    lse_s = [pair for g in range(B_N_GROUPS) for pair in pairs(lse_s[g])]
    y_sample = back(x1s, (pairs(oa_s.astype(BF16)), [one] * _N_A_PAIRS, [zero] * _N_A_PAIRS),
                    (ob_s, [one] * len(ob_s), lse_s), jnp.full((1, A_W), NEG, F32), gates_s,
                    dec).reshape(dec, 1, D_MODEL)

    return (y_prompt, y_sample,
            _state_view(ta, A_KV_HEADS), _state_view(tb0, B_HEADS_PER_GROUP),
            _state_view(tb1, B_HEADS_PER_GROUP), _state_view(tb2, B_HEADS_PER_GROUP),
            _state_view(na, A_KV_HEADS), _state_view(nb0, B_HEADS_PER_GROUP),
            _state_view(nb1, B_HEADS_PER_GROUP), _state_view(nb2, B_HEADS_PER_GROUP))
```

```python
import functools
import math

import jax
import jax.numpy as jnp
from jax import lax
from jax.experimental import pallas as pl
from jax.experimental.pallas import tpu as pltpu

D_MODEL = 1024
D_FF = 2816
HEAD_DIM = 64
A_Q_HEADS = 8
A_KV_HEADS = 2
A_GROUP = A_Q_HEADS // A_KV_HEADS
B_PATTERNS = ((128, 1), (512, 4), (2048, 16))
B_HEADS_PER_GROUP = 4
B_N_GROUPS = 3
N_ALIBI_HEADS = A_Q_HEADS + B_N_GROUPS * B_HEADS_PER_GROUP
BLOCK = 128
EPS = 1e-6
ATTN_SCALE = HEAD_DIM ** -0.5
LOG2E = math.log2(math.e)
A_W = A_Q_HEADS * HEAD_DIM
A_KVW = A_KV_HEADS * HEAD_DIM
B_GW = B_HEADS_PER_GROUP * HEAD_DIM
B_W = B_N_GROUPS * B_GW
QKV_W = A_W + 2 * A_KVW + 3 * B_W
IN_W = QKV_W + 2 * D_MODEL

LANES = 128
CHUNK = 256
NEG = -1e30
VMEM_LIMIT = 56 * 1024 * 1024

F32 = jnp.float32
BF16 = jnp.bfloat16


def _const_spec(shape):
    nd = len(shape)
    return pl.BlockSpec(shape, lambda *_: (0,) * nd, pipeline_mode=pl.Buffered(1))


def _smem_spec():
    return pl.BlockSpec(memory_space=pltpu.SMEM)


def _params(n_axes):
    return pltpu.CompilerParams(
        dimension_semantics=("arbitrary",) * n_axes, vmem_limit_bytes=VMEM_LIMIT)


def _dot(a, b):
    return jnp.dot(a, b, preferred_element_type=F32)


def _dot_nt(a, b):
    return lax.dot_general(a, b, (((1,), (1,)), ((), ())), preferred_element_type=F32)


def _swiglu_residual(x, gain, wg_ref, wu_ref, wd_ref, act_ref, side_work=()):
    n_chunks = D_FF // CHUNK
    assert len(side_work) <= n_chunks
    h = (x * gain).astype(BF16)
    inv_rms = lax.rsqrt(jnp.mean(x * x, axis=-1, keepdims=True) + EPS)
    for c in range(n_chunks):
        if c < len(side_work):
            side_work[c]()
        sl = slice(c * CHUNK, (c + 1) * CHUNK)
        g = _dot(h, wg_ref[:, sl]) * inv_rms
        u = _dot(h, wu_ref[:, sl]) * inv_rms
        act_ref[:, sl] = (g * jax.nn.sigmoid(g) * u).astype(BF16)
    return x + 0.5 * _dot(act_ref[...], wd_ref[...])


_WIN_CAST_BLOCKS = 8


def _ffn_cast_kernel(x_ref, gain_ref, wg_ref, wu_ref, wd_ref, win_ref,
                     y_ref, wg_out, wu_out, wd_out, win_out, h_ref, inv_rms_ref, acc_ref):
    c = pl.program_id(0)

    @pl.when(c == 0)
    def _():
        x = x_ref[...]
        h_ref[...] = (x * gain_ref[...]).astype(BF16)
        inv_rms = lax.rsqrt(jnp.mean(x * x, axis=-1, keepdims=True) + EPS)
        inv_rms_ref[...] = jnp.broadcast_to(inv_rms, inv_rms_ref.shape)
        acc_ref[...] = jnp.zeros(acc_ref.shape, F32)

    wg, wu, wd = wg_ref[...].astype(BF16), wu_ref[...].astype(BF16), wd_ref[...].astype(BF16)
    wg_out[...], wu_out[...], wd_out[...] = wg, wu, wd
    win_out[...] = win_ref[...].astype(BF16)
    inv_rms = inv_rms_ref[:, 0:1]
    g = _dot(h_ref[...], wg) * inv_rms
    u = _dot(h_ref[...], wu) * inv_rms
    acc_ref[...] += _dot((g * jax.nn.sigmoid(g) * u).astype(BF16), wd)

    @pl.when(c == pl.num_programs(0) - 1)
    def _():
        y_ref[...] = x_ref[...] + 0.5 * acc_ref[...]


def _ffn_cast(x, gain, wg, wu, wd, w_in):
    t = x.shape[0]
    n_chunks = D_FF // CHUNK
    assert n_chunks >= _WIN_CAST_BLOCKS and D_MODEL % (_WIN_CAST_BLOCKS * _BF16_ROWS) == 0
    col_spec = lambda: pl.BlockSpec((D_MODEL, CHUNK), lambda c: (0, c))
    row_spec = lambda: pl.BlockSpec((CHUNK, D_MODEL), lambda c: (c, 0))
    win_spec = lambda: pl.BlockSpec((D_MODEL // _WIN_CAST_BLOCKS, IN_W),
                                    lambda c: (jnp.minimum(c, _WIN_CAST_BLOCKS - 1), 0))
    return pl.pallas_call(
        _ffn_cast_kernel,
        grid=(n_chunks,),
        in_specs=[_const_spec((t, D_MODEL)), _const_spec((1, D_MODEL)), col_spec(), col_spec(), row_spec(),
                  win_spec()],
        out_specs=[pl.BlockSpec((t, D_MODEL), lambda c: (0, 0)), col_spec(), col_spec(), row_spec(), win_spec()],
        out_shape=[jax.ShapeDtypeStruct((t, D_MODEL), F32), jax.ShapeDtypeStruct(wg.shape, BF16),
                   jax.ShapeDtypeStruct(wu.shape, BF16), jax.ShapeDtypeStruct(wd.shape, BF16),
                   jax.ShapeDtypeStruct(w_in.shape, BF16)],
        scratch_shapes=[pltpu.VMEM((t, D_MODEL), BF16), pltpu.VMEM((t, LANES), F32), pltpu.VMEM((t, D_MODEL), F32)],
        compiler_params=_params(1),
        name="ffn1_cast",
    )(x, gain, wg, wu, wd, w_in)


_N_QKV_CHUNKS = QKV_W // CHUNK
_N_CHUNKS = IN_W // CHUNK
_LAST_CHUNK = 9


def _tail_plan(tail, tm, n_tiles):
    if tail >= tm:
        return tm, n_tiles - tail // tm
    return tail, n_tiles - 1


_N_PROJ_INPUTS = 5
_N_PROJ_OUTPUTS = 9
_BF16_ROWS = 16


def _proj_kernel(*refs, tm, n_tiles, tails):
    n_riders = (len(refs) - _N_PROJ_INPUTS - _N_PROJ_OUTPUTS) // 2
    x_ref, gain_ref, w_ref, qkgain_ref, ones_ref = refs[:_N_PROJ_INPUTS]
    rider_in = refs[_N_PROJ_INPUTS:_N_PROJ_INPUTS + n_riders]
    outs = refs[_N_PROJ_INPUTS + n_riders:]
    qa_ref, kva_ref, b0_ref, b12_ref, gates_ref, ta_ref, tb0_ref, tb1_ref, tb2_ref = outs[:_N_PROJ_OUTPUTS]
    for src, dst in zip(rider_in, outs[_N_PROJ_OUTPUTS:]):
        dst[...] = src[...].astype(BF16)
    x = x_ref[...]
    h = (x * gain_ref[...]).astype(BF16)
    inv_rms = lax.rsqrt(jnp.mean(x * x, axis=-1, keepdims=True) + EPS)
    lane = lax.broadcasted_iota(jnp.int32, (tm, CHUNK), 1)
    lane1 = lax.broadcasted_iota(jnp.int32, (tm, LANES), 1)

    def z_chunk(c):
        return _dot(h, w_ref[:, c * CHUNK:(c + 1) * CHUNK]) * inv_rms

    def head_norm(z, c):
        ss = _dot((z * z).astype(BF16), ones_ref[...])
        return z * lax.rsqrt(ss * (1.0 / HEAD_DIM) + EPS) * qkgain_ref[:, c * CHUNK:(c + 1) * CHUNK]

    def write_tail(t_ref, row0, y, tail):
        block_w, _ = _tail_plan(tail, tm, n_tiles)
        data = y if block_w == tm else y[tm - block_w:, :]
        t_ref[row0:row0 + CHUNK, :] = data.T

    tb_refs = (tb0_ref, tb1_ref, tb2_ref)

    def consume(c, z):
        if c < 2:
            qa_ref[:, c * CHUNK:(c + 1) * CHUNK] = head_norm(z, c).astype(BF16)
        elif c == 2:
            y = jnp.where(lane < A_KVW, head_norm(z, c), z)
            write_tail(ta_ref, 0, y, tails[0])
            for part in range(2):
                pair = y[:, part * LANES:(part + 1) * LANES]
                swapped = pltpu.roll(pair, HEAD_DIM, axis=1)
                base = part * 2 * LANES
                kva_ref[:, base:base + LANES] = jnp.where(lane1 < HEAD_DIM, pair, swapped).astype(BF16)
                kva_ref[:, base + LANES:base + 2 * LANES] = jnp.where(lane1 < HEAD_DIM, swapped, pair).astype(BF16)
        elif c < _N_QKV_CHUNKS:
            kind, g = divmod(c - 3, B_N_GROUPS)
            y = z if kind == 2 else head_norm(z, c)
            if g == 0:
                b0_ref[:, kind * CHUNK:(kind + 1) * CHUNK] = y.astype(BF16)
            else:
                col = ((g - 1) * 3 + kind) * CHUNK
                b12_ref[:, col:col + CHUNK] = y
            if kind > 0:
                write_tail(tb_refs[g], (kind - 1) * CHUNK, y, tails[1 + g])
        else:
            col = (c - _N_QKV_CHUNKS) * CHUNK
            gates_ref[:, col:col + CHUNK] = jax.nn.sigmoid(z).astype(BF16)

    order = list(range(_N_QKV_CHUNKS, _N_CHUNKS)) + [c for c in range(_N_QKV_CHUNKS) if c != _LAST_CHUNK]
    order.append(_LAST_CHUNK)
    z_next = z_chunk(order[0])
    for i, c in enumerate(order):
        z = z_next
        if i + 1 < len(order):
            z_next = z_chunk(order[i + 1])
        consume(c, z)


def _rider_specs(weights, n_steps, n_tiles):
    specs = []
    for w in weights:
        rows, cols = w.shape
        n_blocks = n_steps
        while rows % (n_blocks * _BF16_ROWS):
            assert n_blocks % 2 == 0, (rows, n_steps)
            n_blocks //= 2
        per = n_steps // n_blocks
        specs.append(pl.BlockSpec((rows // n_blocks, cols), lambda n, j, per=per: ((n * n_tiles + j) // per, 0)))
    return specs


def _proj(x, gain, w_in, qkgain, ones_bd, n_seq, seq, tm, riders=()):
    t = n_seq * seq
    n_tiles = seq // tm
    rider_specs = _rider_specs(riders, n_seq * n_tiles, n_tiles)
    tails = (min(128, seq),) + tuple(min(w, seq) for w, _ in B_PATTERNS)
    tail_rows = (2 * A_KVW, 2 * B_GW, 2 * B_GW, 2 * B_GW)

    def tail_spec(rows, tail):
        block_w, first = _tail_plan(tail, tm, n_tiles)
        return pl.BlockSpec((None, rows, block_w), lambda n, j: (n, 0, jnp.maximum(j - first, 0)))

    row_spec = lambda w: pl.BlockSpec((tm, w), lambda n, j: (n * n_tiles + j, 0))
    out_shape = [
        jax.ShapeDtypeStruct((t, A_W), BF16),
        jax.ShapeDtypeStruct((t, 4 * LANES), BF16),
        jax.ShapeDtypeStruct((t, 3 * B_GW), BF16),
        jax.ShapeDtypeStruct((t, 6 * B_GW), F32),
        jax.ShapeDtypeStruct((t, 2 * D_MODEL), BF16),
    ] + [jax.ShapeDtypeStruct((n_seq, r, tl), F32) for r, tl in zip(tail_rows, tails)]
    out_specs = [row_spec(A_W), row_spec(4 * LANES), row_spec(3 * B_GW), row_spec(6 * B_GW),
                 row_spec(2 * D_MODEL)] + [tail_spec(r, tl) for r, tl in zip(tail_rows, tails)]
    assert len(out_shape) == _N_PROJ_OUTPUTS
    out_shape += [jax.ShapeDtypeStruct(w.shape, BF16) for w in riders]
    outs = pl.pallas_call(
        functools.partial(_proj_kernel, tm=tm, n_tiles=n_tiles, tails=tails),
        grid=(n_seq, n_tiles),
        in_specs=[row_spec(D_MODEL), _const_spec((1, D_MODEL)), _const_spec((D_MODEL, IN_W)),
                  _const_spec((1, QKV_W)), _const_spec((CHUNK, CHUNK))] + rider_specs,
        out_specs=out_specs + _rider_specs(riders, n_seq * n_tiles, n_tiles),
        out_shape=out_shape,
        compiler_params=_params(2),
        name="proj",
    )(x, gain, w_in, qkgain, ones_bd, *riders)
    return outs[:_N_PROJ_OUTPUTS], outs[_N_PROJ_OUTPUTS:]


_HEADS_PER_ITER = 128


def _attn_kernel(*refs, n_pairs, k_pair, dil, seq, head0):
    refs = list(refs)
    slopes_ref = refs.pop(0)
    n_kv = max(k_pair) + 1
    take = lambda count: [refs.pop(0) for _ in range(count)]
    q_refs, k_refs, v_refs = take(n_pairs), take(n_kv), take(n_kv)
    o_refs, den_refs, max_refs = take(n_pairs), take(n_pairs), take(n_pairs)
    (bias_ref,) = refs

    n_blocks = seq // (BLOCK * dil)
    use_prev = n_blocks > 1
    n_heads = 2 * n_pairs
    qi = lax.broadcasted_iota(jnp.int32, (BLOCK, BLOCK), 0)
    ki = lax.broadcasted_iota(jnp.int32, (BLOCK, BLOCK), 1)
    d_cur = qi - ki
    lo = ki < HEAD_DIM
    lo_keys = lax.broadcasted_iota(jnp.int32, ((2 if use_prev else 1) * BLOCK, LANES), 1) < HEAD_DIM

    for h in range(n_heads):
        slope = slopes_ref[head0 + h] * (float(dil) * LOG2E)
        cur_bias = jnp.where(d_cur >= 0, -slope * d_cur.astype(F32), NEG)
        if use_prev:
            prev_bias = jnp.where(d_cur <= 0, -slope * (d_cur + BLOCK).astype(F32), NEG)
            bias_ref[h] = jnp.concatenate([prev_bias, cur_bias], axis=1)
            bias_ref[n_heads + h] = jnp.concatenate([jnp.full_like(prev_bias, NEG), cur_bias], axis=1)
        else:
            bias_ref[h] = cur_bias

    def block_rows(step):
        if n_blocks == 1:
            r, j = step, 0
        else:
            r, j = step // n_blocks, step % n_blocks
        if dil == 1:
            cur = pl.ds(pl.multiple_of(j * BLOCK, BLOCK), BLOCK)
            prev = pl.ds(pl.multiple_of(jnp.maximum(j - 1, 0) * BLOCK, BLOCK), BLOCK)
        else:
            cur = pl.ds(r + j * (BLOCK * dil), BLOCK, stride=dil)
            prev = pl.ds(r + jnp.maximum(j - 1, 0) * (BLOCK * dil), BLOCK, stride=dil)
        table = jnp.where(j == 0, n_heads, 0) if use_prev else 0
        return cur, prev, table

    blocks_per_iter = min(_HEADS_PER_ITER // (2 * n_pairs), dil * n_blocks)
    assert (dil * n_blocks) % blocks_per_iter == 0

    def body(it, carry):
        blocks = [block_rows(it * blocks_per_iter + b) for b in range(blocks_per_iter)]
        scores, values = {}, {}
        for b, (cur, prev, _) in enumerate(blocks):
            for p in range(n_pairs):
                k_ref, v_ref = k_refs[k_pair[p]], v_refs[k_pair[p]]
                q = q_refs[p][cur, :].astype(BF16)
                keys = k_ref[cur, :].astype(BF16)
                vals = v_ref[cur, :].astype(BF16)
                if use_prev:
                    keys = jnp.concatenate([k_ref[prev, :].astype(BF16), keys], axis=0)
                    vals = jnp.concatenate([v_ref[prev, :].astype(BF16), vals], axis=0)
                one = jnp.ones_like(vals)
                values[b, p, 0] = jnp.where(lo_keys, vals, one)
                values[b, p, 1] = jnp.where(lo_keys, one, vals)
                for half in range(2):
                    qm = jnp.where(lo if half == 0 else jnp.logical_not(lo), q, jnp.zeros_like(q))
                    scores[b, p, half] = _dot_nt(qm, keys)
        probs, row_max = {}, {}
        for (b, p, half), s in scores.items():
            s = s + bias_ref[blocks[b][2] + 2 * p + half]
            m = jnp.max(s, axis=-1, keepdims=True)
            probs[b, p, half], row_max[b, p, half] = jnp.exp2(s - m).astype(BF16), m
        for b, (cur, _, _) in enumerate(blocks):
            for p in range(n_pairs):
                acc = [_dot(probs[b, p, half], values[b, p, half]) for half in range(2)]
                o_refs[p][cur, :] = jnp.where(lo, acc[0], acc[1]).astype(o_refs[p].dtype)
                den_refs[p][cur, :] = jnp.where(lo, acc[1], acc[0])
                max_refs[p][cur, :] = jnp.where(lo, row_max[b, p, 0], row_max[b, p, 1])
        return carry

    lax.fori_loop(0, dil * n_blocks // blocks_per_iter, body, 0)


def _attn(slopes, q_arr, k_arr, v_arr, q_idx, k_idx, v_idx, *, n_seq, seq, k_pair, dil, head0, out_dtype, name):
    n_pairs = len(q_idx)

    def pair_spec(idx):
        return pl.BlockSpec((None, seq, LANES), lambda n: (n, 0, idx))

    view = lambda a: a.reshape(n_seq, seq, a.shape[-1])
    operands = [slopes] + [view(q_arr)] * n_pairs + [view(k_arr)] * len(k_idx) + [view(v_arr)] * len(v_idx)
    in_specs = [_smem_spec()] + [pair_spec(i) for i in tuple(q_idx) + tuple(k_idx) + tuple(v_idx)]
    out_shape = [jax.ShapeDtypeStruct((n_seq, seq, LANES), out_dtype)] * n_pairs
    out_shape += [jax.ShapeDtypeStruct((n_seq, seq, LANES), F32)] * (2 * n_pairs)
    outs = pl.pallas_call(
        functools.partial(_attn_kernel, n_pairs=n_pairs, k_pair=k_pair, dil=dil, seq=seq, head0=head0),
        grid=(n_seq,),
        in_specs=in_specs,
        out_specs=[pair_spec(0)] * len(out_shape),
        out_shape=out_shape,
        scratch_shapes=[pltpu.VMEM((2 * n_pairs, BLOCK, BLOCK) if seq == BLOCK * dil else
                                   (4 * n_pairs, BLOCK, 2 * BLOCK), F32)],
        compiler_params=_params(1),
        name=name,
    )(*operands)
    outs = [o.reshape(n_seq * seq, LANES) for o in outs]
    return outs[:n_pairs], outs[n_pairs:2 * n_pairs], outs[2 * n_pairs:]


def _rows_where(row_iota, values):
    out = jnp.zeros(row_iota.shape, F32)
    for r, v in enumerate(values):
        out = jnp.where(row_iota == r, v, out)
    return out


def _shift_cache(cache_ref, tail_ref, out_ref, n, length, rows):
    shifted = pltpu.roll(cache_ref[rows, :], length - 1, axis=1)
    new_col = pltpu.roll(tail_ref[rows, :], (LANES - 1) - n, axis=1)
    lane = lax.broadcasted_iota(jnp.int32, new_col.shape, 1)
    last = jnp.where(lane == LANES - 1, new_col, shifted[:, length - LANES:])
    out_ref[rows, :] = last if length == LANES else jnp.concatenate([shifted[:, :length - LANES], last], axis=1)


def _sample_stages(slopes_ref, sinks_ref, qa_ref, kva_ref, b0_ref, b12_ref,
                   ta_ref, tb0_ref, tb1_ref, tb2_ref, ca_ref, cb0_ref, cb1_ref, cb2_ref,
                   oa_ref, ob_ref, lse_ref, na_ref, nb0_ref, nb1_ref, nb2_ref):
    n = pl.program_id(0)
    row = pl.ds(n, 1)
    sub = lax.broadcasted_iota(jnp.int32, (8, LANES), 0)
    lane = lax.broadcasted_iota(jnp.int32, (8, LANES), 1)
    half = lane // HEAD_DIM
    kv_head = sub // A_GROUP
    sub_b = lax.broadcasted_iota(jnp.int32, (8, B_GW), 0)
    head_mask = (lax.broadcasted_iota(jnp.int32, (8, B_GW), 1) // HEAD_DIM) == sub_b
    b_refs = ((cb0_ref, tb0_ref, nb0_ref), (cb1_ref, tb1_ref, nb1_ref), (cb2_ref, tb2_ref, nb2_ref))
    state = {}

    def scores():
        q_row = qa_ref[row, :]
        q_blk = jnp.zeros((8, LANES), F32)
        for r in range(A_Q_HEADS):
            k, g = divmod(r, A_GROUP)
            chunk = q_row[:, (r // 2) * LANES:(r // 2 + 1) * LANES]
            if g % 2 != k:
                chunk = pltpu.roll(chunk, HEAD_DIM, axis=1)
            q_blk = jnp.where((sub == r) & (half == k), jnp.broadcast_to(chunk, (8, LANES)), q_blk)
        kva = kva_ref[row, :]
        k_new = jnp.where(lane[:1] < HEAD_DIM, kva[:, 0:LANES], kva[:, LANES:2 * LANES])
        v_new = jnp.where(lane[:1] < HEAD_DIM, kva[:, 2 * LANES:3 * LANES], kva[:, 3 * LANES:])
        s = _dot(q_blk.astype(BF16), ca_ref[0:A_KVW, :].astype(BF16))
        state["a"] = (s, jnp.sum(q_blk * k_new, axis=-1, keepdims=True), v_new)
        for g, (c_ref, _, _) in enumerate(b_refs):
            qkv = b0_ref[row, :] if g == 0 else b12_ref[row, (g - 1) * 3 * B_GW:g * 3 * B_GW]
            q_row, k_new, v_new = qkv[:, 0:B_GW], qkv[:, B_GW:2 * B_GW], qkv[:, 2 * B_GW:]
            q_blk = jnp.where(head_mask, jnp.broadcast_to(q_row, (8, B_GW)), 0.0)
            q_col = jnp.broadcast_to(q_row, (8, B_GW)).T[:, 0:1]
            prod = (c_ref[0:B_GW, :] * q_col).reshape(B_HEADS_PER_GROUP, HEAD_DIM, c_ref.shape[1])
            s4 = jnp.sum(prod, axis=1)
            s = jnp.concatenate([s4, jnp.zeros_like(s4)], axis=0)
            state["b", g] = (s, jnp.sum(q_blk * k_new, axis=-1, keepdims=True), v_new)

    def softmaxes():
        s, s_new, v_new = state["a"]
        slope = _rows_where(sub[:, :1], [slopes_ref[r] * LOG2E for r in range(A_Q_HEADS)])
        sink = _rows_where(sub[:, :1], [sinks_ref[r] * LOG2E for r in range(A_Q_HEADS)])
        s = s - slope * (BLOCK - lane).astype(F32)
        m = jnp.maximum(jnp.maximum(jnp.max(s, axis=-1, keepdims=True), s_new), sink)
        e = jnp.exp2(s - m)
        e_new = jnp.exp2(s_new - m)
        denom = jnp.sum(e, axis=-1, keepdims=True) + e_new + jnp.exp2(sink - m)
        state["a"] = (e.astype(BF16), e_new, v_new, m, denom)
        for g, (length, dil) in enumerate(B_PATTERNS):
            s, s_new, v_new = state["b", g]
            head0 = A_Q_HEADS + g * B_HEADS_PER_GROUP
            sub_l = lax.broadcasted_iota(jnp.int32, (8, length), 0)
            col = lax.broadcasted_iota(jnp.int32, (8, length), 1)
            slope = _rows_where(sub_l, [slopes_ref[head0 + h] * LOG2E for h in range(B_HEADS_PER_GROUP)])
            s = jnp.where((col & (dil - 1)) == 0, s - slope * (length - col).astype(F32), NEG)
            m = jnp.maximum(jnp.max(s, axis=-1, keepdims=True), s_new)
            e = jnp.exp2(s - m)
            e_new = jnp.exp2(s_new - m)
            denom = jnp.sum(e, axis=-1, keepdims=True) + e_new
            state["b", g] = (e, e_new, v_new, m, denom)

    def values():
        e, e_new, v_new, m, denom = state["a"]
        acc = _dot_nt(e, ca_ref[A_KVW:2 * A_KVW, :].astype(BF16))
        out = jnp.where(half == kv_head, (acc + e_new * v_new) / denom, 0.0)
        for c in range(A_W // LANES):
            k = c // 2
            pieces = []
            for hh in range(2):
                r = k * A_GROUP + 2 * (c % 2) + hh
                piece = out[r:r + 1, :]
                pieces.append(piece if hh == k else pltpu.roll(piece, HEAD_DIM, axis=1))
            oa_ref[:, c * LANES:(c + 1) * LANES] = jnp.where(lane[:1] < HEAD_DIM, pieces[0], pieces[1])
        for g, (c_ref, _, _) in enumerate(b_refs):
            e, e_new, v_new, m, denom = state["b", g]
            length = c_ref.shape[1]
            weights = jnp.broadcast_to(e[0:B_HEADS_PER_GROUP, None, :], (B_HEADS_PER_GROUP, HEAD_DIM, length))
            acc_col = jnp.sum(c_ref[B_GW:2 * B_GW, :] * weights.reshape(B_GW, length), axis=1, keepdims=True)
            acc = jnp.broadcast_to(jnp.broadcast_to(acc_col, (B_GW, 8)).T[0:1, :], (8, B_GW))
            out = jnp.where(head_mask, (acc + e_new * v_new) / denom, 0.0)
            ob_ref[g] = jnp.sum(out, axis=0, keepdims=True)
            lse = jnp.where(head_mask, m + jnp.log2(denom), 0.0)
            lse_ref[g] = jnp.sum(lse, axis=0, keepdims=True)

    def shift(c_ref, t_ref, n_ref, length, piece, n_pieces):
        rows_per = c_ref.shape[0] // n_pieces
        return lambda: _shift_cache(c_ref, t_ref, n_ref, n, length, slice(piece * rows_per, (piece + 1) * rows_per))

    def shift_small():
        shift(ca_ref, ta_ref, na_ref, BLOCK, 0, 1)()
        shift(cb0_ref, tb0_ref, nb0_ref, B_PATTERNS[0][0], 0, 1)()

    shift_b1 = shift(cb1_ref, tb1_ref, nb1_ref, B_PATTERNS[1][0], 0, 1)
    shift_b2 = [shift(cb2_ref, tb2_ref, nb2_ref, B_PATTERNS[2][0], piece, _N_SHIFT_PIECES)
                for piece in range(_N_SHIFT_PIECES)]
    return [scores, shift_small, softmaxes, shift_b1, values] + shift_b2


_N_SHIFT_PIECES = 4


_N_FFN_INPUTS = 5
_N_SAMPLE_INPUTS = 14
_N_SAMPLE_OUTPUTS = 7


def _ffn_sample_kernel(*refs):
    ffn_in, refs = refs[:_N_FFN_INPUTS], refs[_N_FFN_INPUTS:]
    sample_in, refs = refs[:_N_SAMPLE_INPUTS], refs[_N_SAMPLE_INPUTS:]
    o_ref, sample_out, (act_ref,) = refs[0], refs[1:1 + _N_SAMPLE_OUTPUTS], refs[1 + _N_SAMPLE_OUTPUTS:]
    x_ref, gain_ref, wg_ref, wu_ref, wd_ref = ffn_in
    o_ref[...] = _swiglu_residual(x_ref[...], gain_ref[...], wg_ref, wu_ref, wd_ref, act_ref,
                                  side_work=_sample_stages(*sample_in, *sample_out))


def _ffn_and_sample(x, gain, wg, wu, wd, slopes, sinks, qa, kva, b0, b12, tails, caches):
    n_seq = qa.shape[0]
    t = x.shape[0]
    assert t % n_seq == 0
    tm = t // n_seq
    full = lambda a: pl.BlockSpec(a.shape, lambda n: (0,) * a.ndim)
    per_seq = lambda a: pl.BlockSpec((None,) + a.shape[1:], lambda n: (n, 0, 0))
    small = [qa, kva, b0, b12] + list(tails)
    assert 2 + len(small) + len(caches) == _N_SAMPLE_INPUTS
    out_shape = [jax.ShapeDtypeStruct((t, D_MODEL), F32),
                 jax.ShapeDtypeStruct((n_seq, 1, A_W), F32),
                 jax.ShapeDtypeStruct((B_N_GROUPS, n_seq, 1, B_GW), F32),
                 jax.ShapeDtypeStruct((B_N_GROUPS, n_seq, 1, B_GW), F32)]
    out_shape += [jax.ShapeDtypeStruct(c.shape, F32) for c in caches]
    out_specs = [pl.BlockSpec((tm, D_MODEL), lambda n: (n, 0)),
                 pl.BlockSpec((None, 1, A_W), lambda n: (n, 0, 0)),
                 pl.BlockSpec((B_N_GROUPS, None, 1, B_GW), lambda n: (0, n, 0, 0)),
                 pl.BlockSpec((B_N_GROUPS, None, 1, B_GW), lambda n: (0, n, 0, 0))]
    out_specs += [per_seq(c) for c in caches]
    x1, oa, ob, lse, *new_caches = pl.pallas_call(
        _ffn_sample_kernel,
        grid=(n_seq,),
        in_specs=[pl.BlockSpec((tm, D_MODEL), lambda n: (n, 0)),
                  _const_spec((1, D_MODEL)), _const_spec((D_MODEL, D_FF)),
                  _const_spec((D_MODEL, D_FF)), _const_spec((D_FF, D_MODEL))]
                 + [_smem_spec(), _smem_spec()] + [full(a) for a in small] + [per_seq(c) for c in caches],
        out_specs=out_specs,
        out_shape=out_shape,
        scratch_shapes=[pltpu.VMEM((tm, D_FF), BF16)],
        compiler_params=_params(1),
        name="ffn1_sample",
    )(x, gain, wg, wu, wd, slopes, sinks, *small, *caches)
    return (x1, oa.reshape(n_seq, A_W), ob.reshape(B_N_GROUPS, n_seq, B_GW),
            lse.reshape(B_N_GROUPS, n_seq, B_GW), *new_caches)


_N_A_PAIRS = A_W // LANES
_N_B_PAIRS = B_GW // LANES


def _merge_kernel(*refs):
    refs = list(refs)
    take = lambda count: [refs.pop(0) for _ in range(count)]
    (x_ref,), oa_refs, da_refs, ma_refs = take(1), take(_N_A_PAIRS), take(_N_A_PAIRS), take(_N_A_PAIRS)
    n_b = B_N_GROUPS * _N_B_PAIRS
    ob_refs, db_refs, mb_refs = take(n_b), take(n_b), take(n_b)
    sinks_ref, gates_ref, wua_ref, wub_ref, wo_ref, gain_ref, wg_ref, wu_ref, wd_ref, y_ref, act_ref = refs

    unswap = lambda ref: pltpu.roll(ref[...], HEAD_DIM, axis=1)
    oa_pairs = []
    for p, (o_ref, d_ref, m_ref) in enumerate(zip(oa_refs, da_refs, ma_refs)):
        denom = unswap(d_ref) + jnp.exp2(sinks_ref[:, p * LANES:(p + 1) * LANES] - m_ref[...])
        oa_pairs.append((o_ref[...].astype(F32) / denom).astype(BF16))
    ob_pairs = []
    for p in range(_N_B_PAIRS):
        idx = [g * _N_B_PAIRS + p for g in range(B_N_GROUPS)]
        maxes = [mb_refs[i][...] for i in idx]
        top = functools.reduce(jnp.maximum, maxes)
        weights = [jnp.exp2(m - top) for m in maxes]
        num = sum(w * ob_refs[i][...] for w, i in zip(weights, idx))
        den = sum(w * unswap(db_refs[i]) for w, i in zip(weights, idx))
        ob_pairs.append((num / den).astype(BF16))
    ua = _dot(jnp.concatenate(oa_pairs, axis=1), wua_ref[...])
    ub = _dot(jnp.concatenate(ob_pairs, axis=1), wub_ref[...])
    gate_a = gates_ref[:, :D_MODEL].astype(F32)
    gate_b = gates_ref[:, D_MODEL:].astype(F32)
    mixed = (gate_a * ua + gate_b * ub).astype(BF16)
    x = x_ref[...] + _dot(mixed, wo_ref[...])
    y_ref[...] = _swiglu_residual(x, gain_ref[...], wg_ref, wu_ref, wd_ref, act_ref)


def _merge(x, mixer_a, mixer_b, sink_lanes, gates, wua, wub, wo, gain, wg, wu, wd, tm):
    t = x.shape[0]
    row_spec = lambda w: pl.BlockSpec((tm, w), lambda i: (i, 0))
    pair_inputs = [a for part in tuple(mixer_a) + tuple(mixer_b) for a in part]
    n_pair_inputs = len(pair_inputs)
    assert n_pair_inputs == 3 * _N_A_PAIRS + 3 * B_N_GROUPS * _N_B_PAIRS
    return pl.pallas_call(
        _merge_kernel,
        grid=(t // tm,),
        in_specs=[row_spec(D_MODEL)] + [row_spec(LANES)] * n_pair_inputs + [_const_spec((1, A_W)),
                  row_spec(2 * D_MODEL), _const_spec((A_W, D_MODEL)), _const_spec((B_GW, D_MODEL)),
                  _const_spec((D_MODEL, D_MODEL)),
                  _const_spec((1, D_MODEL)), _const_spec((D_MODEL, D_FF)), _const_spec((D_MODEL, D_FF)),
                  _const_spec((D_FF, D_MODEL))],
        out_specs=row_spec(D_MODEL),
        out_shape=jax.ShapeDtypeStruct((t, D_MODEL), F32),
        scratch_shapes=[pltpu.VMEM((tm, D_FF), BF16)],
        compiler_params=_params(1),
        name="merge_ffn2",
    )(x, *pair_inputs, sink_lanes, gates, wua, wub, wo, gain, wg, wu, wd)


def _cache_view(cache):
    _, n, length, two, h, d = cache.shape
    return jnp.transpose(cache, (0, 1, 3, 4, 5, 2)).reshape(n, two * h * d, length)


def _state_view(rows_by_len, heads):
    n, _, length = rows_by_len.shape
    return jnp.transpose(rows_by_len.reshape(1, n, 2, heads, HEAD_DIM, length), (0, 1, 5, 2, 3, 4))


def kernel(x_prompt, x_sample, cache_a_kv, cache_b1_kv, cache_b2_kv, cache_b3_kv, norm_ffn1, w1_gate, w1_up,
           w1_down, norm_mix, w_in, q_norm_a, k_norm_a, q_norm_b, k_norm_b, sinks_a, w_up_a, w_up_b, w_o,
           norm_ffn2, w2_gate, w2_up, w2_down):
    assert x_prompt.shape[-1] == D_MODEL and w_in.shape == (1, D_MODEL, IN_W)
    batch, seq, _ = x_prompt.shape
    dec = x_sample.shape[0]
    assert x_sample.shape[1] == 1 and seq % (BLOCK * B_PATTERNS[-1][1]) == 0
    assert cache_a_kv.shape[2] == BLOCK
    assert all(c.shape[2] == w for c, (w, _) in zip((cache_b1_kv, cache_b2_kv, cache_b3_kv), B_PATTERNS))

    late_weights = [w[0] for w in (w_up_a, w_up_b, w_o, w2_gate, w2_up, w2_down)]

    i = jnp.arange(1, N_ALIBI_HEADS + 1, dtype=F32)
    slopes = jnp.exp2(-8.0 * i / N_ALIBI_HEADS)
    sinks = sinks_a[0].reshape(A_Q_HEADS).astype(F32)
    ones64 = jnp.ones((HEAD_DIM,), F32)
    q_scale = ATTN_SCALE * LOG2E
    qkgain = jnp.concatenate([
        jnp.tile(q_norm_a[0] * q_scale, A_Q_HEADS), jnp.tile(k_norm_a[0], A_KV_HEADS),
        jnp.tile(ones64, A_KV_HEADS),
        jnp.tile(q_norm_b[0] * q_scale, B_N_GROUPS * B_HEADS_PER_GROUP),
        jnp.tile(k_norm_b[0], B_N_GROUPS * B_HEADS_PER_GROUP),
        jnp.tile(ones64, B_N_GROUPS * B_HEADS_PER_GROUP)]).reshape(1, QKV_W).astype(F32)
    head_of = jnp.arange(CHUNK) // HEAD_DIM
    ones_bd = (head_of[:, None] == head_of[None, :]).astype(BF16)

    x1s, wg1, wu1, wd1, w_in_b = _ffn_cast(x_sample.reshape(dec, D_MODEL), norm_ffn1, w1_gate[0], w1_up[0],
                                           w1_down[0], w_in[0])

    def proj(x1, n_seq, s, tm, riders=()):
        return _proj(x1, norm_mix, w_in_b, qkgain, ones_bd, n_seq, s, tm, riders)

    (qa_s, kva_s, b0_s, b12_s, gates_s, *tails_s), _ = proj(x1s, 1, dec, dec)
    caches = [_cache_view(c) for c in (cache_a_kv, cache_b1_kv, cache_b2_kv, cache_b3_kv)]
    x1p, oa_s, ob_s, lse_s, na, nb0, nb1, nb2 = _ffn_and_sample(
        x_prompt.reshape(batch * seq, D_MODEL), norm_ffn1, wg1, wu1, wd1,
        slopes, sinks, qa_s.astype(F32), kva_s.astype(F32), b0_s.astype(F32), b12_s,
        [t[0] for t in tails_s], caches)

    (qa, kva, b0, b12, gates, ta, tb0, tb1, tb2), (wua, wub, wo, wg2, wu2, wd2) = proj(
        x1p, batch, seq, 512, late_weights)

    def back(x1, mixer_a, mixer_b, sink_lanes, gates, tm):
        return _merge(x1, mixer_a, mixer_b, sink_lanes, gates, wua, wub, wo, norm_ffn2, wg2, wu2, wd2, tm)

    common = dict(n_seq=batch, seq=seq)
    mixer_a = _attn(slopes, qa, kva, kva, (0, 1, 2, 3), (0, 1), (2, 3), k_pair=(0, 0, 1, 1), dil=1,
                    head0=0, out_dtype=BF16, name="attn_a", **common)
    mixer_b = ([], [], [])
    for g, (_, dil) in enumerate(B_PATTERNS):
        src = b0 if g == 0 else b12
        first = 0 if g == 0 else (g - 1) * 3 * _N_B_PAIRS
        parts = _attn(slopes, src, src, src, (first, first + 1), (first + 2, first + 3), (first + 4, first + 5),
                      k_pair=(0, 1), dil=dil, head0=A_Q_HEADS + g * B_HEADS_PER_GROUP, out_dtype=F32,
                      name=f"attn_b{g}", **common)
        for acc, part in zip(mixer_b, parts):
            acc += part
    sink_lanes = jnp.repeat(sinks * LOG2E, HEAD_DIM).reshape(1, A_W)
    y_prompt = back(x1p, mixer_a, mixer_b, sink_lanes, gates, 512).reshape(batch, seq, D_MODEL)

    pairs = lambda a: [a[..., p * LANES:(p + 1) * LANES] for p in range(a.shape[-1] // LANES)]
    one, zero = jnp.ones((dec, LANES), F32), jnp.zeros((dec, LANES), F32)
    ob_s = [pair for g in range(B_N_GROUPS) for pair in pairs(ob_s[g])]
    lse_s = [pair for g in range(B_N_GROUPS) for pair in pairs(lse_s[g])]
    y_sample = back(x1s, (pairs(oa_s.astype(BF16)), [one] * _N_A_PAIRS, [zero] * _N_A_PAIRS),
                    (ob_s, [one] * len(ob_s), lse_s), jnp.full((1, A_W), NEG, F32), gates_s,
                    dec).reshape(dec, 1, D_MODEL)

    return (y_prompt, y_sample,
            _state_view(ta, A_KV_HEADS), _state_view(tb0, B_HEADS_PER_GROUP),
            _state_view(tb1, B_HEADS_PER_GROUP), _state_view(tb2, B_HEADS_PER_GROUP),
            _state_view(na, A_KV_HEADS), _state_view(nb0, B_HEADS_PER_GROUP),
            _state_view(nb1, B_HEADS_PER_GROUP), _state_view(nb2, B_HEADS_PER_GROUP))
```

```python
import functools
import math

import jax
import jax.numpy as jnp
from jax import lax
from jax.experimental import pallas as pl
from jax.experimental.pallas import tpu as pltpu

D_MODEL = 1024
D_FF = 2816
HEAD_DIM = 64
A_Q_HEADS = 8
A_KV_HEADS = 2
A_GROUP = A_Q_HEADS // A_KV_HEADS
B_PATTERNS = ((128, 1), (512, 4), (2048, 16))
B_HEADS_PER_GROUP = 4
B_N_GROUPS = 3
N_ALIBI_HEADS = A_Q_HEADS + B_N_GROUPS * B_HEADS_PER_GROUP
BLOCK = 128
EPS = 1e-6
ATTN_SCALE = HEAD_DIM ** -0.5
LOG2E = math.log2(math.e)
A_W = A_Q_HEADS * HEAD_DIM
A_KVW = A_KV_HEADS * HEAD_DIM
B_GW = B_HEADS_PER_GROUP * HEAD_DIM
B_W = B_N_GROUPS * B_GW
QKV_W = A_W + 2 * A_KVW + 3 * B_W
IN_W = QKV_W + 2 * D_MODEL

LANES = 128
CHUNK = 256
NEG = -1e30
VMEM_LIMIT = 56 * 1024 * 1024

F32 = jnp.float32
BF16 = jnp.bfloat16


def _const_spec(shape):
    nd = len(shape)
    return pl.BlockSpec(shape, lambda *_: (0,) * nd, pipeline_mode=pl.Buffered(1))


def _smem_spec():
    return pl.BlockSpec(memory_space=pltpu.SMEM)


def _params(n_axes):
    return pltpu.CompilerParams(
        dimension_semantics=("arbitrary",) * n_axes, vmem_limit_bytes=VMEM_LIMIT)


def _dot(a, b):
    return jnp.dot(a, b, preferred_element_type=F32)


def _dot_nt(a, b):
    return lax.dot_general(a, b, (((1,), (1,)), ((), ())), preferred_element_type=F32)


def _swiglu_residual(x, gain, wg_ref, wu_ref, wd_ref, act_ref, side_work=()):
    n_chunks = D_FF // CHUNK
    assert len(side_work) <= n_chunks
    h = (x * gain).astype(BF16)
    inv_rms = lax.rsqrt(jnp.mean(x * x, axis=-1, keepdims=True) + EPS)
    for c in range(n_chunks):
        if c < len(side_work):
            side_work[c]()
        sl = slice(c * CHUNK, (c + 1) * CHUNK)
        g = _dot(h, wg_ref[:, sl]) * inv_rms
        u = _dot(h, wu_ref[:, sl]) * inv_rms
        act_ref[:, sl] = (g * jax.nn.sigmoid(g) * u).astype(BF16)
    return x + 0.5 * _dot(act_ref[...], wd_ref[...])


_WIN_CAST_BLOCKS = 8


def _ffn_cast_kernel(x_ref, gain_ref, wg_ref, wu_ref, wd_ref, win_ref,
                     y_ref, wg_out, wu_out, wd_out, win_out, h_ref, inv_rms_ref, acc_ref):
    c = pl.program_id(0)

    @pl.when(c == 0)
    def _():
        x = x_ref[...]
        h_ref[...] = (x * gain_ref[...]).astype(BF16)
        inv_rms = lax.rsqrt(jnp.mean(x * x, axis=-1, keepdims=True) + EPS)
        inv_rms_ref[...] = jnp.broadcast_to(inv_rms, inv_rms_ref.shape)
        acc_ref[...] = jnp.zeros(acc_ref.shape, F32)

    wg, wu, wd = wg_ref[...].astype(BF16), wu_ref[...].astype(BF16), wd_ref[...].astype(BF16)
    wg_out[...], wu_out[...], wd_out[...] = wg, wu, wd
    win_out[...] = win_ref[...].astype(BF16)
    inv_rms = inv_rms_ref[:, 0:1]
    g = _dot(h_ref[...], wg) * inv_rms
    u = _dot(h_ref[...], wu) * inv_rms
    acc_ref[...] += _dot((g * jax.nn.sigmoid(g) * u).astype(BF16), wd)

    @pl.when(c == pl.num_programs(0) - 1)
    def _():
        y_ref[...] = x_ref[...] + 0.5 * acc_ref[...]


def _ffn_cast(x, gain, wg, wu, wd, w_in):
    t = x.shape[0]
    n_chunks = D_FF // CHUNK
    assert n_chunks >= _WIN_CAST_BLOCKS and D_MODEL % (_WIN_CAST_BLOCKS * _BF16_ROWS) == 0
    col_spec = lambda: pl.BlockSpec((D_MODEL, CHUNK), lambda c: (0, c))
    row_spec = lambda: pl.BlockSpec((CHUNK, D_MODEL), lambda c: (c, 0))
    win_spec = lambda: pl.BlockSpec((D_MODEL // _WIN_CAST_BLOCKS, IN_W),
                                    lambda c: (jnp.minimum(c, _WIN_CAST_BLOCKS - 1), 0))
    return pl.pallas_call(
        _ffn_cast_kernel,
        grid=(n_chunks,),
        in_specs=[_const_spec((t, D_MODEL)), _const_spec((1, D_MODEL)), col_spec(), col_spec(), row_spec(),
                  win_spec()],
        out_specs=[pl.BlockSpec((t, D_MODEL), lambda c: (0, 0)), col_spec(), col_spec(), row_spec(), win_spec()],
        out_shape=[jax.ShapeDtypeStruct((t, D_MODEL), F32), jax.ShapeDtypeStruct(wg.shape, BF16),
                   jax.ShapeDtypeStruct(wu.shape, BF16), jax.ShapeDtypeStruct(wd.shape, BF16),
                   jax.ShapeDtypeStruct(w_in.shape, BF16)],
        scratch_shapes=[pltpu.VMEM((t, D_MODEL), BF16), pltpu.VMEM((t, LANES), F32), pltpu.VMEM((t, D_MODEL), F32)],
        compiler_params=_params(1),
        name="ffn1_cast",
    )(x, gain, wg, wu, wd, w_in)


_N_QKV_CHUNKS = QKV_W // CHUNK
_N_CHUNKS = IN_W // CHUNK
_LAST_CHUNK = 9


def _tail_plan(tail, tm, n_tiles):
    if tail >= tm:
        return tm, n_tiles - tail // tm
    return tail, n_tiles - 1


_N_PROJ_INPUTS = 5
_N_PROJ_OUTPUTS = 9
_BF16_ROWS = 16


def _proj_kernel(*refs, tm, n_tiles, tails):
    n_riders = (len(refs) - _N_PROJ_INPUTS - _N_PROJ_OUTPUTS) // 2
    x_ref, gain_ref, w_ref, qkgain_ref, ones_ref = refs[:_N_PROJ_INPUTS]
    rider_in = refs[_N_PROJ_INPUTS:_N_PROJ_INPUTS + n_riders]
    outs = refs[_N_PROJ_INPUTS + n_riders:]
    qa_ref, kva_ref, b0_ref, b12_ref, gates_ref, ta_ref, tb0_ref, tb1_ref, tb2_ref = outs[:_N_PROJ_OUTPUTS]
    for src, dst in zip(rider_in, outs[_N_PROJ_OUTPUTS:]):
        dst[...] = src[...].astype(BF16)
    x = x_ref[...]
    h = (x * gain_ref[...]).astype(BF16)
    inv_rms = lax.rsqrt(jnp.mean(x * x, axis=-1, keepdims=True) + EPS)
    lane = lax.broadcasted_iota(jnp.int32, (tm, CHUNK), 1)
    lane1 = lax.broadcasted_iota(jnp.int32, (tm, LANES), 1)

    def z_chunk(c):
        return _dot(h, w_ref[:, c * CHUNK:(c + 1) * CHUNK]) * inv_rms

    def head_norm(z, c):
        ss = _dot((z * z).astype(BF16), ones_ref[...])
        return z * lax.rsqrt(ss * (1.0 / HEAD_DIM) + EPS) * qkgain_ref[:, c * CHUNK:(c + 1) * CHUNK]

    def write_tail(t_ref, row0, y, tail):
        block_w, _ = _tail_plan(tail, tm, n_tiles)
        data = y if block_w == tm else y[tm - block_w:, :]
        t_ref[row0:row0 + CHUNK, :] = data.T

    tb_refs = (tb0_ref, tb1_ref, tb2_ref)

    def consume(c, z):
        if c < 2:
            qa_ref[:, c * CHUNK:(c + 1) * CHUNK] = head_norm(z, c).astype(BF16)
        elif c == 2:
            y = jnp.where(lane < A_KVW, head_norm(z, c), z)
            write_tail(ta_ref, 0, y, tails[0])
            for part in range(2):
                pair = y[:, part * LANES:(part + 1) * LANES]
                swapped = pltpu.roll(pair, HEAD_DIM, axis=1)
                base = part * 2 * LANES
                kva_ref[:, base:base + LANES] = jnp.where(lane1 < HEAD_DIM, pair, swapped).astype(BF16)
                kva_ref[:, base + LANES:base + 2 * LANES] = jnp.where(lane1 < HEAD_DIM, swapped, pair).astype(BF16)
        elif c < _N_QKV_CHUNKS:
            kind, g = divmod(c - 3, B_N_GROUPS)
            y = z if kind == 2 else head_norm(z, c)
            if g == 0:
                b0_ref[:, kind * CHUNK:(kind + 1) * CHUNK] = y.astype(BF16)
            else:
                col = ((g - 1) * 3 + kind) * CHUNK
                b12_ref[:, col:col + CHUNK] = y
            if kind > 0:
                write_tail(tb_refs[g], (kind - 1) * CHUNK, y, tails[1 + g])
        else:
            col = (c - _N_QKV_CHUNKS) * CHUNK
            gates_ref[:, col:col + CHUNK] = jax.nn.sigmoid(z).astype(BF16)

    order = list(range(_N_QKV_CHUNKS, _N_CHUNKS)) + [c for c in range(_N_QKV_CHUNKS) if c != _LAST_CHUNK]
    order.append(_LAST_CHUNK)
    z_next = z_chunk(order[0])
    for i, c in enumerate(order):
        z = z_next
        if i + 1 < len(order):
            z_next = z_chunk(order[i + 1])
        consume(c, z)


def _rider_specs(weights, n_steps, n_tiles):
    specs = []
    for w in weights:
        rows, cols = w.shape
        n_blocks = n_steps
        while rows % (n_blocks * _BF16_ROWS):
            assert n_blocks % 2 == 0, (rows, n_steps)
            n_blocks //= 2
        per = n_steps // n_blocks
        specs.append(pl.BlockSpec((rows // n_blocks, cols), lambda n, j, per=per: ((n * n_tiles + j) // per, 0)))
    return specs


def _proj(x, gain, w_in, qkgain, ones_bd, n_seq, seq, tm, riders=()):
    t = n_seq * seq
    n_tiles = seq // tm
    rider_specs = _rider_specs(riders, n_seq * n_tiles, n_tiles)
    tails = (min(128, seq),) + tuple(min(w, seq) for w, _ in B_PATTERNS)
    tail_rows = (2 * A_KVW, 2 * B_GW, 2 * B_GW, 2 * B_GW)

    def tail_spec(rows, tail):
        block_w, first = _tail_plan(tail, tm, n_tiles)
        return pl.BlockSpec((None, rows, block_w), lambda n, j: (n, 0, jnp.maximum(j - first, 0)))

    row_spec = lambda w: pl.BlockSpec((tm, w), lambda n, j: (n * n_tiles + j, 0))
    out_shape = [
        jax.ShapeDtypeStruct((t, A_W), BF16),
        jax.ShapeDtypeStruct((t, 4 * LANES), BF16),
        jax.ShapeDtypeStruct((t, 3 * B_GW), BF16),
        jax.ShapeDtypeStruct((t, 6 * B_GW), F32),
        jax.ShapeDtypeStruct((t, 2 * D_MODEL), BF16),
    ] + [jax.ShapeDtypeStruct((n_seq, r, tl), F32) for r, tl in zip(tail_rows, tails)]
    out_specs = [row_spec(A_W), row_spec(4 * LANES), row_spec(3 * B_GW), row_spec(6 * B_GW),
                 row_spec(2 * D_MODEL)] + [tail_spec(r, tl) for r, tl in zip(tail_rows, tails)]
    assert len(out_shape) == _N_PROJ_OUTPUTS
    out_shape += [jax.ShapeDtypeStruct(w.shape, BF16) for w in riders]
    outs = pl.pallas_call(
        functools.partial(_proj_kernel, tm=tm, n_tiles=n_tiles, tails=tails),
        grid=(n_seq, n_tiles),
        in_specs=[row_spec(D_MODEL), _const_spec((1, D_MODEL)), _const_spec((D_MODEL, IN_W)),
                  _const_spec((1, QKV_W)), _const_spec((CHUNK, CHUNK))] + rider_specs,
        out_specs=out_specs + _rider_specs(riders, n_seq * n_tiles, n_tiles),
        out_shape=out_shape,
        compiler_params=_params(2),
        name="proj",
    )(x, gain, w_in, qkgain, ones_bd, *riders)
    return outs[:_N_PROJ_OUTPUTS], outs[_N_PROJ_OUTPUTS:]


_HEADS_PER_ITER = 128


def _attn_kernel(*refs, n_pairs, k_pair, dil, seq, head0):
    refs = list(refs)
    slopes_ref = refs.pop(0)
    n_kv = max(k_pair) + 1
    take = lambda count: [refs.pop(0) for _ in range(count)]
    q_refs, k_refs, v_refs = take(n_pairs), take(n_kv), take(n_kv)
    o_refs, den_refs, max_refs = take(n_pairs), take(n_pairs), take(n_pairs)
    (bias_ref,) = refs

    n_blocks = seq // (BLOCK * dil)
    use_prev = n_blocks > 1
    n_heads = 2 * n_pairs
    qi = lax.broadcasted_iota(jnp.int32, (BLOCK, BLOCK), 0)
    ki = lax.broadcasted_iota(jnp.int32, (BLOCK, BLOCK), 1)
    d_cur = qi - ki
    lo = ki < HEAD_DIM
    lo_keys = lax.broadcasted_iota(jnp.int32, ((2 if use_prev else 1) * BLOCK, LANES), 1) < HEAD_DIM

    for h in range(n_heads):
        slope = slopes_ref[head0 + h] * (float(dil) * LOG2E)
        cur_bias = jnp.where(d_cur >= 0, -slope * d_cur.astype(F32), NEG)
        if use_prev:
            prev_bias = jnp.where(d_cur <= 0, -slope * (d_cur + BLOCK).astype(F32), NEG)
            bias_ref[h] = jnp.concatenate([prev_bias, cur_bias], axis=1)
            bias_ref[n_heads + h] = jnp.concatenate([jnp.full_like(prev_bias, NEG), cur_bias], axis=1)
        else:
            bias_ref[h] = cur_bias

    def block_rows(step):
        if n_blocks == 1:
            r, j = step, 0
        else:
            r, j = step // n_blocks, step % n_blocks
        if dil == 1:
            cur = pl.ds(pl.multiple_of(j * BLOCK, BLOCK), BLOCK)
            prev = pl.ds(pl.multiple_of(jnp.maximum(j - 1, 0) * BLOCK, BLOCK), BLOCK)
        else:
            cur = pl.ds(r + j * (BLOCK * dil), BLOCK, stride=dil)
            prev = pl.ds(r + jnp.maximum(j - 1, 0) * (BLOCK * dil), BLOCK, stride=dil)
        table = jnp.where(j == 0, n_heads, 0) if use_prev else 0
        return cur, prev, table

    blocks_per_iter = min(_HEADS_PER_ITER // (2 * n_pairs), dil * n_blocks)
    assert (dil * n_blocks) % blocks_per_iter == 0

    def body(it, carry):
        blocks = [block_rows(it * blocks_per_iter + b) for b in range(blocks_per_iter)]
        scores, values = {}, {}
        for b, (cur, prev, _) in enumerate(blocks):
            for p in range(n_pairs):
                k_ref, v_ref = k_refs[k_pair[p]], v_refs[k_pair[p]]
                q = q_refs[p][cur, :].astype(BF16)
                keys = k_ref[cur, :].astype(BF16)
                vals = v_ref[cur, :].astype(BF16)
                if use_prev:
                    keys = jnp.concatenate([k_ref[prev, :].astype(BF16), keys], axis=0)
                    vals = jnp.concatenate([v_ref[prev, :].astype(BF16), vals], axis=0)
                one = jnp.ones_like(vals)
                values[b, p, 0] = jnp.where(lo_keys, vals, one)
                values[b, p, 1] = jnp.where(lo_keys, one, vals)
                for half in range(2):
                    qm = jnp.where(lo if half == 0 else jnp.logical_not(lo), q, jnp.zeros_like(q))
                    scores[b, p, half] = _dot_nt(qm, keys)
        probs, row_max = {}, {}
        for (b, p, half), s in scores.items():
            s = s + bias_ref[blocks[b][2] + 2 * p + half]
            m = jnp.max(s, axis=-1, keepdims=True)
            probs[b, p, half], row_max[b, p, half] = jnp.exp2(s - m).astype(BF16), m
        for b, (cur, _, _) in enumerate(blocks):
            for p in range(n_pairs):
                acc = [_dot(probs[b, p, half], values[b, p, half]) for half in range(2)]
                o_refs[p][cur, :] = jnp.where(lo, acc[0], acc[1]).astype(o_refs[p].dtype)
                den_refs[p][cur, :] = jnp.where(lo, acc[1], acc[0])
                max_refs[p][cur, :] = jnp.where(lo, row_max[b, p, 0], row_max[b, p, 1])
        return carry

    lax.fori_loop(0, dil * n_blocks // blocks_per_iter, body, 0)


def _attn(slopes, q_arr, k_arr, v_arr, q_idx, k_idx, v_idx, *, n_seq, seq, k_pair, dil, head0, out_dtype, name):
    n_pairs = len(q_idx)

    def pair_spec(idx):
        return pl.BlockSpec((None, seq, LANES), lambda n: (n, 0, idx))

    view = lambda a: a.reshape(n_seq, seq, a.shape[-1])
    operands = [slopes] + [view(q_arr)] * n_pairs + [view(k_arr)] * len(k_idx) + [view(v_arr)] * len(v_idx)
    in_specs = [_smem_spec()] + [pair_spec(i) for i in tuple(q_idx) + tuple(k_idx) + tuple(v_idx)]
    out_shape = [jax.ShapeDtypeStruct((n_seq, seq, LANES), out_dtype)] * n_pairs
    out_shape += [jax.ShapeDtypeStruct((n_seq, seq, LANES), F32)] * (2 * n_pairs)
    outs = pl.pallas_call(
        functools.partial(_attn_kernel, n_pairs=n_pairs, k_pair=k_pair, dil=dil, seq=seq, head0=head0),
        grid=(n_seq,),
        in_specs=in_specs,
        out_specs=[pair_spec(0)] * len(out_shape),
        out_shape=out_shape,
        scratch_shapes=[pltpu.VMEM((2 * n_pairs, BLOCK, BLOCK) if seq == BLOCK * dil else
                                   (4 * n_pairs, BLOCK, 2 * BLOCK), F32)],
        compiler_params=_params(1),
        name=name,
    )(*operands)
    outs = [o.reshape(n_seq * seq, LANES) for o in outs]
    return outs[:n_pairs], outs[n_pairs:2 * n_pairs], outs[2 * n_pairs:]


def _rows_where(row_iota, values):
    out = jnp.zeros(row_iota.shape, F32)
    for r, v in enumerate(values):
        out = jnp.where(row_iota == r, v, out)
    return out


def _shift_cache(cache_ref, tail_ref, out_ref, n, length, rows):
    shifted = pltpu.roll(cache_ref[rows, :], length - 1, axis=1)
    new_col = pltpu.roll(tail_ref[rows, :], (LANES - 1) - n, axis=1)
    lane = lax.broadcasted_iota(jnp.int32, new_col.shape, 1)
    last = jnp.where(lane == LANES - 1, new_col, shifted[:, length - LANES:])
    out_ref[rows, :] = last if length == LANES else jnp.concatenate([shifted[:, :length - LANES], last], axis=1)


def _sample_stages(slopes_ref, sinks_ref, qa_ref, kva_ref, b0_ref, b12_ref,
                   ta_ref, tb0_ref, tb1_ref, tb2_ref, ca_ref, cb0_ref, cb1_ref, cb2_ref,
                   oa_ref, ob_ref, lse_ref, na_ref, nb0_ref, nb1_ref, nb2_ref):
    n = pl.program_id(0)
    row = pl.ds(n, 1)
    sub = lax.broadcasted_iota(jnp.int32, (8, LANES), 0)
    lane = lax.broadcasted_iota(jnp.int32, (8, LANES), 1)
    half = lane // HEAD_DIM
    kv_head = sub // A_GROUP
    sub_b = lax.broadcasted_iota(jnp.int32, (8, B_GW), 0)
    head_mask = (lax.broadcasted_iota(jnp.int32, (8, B_GW), 1) // HEAD_DIM) == sub_b
    b_refs = ((cb0_ref, tb0_ref, nb0_ref), (cb1_ref, tb1_ref, nb1_ref), (cb2_ref, tb2_ref, nb2_ref))
    state = {}

    def scores():
        q_row = qa_ref[row, :]
        q_blk = jnp.zeros((8, LANES), F32)
        for r in range(A_Q_HEADS):
            k, g = divmod(r, A_GROUP)
            chunk = q_row[:, (r // 2) * LANES:(r // 2 + 1) * LANES]
            if g % 2 != k:
                chunk = pltpu.roll(chunk, HEAD_DIM, axis=1)
            q_blk = jnp.where((sub == r) & (half == k), jnp.broadcast_to(chunk, (8, LANES)), q_blk)
        kva = kva_ref[row, :]
        k_new = jnp.where(lane[:1] < HEAD_DIM, kva[:, 0:LANES], kva[:, LANES:2 * LANES])
        v_new = jnp.where(lane[:1] < HEAD_DIM, kva[:, 2 * LANES:3 * LANES], kva[:, 3 * LANES:])
        s = _dot(q_blk.astype(BF16), ca_ref[0:A_KVW, :].astype(BF16))
        state["a"] = (s, jnp.sum(q_blk * k_new, axis=-1, keepdims=True), v_new)
        for g, (c_ref, _, _) in enumerate(b_refs):
            qkv = b0_ref[row, :] if g == 0 else b12_ref[row, (g - 1) * 3 * B_GW:g * 3 * B_GW]
            q_row, k_new, v_new = qkv[:, 0:B_GW], qkv[:, B_GW:2 * B_GW], qkv[:, 2 * B_GW:]
            q_blk = jnp.where(head_mask, jnp.broadcast_to(q_row, (8, B_GW)), 0.0)
            q_col = jnp.broadcast_to(q_row, (8, B_GW)).T[:, 0:1]
            length = c_ref.shape[1]
            width = min(length, _DECODE_CHUNK)
            pieces = []
            for t in range(length // width):
                cols = slice(t * width, (t + 1) * width)
                prod = (c_ref[0:B_GW, cols] * q_col).reshape(B_HEADS_PER_GROUP, HEAD_DIM, width)
                pieces.append(jnp.sum(prod, axis=1))
            s4 = pieces[0] if len(pieces) == 1 else jnp.concatenate(pieces, axis=1)
            s = jnp.concatenate([s4, jnp.zeros_like(s4)], axis=0)
            state["b", g] = (s, jnp.sum(q_blk * k_new, axis=-1, keepdims=True), v_new)

    def softmaxes():
        s, s_new, v_new = state["a"]
        slope = _rows_where(sub[:, :1], [slopes_ref[r] * LOG2E for r in range(A_Q_HEADS)])
        sink = _rows_where(sub[:, :1], [sinks_ref[r] * LOG2E for r in range(A_Q_HEADS)])
        s = s - slope * (BLOCK - lane).astype(F32)
        m = jnp.maximum(jnp.maximum(jnp.max(s, axis=-1, keepdims=True), s_new), sink)
        e = jnp.exp2(s - m)
        e_new = jnp.exp2(s_new - m)
        denom = jnp.sum(e, axis=-1, keepdims=True) + e_new + jnp.exp2(sink - m)
        state["a"] = (e.astype(BF16), e_new, v_new, m, denom)
        for g, (length, dil) in enumerate(B_PATTERNS):
            s, s_new, v_new = state["b", g]
            head0 = A_Q_HEADS + g * B_HEADS_PER_GROUP
            sub_l = lax.broadcasted_iota(jnp.int32, (8, length), 0)
            col = lax.broadcasted_iota(jnp.int32, (8, length), 1)
            slope = _rows_where(sub_l, [slopes_ref[head0 + h] * LOG2E for h in range(B_HEADS_PER_GROUP)])
            s = jnp.where((col & (dil - 1)) == 0, s - slope * (length - col).astype(F32), NEG)
            m = jnp.maximum(jnp.max(s, axis=-1, keepdims=True), s_new)
            e = jnp.exp2(s - m)
            e_new = jnp.exp2(s_new - m)
            denom = jnp.sum(e, axis=-1, keepdims=True) + e_new
            state["b", g] = (e, e_new, v_new, m, denom)

    def values():
        e, e_new, v_new, m, denom = state["a"]
        acc = _dot_nt(e, ca_ref[A_KVW:2 * A_KVW, :].astype(BF16))
        out = jnp.where(half == kv_head, (acc + e_new * v_new) / denom, 0.0)
        for c in range(A_W // LANES):
            k = c // 2
            pieces = []
            for hh in range(2):
                r = k * A_GROUP + 2 * (c % 2) + hh
                piece = out[r:r + 1, :]
                pieces.append(piece if hh == k else pltpu.roll(piece, HEAD_DIM, axis=1))
            oa_ref[:, c * LANES:(c + 1) * LANES] = jnp.where(lane[:1] < HEAD_DIM, pieces[0], pieces[1])
        for g, (c_ref, _, _) in enumerate(b_refs):
            e, e_new, v_new, m, denom = state["b", g]
            length = c_ref.shape[1]
            width = min(length, _DECODE_CHUNK)
            acc_col = jnp.zeros((B_GW, 1), F32)
            for t in range(length // width):
                cols = slice(t * width, (t + 1) * width)
                weights = jnp.broadcast_to(e[0:B_HEADS_PER_GROUP, None, cols], (B_HEADS_PER_GROUP, HEAD_DIM, width))
                acc_col = acc_col + jnp.sum(c_ref[B_GW:2 * B_GW, cols] * weights.reshape(B_GW, width),
                                            axis=1, keepdims=True)
            acc = jnp.broadcast_to(jnp.broadcast_to(acc_col, (B_GW, 8)).T[0:1, :], (8, B_GW))
            out = jnp.where(head_mask, (acc + e_new * v_new) / denom, 0.0)
            ob_ref[g] = jnp.sum(out, axis=0, keepdims=True)
            lse = jnp.where(head_mask, m + jnp.log2(denom), 0.0)
            lse_ref[g] = jnp.sum(lse, axis=0, keepdims=True)

    def shift(c_ref, t_ref, n_ref, length, piece, n_pieces):
        rows_per = c_ref.shape[0] // n_pieces
        return lambda: _shift_cache(c_ref, t_ref, n_ref, n, length, slice(piece * rows_per, (piece + 1) * rows_per))

    def shift_small():
        shift(ca_ref, ta_ref, na_ref, BLOCK, 0, 1)()
        shift(cb0_ref, tb0_ref, nb0_ref, B_PATTERNS[0][0], 0, 1)()

    shift_b1 = shift(cb1_ref, tb1_ref, nb1_ref, B_PATTERNS[1][0], 0, 1)
    shift_b2 = [shift(cb2_ref, tb2_ref, nb2_ref, B_PATTERNS[2][0], piece, _N_SHIFT_PIECES)
                for piece in range(_N_SHIFT_PIECES)]
    return [scores, shift_small, softmaxes, shift_b1, values] + shift_b2


_DECODE_CHUNK = 128
_N_SHIFT_PIECES = 4


_N_FFN_INPUTS = 5
_N_SAMPLE_INPUTS = 14
_N_SAMPLE_OUTPUTS = 7


def _ffn_sample_kernel(*refs):
    ffn_in, refs = refs[:_N_FFN_INPUTS], refs[_N_FFN_INPUTS:]
    sample_in, refs = refs[:_N_SAMPLE_INPUTS], refs[_N_SAMPLE_INPUTS:]
    o_ref, sample_out, (act_ref,) = refs[0], refs[1:1 + _N_SAMPLE_OUTPUTS], refs[1 + _N_SAMPLE_OUTPUTS:]
    x_ref, gain_ref, wg_ref, wu_ref, wd_ref = ffn_in
    o_ref[...] = _swiglu_residual(x_ref[...], gain_ref[...], wg_ref, wu_ref, wd_ref, act_ref,
                                  side_work=_sample_stages(*sample_in, *sample_out))


def _ffn_and_sample(x, gain, wg, wu, wd, slopes, sinks, qa, kva, b0, b12, tails, caches):
    n_seq = qa.shape[0]
    t = x.shape[0]
    assert t % n_seq == 0
    tm = t // n_seq
    full = lambda a: pl.BlockSpec(a.shape, lambda n: (0,) * a.ndim)
    per_seq = lambda a: pl.BlockSpec((None,) + a.shape[1:], lambda n: (n, 0, 0))
    small = [qa, kva, b0, b12] + list(tails)
    assert 2 + len(small) + len(caches) == _N_SAMPLE_INPUTS
    out_shape = [jax.ShapeDtypeStruct((t, D_MODEL), F32),
                 jax.ShapeDtypeStruct((n_seq, 1, A_W), F32),
                 jax.ShapeDtypeStruct((B_N_GROUPS, n_seq, 1, B_GW), F32),
                 jax.ShapeDtypeStruct((B_N_GROUPS, n_seq, 1, B_GW), F32)]
    out_shape += [jax.ShapeDtypeStruct(c.shape, F32) for c in caches]
    out_specs = [pl.BlockSpec((tm, D_MODEL), lambda n: (n, 0)),
                 pl.BlockSpec((None, 1, A_W), lambda n: (n, 0, 0)),
                 pl.BlockSpec((B_N_GROUPS, None, 1, B_GW), lambda n: (0, n, 0, 0)),
                 pl.BlockSpec((B_N_GROUPS, None, 1, B_GW), lambda n: (0, n, 0, 0))]
    out_specs += [per_seq(c) for c in caches]
    x1, oa, ob, lse, *new_caches = pl.pallas_call(
        _ffn_sample_kernel,
        grid=(n_seq,),
        in_specs=[pl.BlockSpec((tm, D_MODEL), lambda n: (n, 0)),
                  _const_spec((1, D_MODEL)), _const_spec((D_MODEL, D_FF)),
                  _const_spec((D_MODEL, D_FF)), _const_spec((D_FF, D_MODEL))]
                 + [_smem_spec(), _smem_spec()] + [full(a) for a in small] + [per_seq(c) for c in caches],
        out_specs=out_specs,
        out_shape=out_shape,
        scratch_shapes=[pltpu.VMEM((tm, D_FF), BF16)],
        compiler_params=_params(1),
        name="ffn1_sample",
    )(x, gain, wg, wu, wd, slopes, sinks, *small, *caches)
    return (x1, oa.reshape(n_seq, A_W), ob.reshape(B_N_GROUPS, n_seq, B_GW),
            lse.reshape(B_N_GROUPS, n_seq, B_GW), *new_caches)


_N_A_PAIRS = A_W // LANES
_N_B_PAIRS = B_GW // LANES


def _merge_kernel(*refs):
    refs = list(refs)
    take = lambda count: [refs.pop(0) for _ in range(count)]
    (x_ref,), oa_refs, da_refs, ma_refs = take(1), take(_N_A_PAIRS), take(_N_A_PAIRS), take(_N_A_PAIRS)
    n_b = B_N_GROUPS * _N_B_PAIRS
    ob_refs, db_refs, mb_refs = take(n_b), take(n_b), take(n_b)
    sinks_ref, gates_ref, wua_ref, wub_ref, wo_ref, gain_ref, wg_ref, wu_ref, wd_ref, y_ref, act_ref = refs

    unswap = lambda ref: pltpu.roll(ref[...], HEAD_DIM, axis=1)
    oa_pairs = []
    for p, (o_ref, d_ref, m_ref) in enumerate(zip(oa_refs, da_refs, ma_refs)):
        denom = unswap(d_ref) + jnp.exp2(sinks_ref[:, p * LANES:(p + 1) * LANES] - m_ref[...])
        oa_pairs.append((o_ref[...].astype(F32) / denom).astype(BF16))
    ob_pairs = []
    for p in range(_N_B_PAIRS):
        idx = [g * _N_B_PAIRS + p for g in range(B_N_GROUPS)]
        maxes = [mb_refs[i][...] for i in idx]
        top = functools.reduce(jnp.maximum, maxes)
        weights = [jnp.exp2(m - top) for m in maxes]
        num = sum(w * ob_refs[i][...] for w, i in zip(weights, idx))
        den = sum(w * unswap(db_refs[i]) for w, i in zip(weights, idx))
        ob_pairs.append((num / den).astype(BF16))
    ua = _dot(jnp.concatenate(oa_pairs, axis=1), wua_ref[...])
    ub = _dot(jnp.concatenate(ob_pairs, axis=1), wub_ref[...])
    gate_a = gates_ref[:, :D_MODEL].astype(F32)
    gate_b = gates_ref[:, D_MODEL:].astype(F32)
    mixed = (gate_a * ua + gate_b * ub).astype(BF16)
    x = x_ref[...] + _dot(mixed, wo_ref[...])
    y_ref[...] = _swiglu_residual(x, gain_ref[...], wg_ref, wu_ref, wd_ref, act_ref)


def _merge(x, mixer_a, mixer_b, sink_lanes, gates, wua, wub, wo, gain, wg, wu, wd, tm):
    t = x.shape[0]
    row_spec = lambda w: pl.BlockSpec((tm, w), lambda i: (i, 0))
    pair_inputs = [a for part in tuple(mixer_a) + tuple(mixer_b) for a in part]
    n_pair_inputs = len(pair_inputs)
    assert n_pair_inputs == 3 * _N_A_PAIRS + 3 * B_N_GROUPS * _N_B_PAIRS
    return pl.pallas_call(
        _merge_kernel,
        grid=(t // tm,),
        in_specs=[row_spec(D_MODEL)] + [row_spec(LANES)] * n_pair_inputs + [_const_spec((1, A_W)),
                  row_spec(2 * D_MODEL), _const_spec((A_W, D_MODEL)), _const_spec((B_GW, D_MODEL)),
                  _const_spec((D_MODEL, D_MODEL)),
                  _const_spec((1, D_MODEL)), _const_spec((D_MODEL, D_FF)), _const_spec((D_MODEL, D_FF)),
                  _const_spec((D_FF, D_MODEL))],
        out_specs=row_spec(D_MODEL),
        out_shape=jax.ShapeDtypeStruct((t, D_MODEL), F32),
        scratch_shapes=[pltpu.VMEM((tm, D_FF), BF16)],
        compiler_params=_params(1),
        name="merge_ffn2",
    )(x, *pair_inputs, sink_lanes, gates, wua, wub, wo, gain, wg, wu, wd)


def _cache_view(cache):
    _, n, length, two, h, d = cache.shape
    return jnp.transpose(cache, (0, 1, 3, 4, 5, 2)).reshape(n, two * h * d, length)


def _state_view(rows_by_len, heads):
    n, _, length = rows_by_len.shape
    return jnp.transpose(rows_by_len.reshape(1, n, 2, heads, HEAD_DIM, length), (0, 1, 5, 2, 3, 4))


def kernel(x_prompt, x_sample, cache_a_kv, cache_b1_kv, cache_b2_kv, cache_b3_kv, norm_ffn1, w1_gate, w1_up,
           w1_down, norm_mix, w_in, q_norm_a, k_norm_a, q_norm_b, k_norm_b, sinks_a, w_up_a, w_up_b, w_o,
           norm_ffn2, w2_gate, w2_up, w2_down):
    assert x_prompt.shape[-1] == D_MODEL and w_in.shape == (1, D_MODEL, IN_W)
    batch, seq, _ = x_prompt.shape
    dec = x_sample.shape[0]
    assert x_sample.shape[1] == 1 and seq % (BLOCK * B_PATTERNS[-1][1]) == 0
    assert cache_a_kv.shape[2] == BLOCK
    assert all(c.shape[2] == w for c, (w, _) in zip((cache_b1_kv, cache_b2_kv, cache_b3_kv), B_PATTERNS))

    late_weights = [w[0] for w in (w_up_a, w_up_b, w_o, w2_gate, w2_up, w2_down)]

    i = jnp.arange(1, N_ALIBI_HEADS + 1, dtype=F32)
    slopes = jnp.exp2(-8.0 * i / N_ALIBI_HEADS)
    sinks = sinks_a[0].reshape(A_Q_HEADS).astype(F32)
    ones64 = jnp.ones((HEAD_DIM,), F32)
    q_scale = ATTN_SCALE * LOG2E
    qkgain = jnp.concatenate([
        jnp.tile(q_norm_a[0] * q_scale, A_Q_HEADS), jnp.tile(k_norm_a[0], A_KV_HEADS),
        jnp.tile(ones64, A_KV_HEADS),
        jnp.tile(q_norm_b[0] * q_scale, B_N_GROUPS * B_HEADS_PER_GROUP),
        jnp.tile(k_norm_b[0], B_N_GROUPS * B_HEADS_PER_GROUP),
        jnp.tile(ones64, B_N_GROUPS * B_HEADS_PER_GROUP)]).reshape(1, QKV_W).astype(F32)
    head_of = jnp.arange(CHUNK) // HEAD_DIM
    ones_bd = (head_of[:, None] == head_of[None, :]).astype(BF16)

    x1s, wg1, wu1, wd1, w_in_b = _ffn_cast(x_sample.reshape(dec, D_MODEL), norm_ffn1, w1_gate[0], w1_up[0],
                                           w1_down[0], w_in[0])

    def proj(x1, n_seq, s, tm, riders=()):
        return _proj(x1, norm_mix, w_in_b, qkgain, ones_bd, n_seq, s, tm, riders)

    (qa_s, kva_s, b0_s, b12_s, gates_s, *tails_s), _ = proj(x1s, 1, dec, dec)
    caches = [_cache_view(c) for c in (cache_a_kv, cache_b1_kv, cache_b2_kv, cache_b3_kv)]
    x1p, oa_s, ob_s, lse_s, na, nb0, nb1, nb2 = _ffn_and_sample(
        x_prompt.reshape(batch * seq, D_MODEL), norm_ffn1, wg1, wu1, wd1,
        slopes, sinks, qa_s.astype(F32), kva_s.astype(F32), b0_s.astype(F32), b12_s,
        [t[0] for t in tails_s], caches)

    (qa, kva, b0, b12, gates, ta, tb0, tb1, tb2), (wua, wub, wo, wg2, wu2, wd2) = proj(
        x1p, batch, seq, 512, late_weights)

    def back(x1, mixer_a, mixer_b, sink_lanes, gates, tm):
        return _merge(x1, mixer_a, mixer_b, sink_lanes, gates, wua, wub, wo, norm_ffn2, wg2, wu2, wd2, tm)

    common = dict(n_seq=batch, seq=seq)
    mixer_a = _attn(slopes, qa, kva, kva, (0, 1, 2, 3), (0, 1), (2, 3), k_pair=(0, 0, 1, 1), dil=1,
                    head0=0, out_dtype=BF16, name="attn_a", **common)
    mixer_b = ([], [], [])
    for g, (_, dil) in enumerate(B_PATTERNS):
        src = b0 if g == 0 else b12
        first = 0 if g == 0 else (g - 1) * 3 * _N_B_PAIRS
        parts = _attn(slopes, src, src, src, (first, first + 1), (first + 2, first + 3), (first + 4, first + 5),
                      k_pair=(0, 1), dil=dil, head0=A_Q_HEADS + g * B_HEADS_PER_GROUP, out_dtype=F32,
                      name=f"attn_b{g}", **common)
        for acc, part in zip(mixer_b, parts):
            acc += part
    sink_lanes = jnp.repeat(sinks * LOG2E, HEAD_DIM).reshape(1, A_W)
    y_prompt = back(x1p, mixer_a, mixer_b, sink_lanes, gates, 512).reshape(batch, seq, D_MODEL)

    pairs = lambda a: [a[..., p * LANES:(p + 1) * LANES] for p in range(a.shape[-1] // LANES)]
    one, zero = jnp.ones((dec, LANES), F32), jnp.zeros((dec, LANES), F32)
    ob_s = [pair for g in range(B_N_GROUPS) for pair in pairs(ob_s[g])]
    lse_s = [pair for g in range(B_N_GROUPS) for pair in pairs(lse_s[g])]
    y_sample = back(x1s, (pairs(oa_s.astype(BF16)), [one] * _N_A_PAIRS, [zero] * _N_A_PAIRS),
                    (ob_s, [one] * len(ob_s), lse_s), jnp.full((1, A_W), NEG, F32), gates_s,
                    dec).reshape(dec, 1, D_MODEL)

    return (y_prompt, y_sample,
            _state_view(ta, A_KV_HEADS), _state_view(tb0, B_HEADS_PER_GROUP),
            _state_view(tb1, B_HEADS_PER_GROUP), _state_view(tb2, B_HEADS_PER_GROUP),
            _state_view(na, A_KV_HEADS), _state_view(nb0, B_HEADS_PER_GROUP),
            _state_view(nb1, B_HEADS_PER_GROUP), _state_view(nb2, B_HEADS_PER_GROUP))
```
